```python
import math
import jax
import jax.numpy as jnp
from jax import lax
import numpy as np

D_MODEL = 1024
BATCH = 8
SEQ = 2048
DEPTH = 2

N_META = 16
RMS_EPS = 1e-6
CONV_WIDTH = 4
SSD_HEADS = 16
SSD_HEAD_DIM = 64
SSD_WIDTH = SSD_HEADS * SSD_HEAD_DIM
SSD_GROUPS = 2
SSD_HEADS_PER_GROUP = SSD_HEADS // SSD_GROUPS
SSD_STATE = 128
SSD_CHUNK = 128
SSD_CONV_DIM = SSD_WIDTH + 2 * SSD_GROUPS * SSD_STATE
LRU_WIDTH = D_MODEL
LRU_HEADS = 16
LRU_BLOCK = LRU_WIDTH // LRU_HEADS
LRU_C = 8.0
PROJ_COLS = SSD_WIDTH + SSD_CONV_DIM + SSD_HEADS + 2 * LRU_WIDTH
MIX_WIDTH = SSD_WIDTH + LRU_WIDTH
POOL_WINDOWS = (2, 4, 8, 16)
POOL_GROUPS = len(POOL_WINDOWS)
POOL_GROUP_DIM = D_MODEL // POOL_GROUPS
POOL_MAX_WINDOW = 16
MOE_GROUPS = 4
MOE_EXPERTS_PER_GROUP = 4
MOE_EXPERTS = MOE_GROUPS * MOE_EXPERTS_PER_GROUP
MOE_TOP_K = 2
D_EXPERT = D_MODEL // 2
N_EVEN = (DEPTH + 1) // 2
N_ODD = DEPTH // 2

kernel_name = "hybrid_ssd_rglru_pool_hmoe_meta"


def rms_norm(x, gain):
    xf = x.astype(jnp.float32)
    y = xf * lax.rsqrt(jnp.mean(xf * xf, axis=-1, keepdims=True) + RMS_EPS)
    return (y * gain.astype(jnp.float32)).astype(x.dtype)


def causal_depthwise_conv(x, w, b):
    c = x.shape[-1]
    y = lax.conv_general_dilated(
        x, w[:, None, :].astype(x.dtype), (1,), [(CONV_WIDTH - 1, 0)],
        dimension_numbers=("NWC", "WIO", "NWC"), feature_group_count=c)
    return y + b.astype(x.dtype)


def ssd_chunked_scan(xdt, adt, b_mat, c_mat):
    bsz, t = xdt.shape[:2]
    nc, l = t // SSD_CHUNK, SSD_CHUNK
    g, e, p, n = SSD_GROUPS, SSD_HEADS_PER_GROUP, SSD_HEAD_DIM, SSD_STATE
    xc = xdt.reshape(bsz, nc, l, g, e, p)
    bc = b_mat.reshape(bsz, nc, l, g, n)
    cc = c_mat.reshape(bsz, nc, l, g, n)
    ac = adt.reshape(bsz, nc, l, g, e).transpose(0, 3, 4, 1, 2)
    a_cs = jnp.cumsum(ac, axis=-1)
    causal = jnp.tril(jnp.ones((l, l), dtype=bool))
    seg = a_cs[..., :, None] - a_cs[..., None, :]
    decay = jnp.exp(jnp.where(causal, seg, -jnp.inf))
    cb = jnp.einsum("bclgn,bcsgn->bgcls", cc, bc)
    y_diag = jnp.einsum("bgecls,bcsgep->bclgep", cb[:, :, None] * decay, xc)
    decay_to_end = jnp.exp(a_cs[..., -1:] - a_cs)
    states = jnp.einsum("bclgn,bgecl,bclgep->bcgepn", bc, decay_to_end, xc)
    chunk_decay = jnp.pad(a_cs[..., -1], [(0, 0)] * 3 + [(1, 0)])
    ccs = jnp.cumsum(chunk_decay, axis=-1)
    cseg = ccs[..., :, None] - ccs[..., None, :]
    ccausal = jnp.tril(jnp.ones((nc + 1, nc + 1), dtype=bool))
    chunk_trans = jnp.exp(jnp.where(ccausal, cseg, -jnp.inf))
    states = jnp.concatenate([jnp.zeros_like(states[:, :1]), states], axis=1)
    states_in = jnp.einsum("bgezc,bcgepn->bzgepn", chunk_trans, states)[:, :-1]
    in_decay = jnp.exp(a_cs).transpose(0, 3, 4, 1, 2)[..., None]
    y_off = jnp.einsum("bclgn,bcgepn->bclgep", cc, states_in) * in_decay
    return (y_diag + y_off).reshape(bsz, t, g, e, p)


def ssd_mixer(z, xbc, dt_raw, conv_w, conv_b, dt_bias, a_log, d_skip, norm_gain):
    f32 = jnp.float32
    bsz, l, _ = xbc.shape
    g, e, p, n = SSD_GROUPS, SSD_HEADS_PER_GROUP, SSD_HEAD_DIM, SSD_STATE
    xbc = jax.nn.silu(causal_depthwise_conv(xbc, conv_w, conv_b)).astype(f32)
    xh = xbc[..., :SSD_WIDTH].reshape(bsz, l, g, e, p)
    b_mat = xbc[..., SSD_WIDTH:SSD_WIDTH + g * n].reshape(bsz, l, g, n)
    c_mat = xbc[..., SSD_WIDTH + g * n:].reshape(bsz, l, g, n)
    dt = jax.nn.softplus(dt_raw.astype(f32) + dt_bias.astype(f32)).reshape(bsz, l, g, e)
    a = -jnp.exp(a_log.astype(f32)).reshape(g, e)
    pad = SSD_CHUNK - N_META
    def front(t):
        return jnp.pad(t, [(0, 0), (pad, 0)] + [(0, 0)] * (t.ndim - 2))
    y = ssd_chunked_scan(front(xh * dt[..., None]), front(dt * a), front(b_mat), front(c_mat))[:, pad:]
    y = y + d_skip.astype(f32).reshape(g, e)[..., None] * xh
    y = y.reshape(bsz, l, SSD_WIDTH) * jax.nn.silu(z.astype(f32))
    yg = y.reshape(bsz, l, g, SSD_WIDTH // g)
    yg = yg * lax.rsqrt(jnp.mean(yg * yg, axis=-1, keepdims=True) + RMS_EPS)
    return (yg.reshape(bsz, l, SSD_WIDTH) * norm_gain.astype(f32)).astype(z.dtype)


def rg_lru(xb, w_a, b_a, w_x, b_x, lam):
    f32 = jnp.float32
    bsz, l, _ = xb.shape
    xf = xb.astype(f32)
    xh = xf.reshape(bsz, l, LRU_HEADS, LRU_BLOCK)
    r = jax.nn.sigmoid(jnp.einsum("blhi,hij->blhj", xh, w_a.astype(f32)).reshape(bsz, l, LRU_WIDTH) + b_a.astype(f32))
    i = jax.nn.sigmoid(jnp.einsum("blhi,hij->blhj", xh, w_x.astype(f32)).reshape(bsz, l, LRU_WIDTH) + b_x.astype(f32))
    log_a = -LRU_C * r * jax.nn.softplus(-lam.astype(f32))
    a = jnp.exp(log_a)
    mult = jnp.sqrt(-jnp.expm1(2.0 * log_a))
    first = (jnp.arange(l) == 0)[None, :, None]
    mult = jnp.where(first, 1.0, mult)
    u = mult * (i * xf)

    def combine(lhs, rhs):
        a1, b1 = lhs
        a2, b2 = rhs
        return a1 * a2, a2 * b1 + b2

    _, h = lax.associative_scan(combine, (a, u), axis=1)
    return h.astype(xb.dtype)


def ssd_lru_mixer(hn, w_in, ssd_conv_w, ssd_conv_b, ssd_dt_bias, ssd_a_log, ssd_d, ssd_norm,
                  lru_conv_w, lru_conv_b, lru_w_a, lru_b_a, lru_w_x, lru_b_x, lru_lambda, w_out):
    proj = hn @ w_in
    i1 = SSD_WIDTH
    i2 = i1 + SSD_CONV_DIM
    i3 = i2 + SSD_HEADS
    i4 = i3 + LRU_WIDTH
    z, xbc, dt_raw, lru_gate, lru_in = jnp.split(proj, [i1, i2, i3, i4], axis=-1)
    y_ssd = ssd_mixer(z, xbc, dt_raw, ssd_conv_w, ssd_conv_b, ssd_dt_bias, ssd_a_log, ssd_d, ssd_norm)
    xb = causal_depthwise_conv(lru_in, lru_conv_w, lru_conv_b)
    y_lru = rg_lru(xb, lru_w_a, lru_b_a, lru_w_x, lru_b_x, lru_lambda) * jax.nn.gelu(lru_gate, approximate=True)
    return jnp.concatenate([y_ssd, y_lru.astype(y_ssd.dtype)], axis=-1) @ w_out


def multiscale_pool_mixer(hn, pool_w, pool_b, pool_scale):
    f32 = jnp.float32
    bsz, l, d = hn.shape
    hf = hn.astype(f32)
    csum = jnp.concatenate([jnp.zeros((bsz, POOL_MAX_WINDOW, d), f32), jnp.cumsum(hf, axis=1)], axis=1)
    pos = jnp.arange(l)
    outs = []
    for g, w in enumerate(POOL_WINDOWS):
        lo, hi = g * POOL_GROUP_DIM, (g + 1) * POOL_GROUP_DIM
        wsum = csum[:, POOL_MAX_WINDOW:, lo:hi] - csum[:, POOL_MAX_WINDOW - w:POOL_MAX_WINDOW - w + l, lo:hi]
        count = jnp.minimum(pos + 1, w).astype(f32)[None, :, None]
        outs.append(wsum / count - hf[..., lo:hi])
    pooled = jnp.stack(outs, axis=2)
    y = jnp.einsum("blgi,gij->blgj", pooled, pool_w.astype(f32)).reshape(bsz, l, d) + pool_b.astype(f32)
    return (y * pool_scale.astype(f32)).astype(hn.dtype)


def hierarchical_moe(hn, w_group, b_group, w_expert, b_expert, w_gate, w_up, w_down):
    f32 = jnp.float32
    bsz, l, d = hn.shape
    tok = hn.reshape(-1, d)
    tf = tok.astype(f32)
    p_group = jax.nn.softmax(tf @ w_group.astype(f32) + b_group.astype(f32), axis=-1)
    g_sel = jnp.argmax(p_group, axis=-1)
    p_g = jnp.take_along_axis(p_group, g_sel[:, None], axis=-1)
    fine = (tf @ w_expert.astype(f32) + b_expert.astype(f32)).reshape(-1, MOE_GROUPS, MOE_EXPERTS_PER_GROUP)
    fine = jnp.take_along_axis(fine, g_sel[:, None, None], axis=1)[:, 0]
    top_p, top_i = lax.top_k(jax.nn.softmax(fine, axis=-1), MOE_TOP_K)
    top_p = top_p / jnp.sum(top_p, axis=-1, keepdims=True)
    expert_idx = g_sel[:, None] * MOE_EXPERTS_PER_GROUP + top_i
    gates = jnp.sum(jax.nn.one_hot(expert_idx, MOE_EXPERTS, dtype=f32) * (p_g * top_p)[..., None], axis=1)
    out = jnp.zeros_like(tf)
    for e in range(MOE_EXPERTS):
        hid = jax.nn.silu(tok @ w_gate[e]) * (tok @ w_up[e])
        out = out + gates[:, e:e + 1] * (hid @ w_down[e]).astype(f32)
    return out.astype(hn.dtype).reshape(bsz, l, d)


def setup_inputs(seed: int = 0) -> dict:
    key = jax.random.key(seed)
    ks = iter(jax.random.split(key, 48))
    f32 = jnp.float32
    ne, no = N_EVEN, N_ODD

    def nrm(shape, scale):
        return scale * jax.random.normal(next(ks), shape, f32)

    def gain(shape):
        return 1.0 + 0.05 * jax.random.normal(next(ks), shape, f32)

    x = nrm((BATCH, SEQ, D_MODEL), 1.0)
    meta_tokens = nrm((N_META, D_MODEL), 1.0)
    norm_final = gain((D_MODEL,))
    mix_norm_even = gain((ne, D_MODEL))
    w_in = nrm((ne, D_MODEL, PROJ_COLS), D_MODEL ** -0.5)
    ssd_conv_w = nrm((ne, CONV_WIDTH, SSD_CONV_DIM), CONV_WIDTH ** -0.5)
    ssd_conv_b = nrm((ne, SSD_CONV_DIM), 0.02)
    dt0 = jnp.exp(jax.random.uniform(next(ks), (ne, SSD_HEADS), f32, math.log(1e-3), math.log(0.1)))
    ssd_dt_bias = dt0 + jnp.log(-jnp.expm1(-dt0))
    ssd_a_log = jnp.log(jax.random.uniform(next(ks), (ne, SSD_HEADS), f32, 1.0, 16.0))
    ssd_d = gain((ne, SSD_HEADS))
    ssd_norm = gain((ne, SSD_WIDTH))
    lru_conv_w = nrm((ne, CONV_WIDTH, LRU_WIDTH), CONV_WIDTH ** -0.5)
    lru_conv_b = nrm((ne, LRU_WIDTH), 0.02)
    lru_w_a = nrm((ne, LRU_HEADS, LRU_BLOCK, LRU_BLOCK), LRU_BLOCK ** -0.5)
    lru_b_a = nrm((ne, LRU_WIDTH), 0.02)
    lru_w_x = nrm((ne, LRU_HEADS, LRU_BLOCK, LRU_BLOCK), LRU_BLOCK ** -0.5)
    lru_b_x = nrm((ne, LRU_WIDTH), 0.02)
    a0 = jax.random.uniform(next(ks), (ne, LRU_WIDTH), f32, 0.9, 0.999)
    s0 = a0 ** (1.0 / LRU_C)
    lru_lambda = jnp.log(s0) - jnp.log1p(-s0)
    w_out = nrm((ne, MIX_WIDTH, D_MODEL), MIX_WIDTH ** -0.5)
    mix_norm_odd = gain((no, D_MODEL))
    pool_w = nrm((no, POOL_GROUPS, POOL_GROUP_DIM, POOL_GROUP_DIM), POOL_GROUP_DIM ** -0.5)
    pool_b = nrm((no, D_MODEL), 0.02)
    pool_scale = gain((no, D_MODEL))
    ffn_norm = gain((DEPTH, D_MODEL))
    router_group_w = nrm((DEPTH, D_MODEL, MOE_GROUPS), D_MODEL ** -0.5)
    router_group_b = nrm((DEPTH, MOE_GROUPS), 0.01)
    router_expert_w = nrm((DEPTH, D_MODEL, MOE_EXPERTS), D_MODEL ** -0.5)
    router_expert_b = nrm((DEPTH, MOE_EXPERTS), 0.01)
    expert_w_gate = nrm((DEPTH, MOE_EXPERTS, D_MODEL, D_EXPERT), D_MODEL ** -0.5)
    expert_w_up = nrm((DEPTH, MOE_EXPERTS, D_MODEL, D_EXPERT), D_MODEL ** -0.5)
    expert_w_down = nrm((DEPTH, MOE_EXPERTS, D_EXPERT, D_MODEL), D_EXPERT ** -0.5)
    return {
        "x": x, "meta_tokens": meta_tokens, "norm_final": norm_final,
        "mix_norm_even": mix_norm_even, "w_in": w_in,
        "ssd_conv_w": ssd_conv_w, "ssd_conv_b": ssd_conv_b, "ssd_dt_bias": ssd_dt_bias,
        "ssd_a_log": ssd_a_log, "ssd_d": ssd_d, "ssd_norm": ssd_norm,
        "lru_conv_w": lru_conv_w, "lru_conv_b": lru_conv_b, "lru_w_a": lru_w_a, "lru_b_a": lru_b_a,
        "lru_w_x": lru_w_x, "lru_b_x": lru_b_x, "lru_lambda": lru_lambda, "w_out": w_out,
        "mix_norm_odd": mix_norm_odd, "pool_w": pool_w, "pool_b": pool_b, "pool_scale": pool_scale,
        "ffn_norm": ffn_norm, "router_group_w": router_group_w, "router_group_b": router_group_b,
        "router_expert_w": router_expert_w, "router_expert_b": router_expert_b,
        "expert_w_gate": expert_w_gate, "expert_w_up": expert_w_up, "expert_w_down": expert_w_down,
    }


def reference(x, meta_tokens, norm_final, mix_norm_even, w_in, ssd_conv_w, ssd_conv_b, ssd_dt_bias,
              ssd_a_log, ssd_d, ssd_norm, lru_conv_w, lru_conv_b, lru_w_a, lru_b_a, lru_w_x, lru_b_x,
              lru_lambda, w_out, mix_norm_odd, pool_w, pool_b, pool_scale, ffn_norm, router_group_w,
              router_group_b, router_expert_w, router_expert_b, expert_w_gate, expert_w_up, expert_w_down):
    bsz = x.shape[0]
    meta = jnp.broadcast_to(meta_tokens.astype(x.dtype)[None], (bsz, N_META, D_MODEL))
    h = jnp.concatenate([meta, x], axis=1)
    for layer in range(DEPTH):
        j = layer // 2
        if layer % 2 == 0:
            h = h + ssd_lru_mixer(
                rms_norm(h, mix_norm_even[j]), w_in[j], ssd_conv_w[j], ssd_conv_b[j], ssd_dt_bias[j],
                ssd_a_log[j], ssd_d[j], ssd_norm[j], lru_conv_w[j], lru_conv_b[j], lru_w_a[j], lru_b_a[j],
                lru_w_x[j], lru_b_x[j], lru_lambda[j], w_out[j])
        else:
            h = h + multiscale_pool_mixer(rms_norm(h, mix_norm_odd[j]), pool_w[j], pool_b[j], pool_scale[j])
        h = h + hierarchical_moe(
            rms_norm(h, ffn_norm[layer]), router_group_w[layer], router_group_b[layer],
            router_expert_w[layer], router_expert_b[layer], expert_w_gate[layer], expert_w_up[layer],
            expert_w_down[layer])
    return rms_norm(h[:, N_META:], norm_final)
```

```python
import functools
import math

import jax
import jax.numpy as jnp
from jax import lax
from jax.experimental import pallas as pl
from jax.experimental.pallas import tpu as pltpu

F32 = jnp.float32
BF16 = jnp.bfloat16
HIGHEST = lax.Precision.HIGHEST

D_MODEL = 1024
BATCH = 8
SEQ = 2048
N_META = 16
RMS_EPS = 1e-6
CONV_WIDTH = 4
CHUNK = 128
PAD = CHUNK - N_META
LP = PAD + N_META + SEQ
NCHUNK = LP // CHUNK
TP = BATCH * LP

SSD_HEADS = 16
SSD_HEAD_DIM = 64
SSD_WIDTH = 1024
SSD_GROUPS = 2
SSD_STATE = 128
SSD_GROUP_WIDTH = SSD_WIDTH // SSD_GROUPS
SSD_CONV_DIM = SSD_WIDTH + 2 * SSD_GROUPS * SSD_STATE
LRU_WIDTH = 1024
LRU_C = 8.0
LANES = 128
POOL_WINDOWS = (2, 4, 8, 16)
POOL_GROUP_DIM = 256
POOL_MAX_WINDOW = 16
MOE_GROUPS = 4
MOE_PER_GROUP = 4
MOE_EXPERTS = 16
D_EXPERT = 512
ROUTER_EXPERT_ROW = 8

VMEM_LIMIT = 56 * 1024 * 1024


def _dot(a, b, precision=None):
    return jnp.dot(a, b, preferred_element_type=F32, precision=precision)


def _rms(x, gain):
    ms = jnp.mean(x * x, axis=-1, keepdims=True)
    return x * lax.rsqrt(ms + RMS_EPS) * gain


def _silu(x):
    return x * jax.nn.sigmoid(x)


def _softplus(x):
    return jnp.maximum(x, 0.0) + jnp.log1p(jnp.exp(-jnp.abs(x)))


def _row_ids(shape, chunk):
    return lax.broadcasted_iota(jnp.int32, shape, 0) + chunk * CHUNK


PROJ_TM = 256
_Z0, _X0, _G0, _L0, _T0, _PEND = 0, 1024, 2560, 3584, 4608, 4736


def _in_proj_kernel(h_ref, g_ref, w_ref, z_ref, xbc_ref, gate_ref, lin_ref, dt_ref):
    hn = _rms(h_ref[...], g_ref[...]).astype(BF16)
    z_ref[...] = _dot(hn, w_ref[:, _Z0:_X0])
    xbc_ref[...] = _dot(hn, w_ref[:, _X0:_G0])
    gate_ref[...] = _dot(hn, w_ref[:, _G0:_L0])
    lin_ref[...] = _dot(hn, w_ref[:, _L0:_T0])
    dt_ref[...] = _dot(hn, w_ref[:, _T0:_PEND])


def _in_proj(h, gain, w):
    widths = (1024, SSD_CONV_DIM, 1024, 1024, LANES)
    row = lambda i: (i, 0)
    return pl.pallas_call(
        _in_proj_kernel,
        grid=(TP // PROJ_TM,),
        in_specs=[
            pl.BlockSpec((PROJ_TM, D_MODEL), row),
            pl.BlockSpec((1, D_MODEL), lambda i: (0, 0)),
            pl.BlockSpec((D_MODEL, _PEND), lambda i: (0, 0)),
        ],
        out_specs=[pl.BlockSpec((PROJ_TM, n), row) for n in widths],
        out_shape=[jax.ShapeDtypeStruct((TP, n), F32) for n in widths],
        compiler_params=pltpu.CompilerParams(
            dimension_semantics=("arbitrary",), vmem_limit_bytes=VMEM_LIMIT),
        name="in_proj",
    )(h, gain, w)


def _causal_conv(buf, x, w_ref, b_ref):
    buf[8:8 + CHUNK, :] = x
    acc = b_ref[...] + w_ref[0:1, :] * buf[5:5 + CHUNK, :]
    for k in range(1, CONV_WIDTH):
        acc = acc + w_ref[k:k + 1, :] * buf[5 + k:5 + k + CHUNK, :]
    buf[0:8, :] = buf[CHUNK:CHUNK + 8, :]
    return acc


def _mixer_kernel(h_ref, z_ref, xbc_ref, gate_ref, lin_ref, dt_ref,
                  cw_ref, cb_ref, dtb_ref, alog_ref, dsk_ref, ng_ref,
                  lcw_ref, lcb_ref, wa_ref, ba_ref, wx_ref, bx_ref, lam_ref,
                  wout_ref, exp_ref, o_ref, cbx, cbl, st, lc):
    c = pl.program_id(1)

    @pl.when(c == 0)
    def _():
        cbx[0:8, :] = jnp.zeros((8, SSD_CONV_DIM), F32)
        cbl[0:8, :] = jnp.zeros((8, LRU_WIDTH), F32)
        st[...] = jnp.zeros_like(st)
        lc[...] = jnp.zeros_like(lc)

    xc = _silu(_causal_conv(cbx, xbc_ref[0], cw_ref, cb_ref))
    xs = xc[:, 0:SSD_WIDTH]
    valid_s = _row_ids((CHUNK, 2 * SSD_STATE), c) >= PAD
    bm = jnp.where(valid_s, xc[:, SSD_WIDTH:SSD_WIDTH + 2 * SSD_STATE], 0.0)
    cm = jnp.where(valid_s, xc[:, SSD_WIDTH + 2 * SSD_STATE:], 0.0)

    li = lax.broadcasted_iota(jnp.int32, (CHUNK, CHUNK), 0)
    si = lax.broadcasted_iota(jnp.int32, (CHUNK, CHUNK), 1)
    causal = si <= li
    dt = _softplus(dt_ref[0] + dtb_ref[...])
    dt = jnp.where(li + c * CHUNK >= PAD, dt, 0.0)
    adt = dt * (-jnp.exp(alog_ref[...]))
    a_cs = _dot(causal.astype(F32), adt, HIGHEST)
    a_last = a_cs[CHUNK - 1:CHUNK, :]
    ea = jnp.exp(a_cs)
    expand = exp_ref[...]
    dt_x = _dot(dt, expand, HIGHEST)
    w_x = _dot(jnp.exp(a_last - a_cs) * dt, expand, HIGHEST)
    ea_x = _dot(ea, expand, HIGHEST)
    xdt = xs * dt_x
    a_cs_t = a_cs.T
    lane = lax.broadcasted_iota(jnp.int32, (CHUNK, LANES), 1)

    ys = []
    for g in range(SSD_GROUPS):
        gsl = slice(g * SSD_GROUP_WIDTH, (g + 1) * SSD_GROUP_WIDTH)
        bg = bm[:, g * SSD_STATE:(g + 1) * SSD_STATE]
        cg16 = cm[:, g * SSD_STATE:(g + 1) * SSD_STATE].astype(BF16)
        cbm = lax.dot_general(cg16, bg.astype(BF16), (((1,), (1,)), ((), ())),
                              preferred_element_type=F32)
        s_in = st[:, gsl]
        y_off = _dot(cg16, s_in.astype(BF16)) * ea_x[:, gsl]
        s_new = _dot(bg.T.astype(BF16), (w_x[:, gsl] * xs[:, gsl]).astype(BF16))
        st[:, gsl] = ea_x[CHUNK - 1:CHUNK, gsl] * s_in + s_new
        for j in range(SSD_GROUP_WIDTH // LANES):
            xp = xdt[:, g * SSD_GROUP_WIDTH + j * LANES:
                     g * SSD_GROUP_WIDTH + (j + 1) * LANES].astype(BF16)
            parts = []
            for hh in range(LANES // SSD_HEAD_DIM):
                hd = g * (SSD_HEADS // SSD_GROUPS) + j * (LANES // SSD_HEAD_DIM) + hh
                seg = a_cs[:, hd:hd + 1] - a_cs_t[hd:hd + 1, :]
                dec = jnp.exp(jnp.where(causal, seg, -1e30))
                parts.append(_dot((cbm * dec).astype(BF16), xp))
            ys.append(jnp.where(lane < SSD_HEAD_DIM, parts[0], parts[1])
                      + y_off[:, j * LANES:(j + 1) * LANES])
    y = jnp.concatenate(ys, axis=1) + dsk_ref[...] * xs
    y = y * _silu(z_ref[0])
    normed = []
    for g in range(SSD_GROUPS):
        yg = y[:, g * SSD_GROUP_WIDTH:(g + 1) * SSD_GROUP_WIDTH]
        normed.append(yg * lax.rsqrt(jnp.mean(yg * yg, axis=-1, keepdims=True) + RMS_EPS))
    y_ssd = jnp.concatenate(normed, axis=1) * ng_ref[...]

    xb = _causal_conv(cbl, lin_ref[0], lcw_ref, lcb_ref)
    xb16 = xb.astype(BF16)
    nblk = LRU_WIDTH // LANES
    ra = jnp.concatenate(
        [_dot(xb16[:, j * LANES:(j + 1) * LANES], wa_ref[j]) for j in range(nblk)], axis=1)
    ix = jnp.concatenate(
        [_dot(xb16[:, j * LANES:(j + 1) * LANES], wx_ref[j]) for j in range(nblk)], axis=1)
    r = jax.nn.sigmoid(ra + ba_ref[...])
    ig = jax.nn.sigmoid(ix + bx_ref[...])
    log_a = (-LRU_C * _softplus(-lam_ref[...])) * r
    a = jnp.exp(log_a)
    mult = jnp.sqrt(jnp.tanh(-log_a) * (a * a + 1.0))
    grow = _row_ids((CHUNK, LRU_WIDTH), c)
    mult = jnp.where(grow == PAD, 1.0, mult)
    u = jnp.where(grow >= PAD, mult * (ig * xb), 0.0)

    r8 = lax.broadcasted_iota(jnp.int32, (8, LRU_WIDTH), 0)
    carry = lc[0:1, :]
    hs = []
    for j in range(CHUNK // 8):
        a8 = a[j * 8:(j + 1) * 8, :]
        u8 = u[j * 8:(j + 1) * 8, :]
        for d in (1, 2, 4):
            a_sh = jnp.where(r8 >= d, pltpu.roll(a8, d, 0), 1.0)
            u_sh = jnp.where(r8 >= d, pltpu.roll(u8, d, 0), 0.0)
            u8 = a8 * u_sh + u8
            a8 = a8 * a_sh
        h8 = a8 * carry + u8
        carry = h8[7:8, :]
        hs.append(h8)
    lc[0:1, :] = carry
    gt = gate_ref[0]
    gelu = 0.5 * gt * (1.0 + jnp.tanh(math.sqrt(2.0 / math.pi) * (gt + 0.044715 * (gt * gt * gt))))
    y_lru = jnp.concatenate(hs, axis=0) * gelu

    ycat = jnp.concatenate([y_ssd, y_lru], axis=1).astype(BF16)
    out = _dot(ycat, wout_ref[...]) + h_ref[0]
    o_ref[0] = jnp.where(grow >= PAD, out, 0.0)


def _mixer(h, z, xbc, gate, lin, dt, params):
    blk = lambda n: pl.BlockSpec((1, CHUNK, n), lambda b, c: (b, c, 0))
    full = lambda a: pl.BlockSpec(a.shape, lambda b, c: (0,) * a.ndim)
    acts = [h, z, xbc, gate, lin, dt]
    act_specs = [blk(a.shape[-1]) for a in acts]
    return pl.pallas_call(
        _mixer_kernel,
        grid=(BATCH, NCHUNK),
        in_specs=act_specs + [full(p) for p in params],
        out_specs=blk(D_MODEL),
        out_shape=jax.ShapeDtypeStruct((BATCH, LP, D_MODEL), F32),
        scratch_shapes=[
            pltpu.VMEM((CHUNK + 8, SSD_CONV_DIM), F32),
            pltpu.VMEM((CHUNK + 8, LRU_WIDTH), F32),
            pltpu.VMEM((SSD_STATE, SSD_WIDTH), F32),
            pltpu.VMEM((8, LRU_WIDTH), F32),
        ],
        compiler_params=pltpu.CompilerParams(
            dimension_semantics=("arbitrary", "arbitrary"), vmem_limit_bytes=VMEM_LIMIT),
        name="ssd_lru_mixer",
    )(*acts, *params)


def _pool_kernel(h_ref, g_ref, pw_ref, pb_ref, ps_ref, o_ref, buf):
    c = pl.program_id(1)

    @pl.when(c == 0)
    def _():
        buf[0:POOL_MAX_WINDOW, :] = jnp.zeros((POOL_MAX_WINDOW, D_MODEL), F32)

    h = h_ref[0]
    hn = _rms(h, g_ref[...])
    base = POOL_MAX_WINDOW
    buf[base:base + CHUNK, :] = hn
    pos = _row_ids((CHUNK, POOL_GROUP_DIM), c) - PAD
    outs = []
    for g, w in enumerate(POOL_WINDOWS):
        sl = slice(g * POOL_GROUP_DIM, (g + 1) * POOL_GROUP_DIM)
        ws = buf[base:base + CHUNK, sl]
        for j in range(1, w):
            ws = ws + buf[base - j:base - j + CHUNK, sl]
        count = jnp.clip(pos + 1, 1, w).astype(F32)
        pooled = ws / count - hn[:, sl]
        outs.append(_dot(pooled.astype(BF16), pw_ref[g]))
    buf[0:base, :] = buf[CHUNK:CHUNK + base, :]
    y = (jnp.concatenate(outs, axis=1) + pb_ref[...]) * ps_ref[...]
    o_ref[0] = jnp.where(_row_ids((CHUNK, D_MODEL), c) >= PAD, h + y, 0.0)


def _pool_mixer(h, gain, pw, pb, ps):
    blk = pl.BlockSpec((1, CHUNK, D_MODEL), lambda b, c: (b, c, 0))
    full = lambda a: pl.BlockSpec(a.shape, lambda b, c: (0,) * a.ndim)
    return pl.pallas_call(
        _pool_kernel,
        grid=(BATCH, NCHUNK),
        in_specs=[blk, full(gain), full(pw), full(pb), full(ps)],
        out_specs=blk,
        out_shape=jax.ShapeDtypeStruct((BATCH, LP, D_MODEL), F32),
        scratch_shapes=[pltpu.VMEM((CHUNK + POOL_MAX_WINDOW, D_MODEL), F32)],
        compiler_params=pltpu.CompilerParams(
            dimension_semantics=("arbitrary", "arbitrary"), vmem_limit_bytes=VMEM_LIMIT),
        name="pool_mixer",
    )(h, gain, pw, pb, ps)


ROUTER_TM = 512


def _first_argmax(vals):
    best, idx = vals[0], jnp.zeros(vals[0].shape, jnp.int32)
    for k in range(1, len(vals)):
        better = vals[k] > best
        idx = jnp.where(better, k, idx)
        best = jnp.where(better, vals[k], best)
    return idx, best


def _softmax_rows(vals):
    m = functools.reduce(jnp.maximum, vals)
    ex = [jnp.exp(v - m) for v in vals]
    tot = functools.reduce(lambda p, q: p + q, ex)
    return [e / tot for e in ex]


def _router_kernel(h_ref, g_ref, wr_ref, br_ref, hn_ref, gates_ref):
    hn = _rms(h_ref[...], g_ref[...])
    hn_ref[...] = hn.astype(BF16)
    logits = _dot(hn, wr_ref[...], HIGHEST) + br_ref[...]
    lt = logits.T
    p_group = _softmax_rows([lt[k:k + 1, :] for k in range(MOE_GROUPS)])
    g_sel, p_g = _first_argmax(p_group)
    fine = []
    for k in range(MOE_PER_GROUP):
        f = lt[ROUTER_EXPERT_ROW + k:ROUTER_EXPERT_ROW + k + 1, :]
        for g in range(1, MOE_GROUPS):
            r0 = ROUTER_EXPERT_ROW + g * MOE_PER_GROUP + k
            f = jnp.where(g_sel == g, lt[r0:r0 + 1, :], f)
        fine.append(f)
    q = _softmax_rows(fine)
    i1, t1 = _first_argmax(q)
    i2, t2 = _first_argmax([jnp.where(i1 == k, -1.0, q[k]) for k in range(MOE_PER_GROUP)])
    tot = t1 + t2
    gate1 = p_g * (t1 / tot)
    gate2 = p_g * (t2 / tot)
    e1 = g_sel * MOE_PER_GROUP + i1
    e2 = g_sel * MOE_PER_GROUP + i2
    rows = lax.broadcasted_iota(jnp.int32, lt.shape, 0)
    dense_t = jnp.where(rows == e1, gate1, jnp.where(rows == e2, gate2, 0.0))
    gates_ref[...] = dense_t.T


def _router(h, gain, wr, br):
    row = lambda i: (i, 0)
    return pl.pallas_call(
        _router_kernel,
        grid=(TP // ROUTER_TM,),
        in_specs=[
            pl.BlockSpec((ROUTER_TM, D_MODEL), row),
            pl.BlockSpec((1, D_MODEL), lambda i: (0, 0)),
            pl.BlockSpec((D_MODEL, LANES), lambda i: (0, 0)),
            pl.BlockSpec((1, LANES), lambda i: (0, 0)),
        ],
        out_specs=[pl.BlockSpec((ROUTER_TM, D_MODEL), row), pl.BlockSpec((ROUTER_TM, LANES), row)],
        out_shape=[jax.ShapeDtypeStruct((TP, D_MODEL), BF16), jax.ShapeDtypeStruct((TP, LANES), F32)],
        compiler_params=pltpu.CompilerParams(
            dimension_semantics=("arbitrary",), vmem_limit_bytes=VMEM_LIMIT),
        name="moe_router",
    )(h, gain, wr, br)


MOE_TM = 1024


def _moe_kernel(h_ref, hn_ref, gates_ref, wg_ref, wu_ref, wd_ref, o_ref, acc):
    e = pl.program_id(1)

    @pl.when(e == 0)
    def _():
        acc[...] = h_ref[...]

    x = hn_ref[...]
    a = _dot(x, wg_ref[0])
    b = _dot(x, wu_ref[0])
    y = _dot((_silu(a) * b).astype(BF16), wd_ref[0])
    lane = lax.broadcasted_iota(jnp.int32, (MOE_TM, LANES), 1)
    gate = jnp.sum(jnp.where(lane == e, gates_ref[...], 0.0), axis=1, keepdims=True)
    acc[...] += gate * y

    @pl.when(e == MOE_EXPERTS - 1)
    def _():
        o_ref[...] = acc[...]


def _moe(h, hn, gates, wg, wu, wd):
    row = lambda i, e: (i, 0)
    return pl.pallas_call(
        _moe_kernel,
        grid=(TP // MOE_TM, MOE_EXPERTS),
        in_specs=[
            pl.BlockSpec((MOE_TM, D_MODEL), row),
            pl.BlockSpec((MOE_TM, D_MODEL), row),
            pl.BlockSpec((MOE_TM, LANES), row),
            pl.BlockSpec((1, D_MODEL, D_EXPERT), lambda i, e: (e, 0, 0)),
            pl.BlockSpec((1, D_MODEL, D_EXPERT), lambda i, e: (e, 0, 0)),
            pl.BlockSpec((1, D_EXPERT, D_MODEL), lambda i, e: (e, 0, 0)),
        ],
        out_specs=pl.BlockSpec((MOE_TM, D_MODEL), row),
        out_shape=jax.ShapeDtypeStruct((TP, D_MODEL), F32),
        scratch_shapes=[pltpu.VMEM((MOE_TM, D_MODEL), F32)],
        compiler_params=pltpu.CompilerParams(
            dimension_semantics=("arbitrary", "arbitrary"), vmem_limit_bytes=VMEM_LIMIT),
        name="moe_experts",
    )(h, hn, gates, wg, wu, wd)


def _final_norm_kernel(h_ref, g_ref, o_ref):
    o_ref[0] = _rms(h_ref[0], g_ref[...])


def _final_norm(h, gain):
    return pl.pallas_call(
        _final_norm_kernel,
        grid=(BATCH, SEQ // CHUNK),
        in_specs=[pl.BlockSpec((1, CHUNK, D_MODEL), lambda b, c: (b, c + 1, 0)),
                  pl.BlockSpec((1, D_MODEL), lambda b, c: (0, 0))],
        out_specs=pl.BlockSpec((1, CHUNK, D_MODEL), lambda b, c: (b, c, 0)),
        out_shape=jax.ShapeDtypeStruct((BATCH, SEQ, D_MODEL), F32),
        compiler_params=pltpu.CompilerParams(dimension_semantics=("arbitrary", "arbitrary")),
        name="final_norm",
    )(h, gain)


def _row(v):
    return v.reshape(1, -1).astype(F32)


def _pad_lanes(v):
    return jnp.pad(_row(v), ((0, 0), (0, LANES - v.shape[-1])))


def _pair_blockdiag(w):
    w = w.reshape(LRU_WIDTH // LANES, 2, 64, 64)
    z = jnp.zeros_like(w[:, 0])
    top = jnp.concatenate([w[:, 0], z], axis=2)
    bot = jnp.concatenate([z, w[:, 1]], axis=2)
    return jnp.concatenate([top, bot], axis=1).astype(BF16)


def _moe_layer(h, layer, ffn_norm, rgw, rgb, rew, reb, wg, wu, wd):
    wr = jnp.zeros((D_MODEL, LANES), F32)
    wr = wr.at[:, 0:MOE_GROUPS].set(rgw[layer])
    wr = wr.at[:, ROUTER_EXPERT_ROW:ROUTER_EXPERT_ROW + MOE_EXPERTS].set(rew[layer])
    br = jnp.zeros((1, LANES), F32)
    br = br.at[0, 0:MOE_GROUPS].set(rgb[layer])
    br = br.at[0, ROUTER_EXPERT_ROW:ROUTER_EXPERT_ROW + MOE_EXPERTS].set(reb[layer])
    hn, gates = _router(h, _row(ffn_norm[layer]), wr, br)
    return _moe(h, hn, gates, wg[layer].astype(BF16), wu[layer].astype(BF16), wd[layer].astype(BF16))


def kernel(x, meta_tokens, norm_final, mix_norm_even, w_in, ssd_conv_w, ssd_conv_b, ssd_dt_bias, ssd_a_log, ssd_d, ssd_norm, lru_conv_w, lru_conv_b, lru_w_a, lru_b_a, lru_w_x, lru_b_x, lru_lambda, w_out, mix_norm_odd, pool_w, pool_b, pool_scale, ffn_norm, router_group_w, router_group_b, router_expert_w, router_expert_b, expert_w_gate, expert_w_up, expert_w_down):
    meta = jnp.broadcast_to(meta_tokens.astype(F32)[None], (BATCH, N_META, D_MODEL))
    h = jnp.concatenate([jnp.zeros((BATCH, PAD, D_MODEL), F32), meta, x.astype(F32)], axis=1)
    h = h.reshape(TP, D_MODEL)
    moe_args = (ffn_norm, router_group_w, router_group_b, router_expert_w, router_expert_b,
                expert_w_gate, expert_w_up, expert_w_down)

    wi = w_in[0]
    w_proj = jnp.concatenate(
        [wi[:, 0:1024], wi[:, 1024:2560], wi[:, 2576:3600], wi[:, 3600:4624], wi[:, 2560:2576],
         jnp.zeros((D_MODEL, LANES - SSD_HEADS), F32)], axis=1).astype(BF16)
    z, xbc, gate, lin, dt = _in_proj(h, _row(mix_norm_even[0]), w_proj)
    expand = (jnp.arange(LANES)[:, None] == (jnp.arange(SSD_WIDTH) // SSD_HEAD_DIM)[None, :]).astype(F32)
    params = [
        ssd_conv_w[0].astype(F32), _row(ssd_conv_b[0]), _pad_lanes(ssd_dt_bias[0]), _pad_lanes(ssd_a_log[0]),
        _row(jnp.repeat(ssd_d[0], SSD_HEAD_DIM)), _row(ssd_norm[0]),
        lru_conv_w[0].astype(F32), _row(lru_conv_b[0]), _pair_blockdiag(lru_w_a[0]), _row(lru_b_a[0]),
        _pair_blockdiag(lru_w_x[0]), _row(lru_b_x[0]), _row(lru_lambda[0]),
        w_out[0].astype(BF16), expand,
    ]
    r3 = lambda a: a.reshape(BATCH, LP, a.shape[-1])
    h = _mixer(r3(h), r3(z), r3(xbc), r3(gate), r3(lin), r3(dt), params).reshape(TP, D_MODEL)
    h = _moe_layer(h, 0, *moe_args)

    h = _pool_mixer(h.reshape(BATCH, LP, D_MODEL), _row(mix_norm_odd[0]), pool_w[0].astype(BF16),
                    _row(pool_b[0]), _row(pool_scale[0])).reshape(TP, D_MODEL)
    h = _moe_layer(h, 1, *moe_args)

    return _final_norm(h.reshape(BATCH, LP, D_MODEL), _row(norm_final))
```

```python
import functools
import math

import jax
import jax.numpy as jnp
from jax import lax
from jax.experimental import pallas as pl
from jax.experimental.pallas import tpu as pltpu

F32 = jnp.float32
BF16 = jnp.bfloat16
HIGHEST = lax.Precision.HIGHEST

D_MODEL = 1024
BATCH = 8
SEQ = 2048
N_META = 16
RMS_EPS = 1e-6
CONV_WIDTH = 4
CHUNK = 128
PAD = CHUNK - N_META
LP = PAD + N_META + SEQ
NCHUNK = LP // CHUNK
TP = BATCH * LP

SSD_HEADS = 16
SSD_HEAD_DIM = 64
SSD_WIDTH = 1024
SSD_GROUPS = 2
SSD_STATE = 128
SSD_GROUP_WIDTH = SSD_WIDTH // SSD_GROUPS
SSD_CONV_DIM = SSD_WIDTH + 2 * SSD_GROUPS * SSD_STATE
LRU_WIDTH = 1024
LRU_C = 8.0
LANES = 128
POOL_WINDOWS = (2, 4, 8, 16)
POOL_GROUP_DIM = 256
POOL_MAX_WINDOW = 16
MOE_GROUPS = 4
MOE_PER_GROUP = 4
MOE_EXPERTS = 16
D_EXPERT = 512
ROUTER_EXPERT_ROW = 8

VMEM_LIMIT = 56 * 1024 * 1024


def _dot(a, b, precision=None):
    return jnp.dot(a, b, preferred_element_type=F32, precision=precision)


def _rms(x, gain):
    ms = jnp.mean(x * x, axis=-1, keepdims=True)
    return x * lax.rsqrt(ms + RMS_EPS) * gain


def _silu(x):
    return x * jax.nn.sigmoid(x)


def _softplus(x):
    return jnp.maximum(x, 0.0) + jnp.log1p(jnp.exp(-jnp.abs(x)))


def _row_ids(shape, chunk):
    return lax.broadcasted_iota(jnp.int32, shape, 0) + chunk * CHUNK


PROJ_TM = 256
_Z0, _X0, _G0, _L0, _T0, _PEND = 0, 1024, 2560, 3584, 4608, 4736


def _in_proj_kernel(h_ref, g_ref, w_ref, z_ref, xbc_ref, gate_ref, lin_ref, dt_ref):
    hn = _rms(h_ref[...], g_ref[...]).astype(BF16)
    z_ref[...] = _dot(hn, w_ref[:, _Z0:_X0])
    xbc_ref[...] = _dot(hn, w_ref[:, _X0:_G0])
    gate_ref[...] = _dot(hn, w_ref[:, _G0:_L0])
    lin_ref[...] = _dot(hn, w_ref[:, _L0:_T0])
    dt_ref[...] = _dot(hn, w_ref[:, _T0:_PEND])


def _in_proj(h, gain, w):
    widths = (1024, SSD_CONV_DIM, 1024, 1024, LANES)
    row = lambda i: (i, 0)
    return pl.pallas_call(
        _in_proj_kernel,
        grid=(TP // PROJ_TM,),
        in_specs=[
            pl.BlockSpec((PROJ_TM, D_MODEL), row),
            pl.BlockSpec((1, D_MODEL), lambda i: (0, 0)),
            pl.BlockSpec((D_MODEL, _PEND), lambda i: (0, 0)),
        ],
        out_specs=[pl.BlockSpec((PROJ_TM, n), row) for n in widths],
        out_shape=[jax.ShapeDtypeStruct((TP, n), F32) for n in widths],
        compiler_params=pltpu.CompilerParams(
            dimension_semantics=("arbitrary",), vmem_limit_bytes=VMEM_LIMIT),
        name="in_proj",
    )(h, gain, w)


def _causal_conv(buf, x, w_ref, b_ref):
    buf[8:8 + CHUNK, :] = x
    acc = b_ref[...] + w_ref[0:1, :] * buf[5:5 + CHUNK, :]
    for k in range(1, CONV_WIDTH):
        acc = acc + w_ref[k:k + 1, :] * buf[5 + k:5 + k + CHUNK, :]
    buf[0:8, :] = buf[CHUNK:CHUNK + 8, :]
    return acc


def _mixer_kernel(h_ref, z_ref, xbc_ref, gate_ref, lin_ref, dt_ref,
                  cw_ref, cb_ref, dtb_ref, alog_ref, dsk_ref, ng_ref,
                  lcw_ref, lcb_ref, wa_ref, ba_ref, wx_ref, bx_ref, lam_ref,
                  wout_ref, exp_ref, o_ref, cbx, cbl, st, lc):
    c = pl.program_id(1)

    @pl.when(c == 0)
    def _():
        cbx[0:8, :] = jnp.zeros((8, SSD_CONV_DIM), F32)
        cbl[0:8, :] = jnp.zeros((8, LRU_WIDTH), F32)
        st[...] = jnp.zeros_like(st)
        lc[...] = jnp.zeros_like(lc)

    xc = _silu(_causal_conv(cbx, xbc_ref[0], cw_ref, cb_ref))
    xs = xc[:, 0:SSD_WIDTH]
    valid_s = _row_ids((CHUNK, 2 * SSD_STATE), c) >= PAD
    bm = jnp.where(valid_s, xc[:, SSD_WIDTH:SSD_WIDTH + 2 * SSD_STATE], 0.0)
    cm = jnp.where(valid_s, xc[:, SSD_WIDTH + 2 * SSD_STATE:], 0.0)

    li = lax.broadcasted_iota(jnp.int32, (CHUNK, CHUNK), 0)
    si = lax.broadcasted_iota(jnp.int32, (CHUNK, CHUNK), 1)
    causal = si <= li
    dt = _softplus(dt_ref[0] + dtb_ref[...])
    dt = jnp.where(li + c * CHUNK >= PAD, dt, 0.0)
    adt = dt * (-jnp.exp(alog_ref[...]))
    a_cs = _dot(causal.astype(F32), adt, HIGHEST)
    a_last = a_cs[CHUNK - 1:CHUNK, :]
    ea = jnp.exp(a_cs)
    expand = exp_ref[...]
    dt_x = _dot(dt, expand, HIGHEST)
    w_x = _dot(jnp.exp(a_last - a_cs) * dt, expand, HIGHEST)
    ea_x = _dot(ea, expand, HIGHEST)
    xdt = xs * dt_x
    a_cs_t = a_cs.T
    lane = lax.broadcasted_iota(jnp.int32, (CHUNK, LANES), 1)

    ys = []
    for g in range(SSD_GROUPS):
        gsl = slice(g * SSD_GROUP_WIDTH, (g + 1) * SSD_GROUP_WIDTH)
        bg = bm[:, g * SSD_STATE:(g + 1) * SSD_STATE]
        cg16 = cm[:, g * SSD_STATE:(g + 1) * SSD_STATE].astype(BF16)
        cbm = lax.dot_general(cg16, bg.astype(BF16), (((1,), (1,)), ((), ())),
                              preferred_element_type=F32)
        s_in = st[:, gsl]
        y_off = _dot(cg16, s_in.astype(BF16)) * ea_x[:, gsl]
        s_new = _dot(bg.T.astype(BF16), (w_x[:, gsl] * xs[:, gsl]).astype(BF16))
        st[:, gsl] = ea_x[CHUNK - 1:CHUNK, gsl] * s_in + s_new
        for j in range(SSD_GROUP_WIDTH // LANES):
            xp = xdt[:, g * SSD_GROUP_WIDTH + j * LANES:
                     g * SSD_GROUP_WIDTH + (j + 1) * LANES].astype(BF16)
            parts = []
            for hh in range(LANES // SSD_HEAD_DIM):
                hd = g * (SSD_HEADS // SSD_GROUPS) + j * (LANES // SSD_HEAD_DIM) + hh
                seg = a_cs[:, hd:hd + 1] - a_cs_t[hd:hd + 1, :]
                dec = jnp.exp(jnp.where(causal, seg, -1e30))
                parts.append(_dot((cbm * dec).astype(BF16), xp))
            ys.append(jnp.where(lane < SSD_HEAD_DIM, parts[0], parts[1])
                      + y_off[:, j * LANES:(j + 1) * LANES])
    y = jnp.concatenate(ys, axis=1) + dsk_ref[...] * xs
    y = y * _silu(z_ref[0])
    normed = []
    for g in range(SSD_GROUPS):
        yg = y[:, g * SSD_GROUP_WIDTH:(g + 1) * SSD_GROUP_WIDTH]
        normed.append(yg * lax.rsqrt(jnp.mean(yg * yg, axis=-1, keepdims=True) + RMS_EPS))
    y_ssd = jnp.concatenate(normed, axis=1) * ng_ref[...]

    xb = _causal_conv(cbl, lin_ref[0], lcw_ref, lcb_ref)
    xb16 = xb.astype(BF16)
    nblk = LRU_WIDTH // LANES
    ra = jnp.concatenate(
        [_dot(xb16[:, j * LANES:(j + 1) * LANES], wa_ref[j]) for j in range(nblk)], axis=1)
    ix = jnp.concatenate(
        [_dot(xb16[:, j * LANES:(j + 1) * LANES], wx_ref[j]) for j in range(nblk)], axis=1)
    r = jax.nn.sigmoid(ra + ba_ref[...])
    ig = jax.nn.sigmoid(ix + bx_ref[...])
    log_a = (-LRU_C * _softplus(-lam_ref[...])) * r
    a = jnp.exp(log_a)
    mult = jnp.sqrt(jnp.tanh(-log_a) * (a * a + 1.0))
    grow = _row_ids((CHUNK, LRU_WIDTH), c)
    mult = jnp.where(grow == PAD, 1.0, mult)
    u = jnp.where(grow >= PAD, mult * (ig * xb), 0.0)

    r8 = lax.broadcasted_iota(jnp.int32, (8, LRU_WIDTH), 0)
    carry = lc[0:1, :]
    hs = []
    for j in range(CHUNK // 8):
        a8 = a[j * 8:(j + 1) * 8, :]
        u8 = u[j * 8:(j + 1) * 8, :]
        for d in (1, 2, 4):
            a_sh = jnp.where(r8 >= d, pltpu.roll(a8, d, 0), 1.0)
            u_sh = jnp.where(r8 >= d, pltpu.roll(u8, d, 0), 0.0)
            u8 = a8 * u_sh + u8
            a8 = a8 * a_sh
        h8 = a8 * carry + u8
        carry = h8[7:8, :]
        hs.append(h8)
    lc[0:1, :] = carry
    gt = gate_ref[0]
    gelu = 0.5 * gt * (1.0 + jnp.tanh(math.sqrt(2.0 / math.pi) * (gt + 0.044715 * (gt * gt * gt))))
    y_lru = jnp.concatenate(hs, axis=0) * gelu

    ycat = jnp.concatenate([y_ssd, y_lru], axis=1).astype(BF16)
    out = _dot(ycat, wout_ref[...]) + h_ref[0]
    o_ref[0] = jnp.where(grow >= PAD, out, 0.0)


def _mixer(h, z, xbc, gate, lin, dt, params):
    blk = lambda n: pl.BlockSpec((1, CHUNK, n), lambda b, c: (b, c, 0))
    full = lambda a: pl.BlockSpec(a.shape, lambda b, c: (0,) * a.ndim)
    acts = [h, z, xbc, gate, lin, dt]
    act_specs = [blk(a.shape[-1]) for a in acts]
    return pl.pallas_call(
        _mixer_kernel,
        grid=(BATCH, NCHUNK),
        in_specs=act_specs + [full(p) for p in params],
        out_specs=blk(D_MODEL),
        out_shape=jax.ShapeDtypeStruct((BATCH, LP, D_MODEL), F32),
        scratch_shapes=[
            pltpu.VMEM((CHUNK + 8, SSD_CONV_DIM), F32),
            pltpu.VMEM((CHUNK + 8, LRU_WIDTH), F32),
            pltpu.VMEM((SSD_STATE, SSD_WIDTH), F32),
            pltpu.VMEM((8, LRU_WIDTH), F32),
        ],
        compiler_params=pltpu.CompilerParams(
            dimension_semantics=("arbitrary", "arbitrary"), vmem_limit_bytes=VMEM_LIMIT),
        name="ssd_lru_mixer",
    )(*acts, *params)


def _pool_kernel(h_ref, g_ref, pw_ref, pb_ref, ps_ref, o_ref, buf):
    c = pl.program_id(1)

    @pl.when(c == 0)
    def _():
        buf[0:POOL_MAX_WINDOW, :] = jnp.zeros((POOL_MAX_WINDOW, D_MODEL), F32)

    h = h_ref[0]
    hn = _rms(h, g_ref[...])
    base = POOL_MAX_WINDOW
    buf[base:base + CHUNK, :] = hn
    pos = _row_ids((CHUNK, POOL_GROUP_DIM), c) - PAD
    outs = []
    for g, w in enumerate(POOL_WINDOWS):
        sl = slice(g * POOL_GROUP_DIM, (g + 1) * POOL_GROUP_DIM)
        ws = buf[base:base + CHUNK, sl]
        for j in range(1, w):
            ws = ws + buf[base - j:base - j + CHUNK, sl]
        count = jnp.clip(pos + 1, 1, w).astype(F32)
        pooled = ws / count - hn[:, sl]
        outs.append(_dot(pooled.astype(BF16), pw_ref[g]))
    buf[0:base, :] = buf[CHUNK:CHUNK + base, :]
    y = (jnp.concatenate(outs, axis=1) + pb_ref[...]) * ps_ref[...]
    o_ref[0] = jnp.where(_row_ids((CHUNK, D_MODEL), c) >= PAD, h + y, 0.0)


def _pool_mixer(h, gain, pw, pb, ps):
    blk = pl.BlockSpec((1, CHUNK, D_MODEL), lambda b, c: (b, c, 0))
    full = lambda a: pl.BlockSpec(a.shape, lambda b, c: (0,) * a.ndim)
    return pl.pallas_call(
        _pool_kernel,
        grid=(BATCH, NCHUNK),
        in_specs=[blk, full(gain), full(pw), full(pb), full(ps)],
        out_specs=blk,
        out_shape=jax.ShapeDtypeStruct((BATCH, LP, D_MODEL), F32),
        scratch_shapes=[pltpu.VMEM((CHUNK + POOL_MAX_WINDOW, D_MODEL), F32)],
        compiler_params=pltpu.CompilerParams(
            dimension_semantics=("arbitrary", "arbitrary"), vmem_limit_bytes=VMEM_LIMIT),
        name="pool_mixer",
    )(h, gain, pw, pb, ps)


MOE_TM = 512
N_MOE_TILES = TP // MOE_TM
RUN_ALIGN = 16
RUN_SHIFT = 4
LOCAL_ROWS = 1280
EXPERT_TILE = 256
N_ROUTED = BATCH * (N_META + SEQ)
MAX_ROWS = (2 * N_ROUTED + N_MOE_TILES * MOE_EXPERTS * (RUN_ALIGN - 1)
            + MOE_EXPERTS * (EXPERT_TILE - 1))
NT_MAX = -(-MAX_ROWS // EXPERT_TILE)
NR = NT_MAX * EXPERT_TILE
XS_COLS = D_MODEL + LANES
INT_ROWS = 8
TOK_GATE1, TOK_GATE2, TOK_POS1, TOK_POS2 = 0, 3, 6, 7
assert LOCAL_ROWS >= 2 * MOE_TM + MOE_EXPERTS * (RUN_ALIGN - 1) and LOCAL_ROWS % LANES == 0


def _first_argmax(vals):
    best, idx = vals[0], jnp.zeros(vals[0].shape, jnp.int32)
    for k in range(1, len(vals)):
        better = vals[k] > best
        idx = jnp.where(better, k, idx)
        best = jnp.where(better, vals[k], best)
    return idx, best


def _softmax_rows(vals):
    m = functools.reduce(jnp.maximum, vals)
    ex = [jnp.exp(v - m) for v in vals]
    tot = functools.reduce(lambda p, q: p + q, ex)
    return [e / tot for e in ex]


def _bf16_pieces(x):
    hi = x.astype(BF16).astype(F32)
    rest = x - hi
    mid = rest.astype(BF16).astype(F32)
    lo = (rest - mid).astype(BF16).astype(F32)
    return [hi, mid, lo]


def _router_kernel(h_ref, valid_ref, g_ref, wr_ref, br_ref, lpos_ref, tok_ref, runlen_ref, runoff_ref):
    hn = _rms(h_ref[...], g_ref[...])
    logits = _dot(hn, wr_ref[...], HIGHEST) + br_ref[...]
    lt = logits.T
    p_group = _softmax_rows([lt[k:k + 1, :] for k in range(MOE_GROUPS)])
    g_sel, p_g = _first_argmax(p_group)
    fine = []
    for k in range(MOE_PER_GROUP):
        f = lt[ROUTER_EXPERT_ROW + k:ROUTER_EXPERT_ROW + k + 1, :]
        for g in range(1, MOE_GROUPS):
            r0 = ROUTER_EXPERT_ROW + g * MOE_PER_GROUP + k
            f = jnp.where(g_sel == g, lt[r0:r0 + 1, :], f)
        fine.append(f)
    q = _softmax_rows(fine)
    i1, t1 = _first_argmax(q)
    i2, t2 = _first_argmax([jnp.where(i1 == k, -1.0, q[k]) for k in range(MOE_PER_GROUP)])
    tot = t1 + t2
    gate1 = p_g * (t1 / tot)
    gate2 = p_g * (t2 / tot)

    valid = valid_ref[...] > 0
    e1 = jnp.where(valid, g_sel * MOE_PER_GROUP + i1, -1)
    e2 = jnp.where(valid, g_sel * MOE_PER_GROUP + i2, -1)
    erow = lax.broadcasted_iota(jnp.int32, (MOE_EXPERTS, MOE_TM), 0)
    hit1 = erow == e1
    hit2 = erow == e2
    onehot = jnp.where(hit1 | hit2, 1.0, 0.0)
    si = lax.broadcasted_iota(jnp.int32, (MOE_TM, MOE_TM), 0)
    ti = lax.broadcasted_iota(jnp.int32, (MOE_TM, MOE_TM), 1)
    before = jnp.where(si < ti, 1.0, 0.0).astype(BF16)
    seen = _dot(onehot.astype(BF16), before)
    count = jnp.sum(onehot, axis=1, keepdims=True).astype(jnp.int32)
    runlen = lax.shift_left(lax.shift_right_logical(count + (RUN_ALIGN - 1), RUN_SHIFT), RUN_SHIFT)
    runlen_b = jnp.broadcast_to(runlen, (MOE_EXPERTS, LANES))
    ei = lax.broadcasted_iota(jnp.int32, (MOE_EXPERTS, MOE_EXPERTS), 0)
    ej = lax.broadcasted_iota(jnp.int32, (MOE_EXPERTS, MOE_EXPERTS), 1)
    runoff_b = _dot(jnp.where(ej < ei, 1.0, 0.0), runlen_b.astype(F32), HIGHEST)
    runlen_ref[...] = runlen_b
    runoff_ref[...] = runoff_b.astype(jnp.int32)
    place = seen + runoff_b[:, 0:1]
    pos1 = jnp.where(valid, jnp.sum(jnp.where(hit1, place, 0.0), axis=0, keepdims=True), -1.0)
    pos2 = jnp.where(valid, jnp.sum(jnp.where(hit2, place, 0.0), axis=0, keepdims=True), -1.0)
    r8 = lax.broadcasted_iota(jnp.int32, (INT_ROWS, MOE_TM), 0)
    lpos_ref[...] = jnp.where(r8 == 0, pos1.astype(jnp.int32),
                              jnp.where(r8 == 1, pos2.astype(jnp.int32), 0))

    rows = lax.broadcasted_iota(jnp.int32, lt.shape, 0)
    table = jnp.zeros(lt.shape, F32)
    for k, piece in enumerate(_bf16_pieces(gate1)):
        table = jnp.where(rows == TOK_GATE1 + k, piece, table)
    for k, piece in enumerate(_bf16_pieces(gate2)):
        table = jnp.where(rows == TOK_GATE2 + k, piece, table)
    table = jnp.where(rows == TOK_POS1, pos1, jnp.where(rows == TOK_POS2, pos2, table))
    tok_ref[...] = table.T


def _router(h, valid, gain, wr, br):
    row = lambda i: (i, 0)
    col = lambda i: (0, i)
    const = lambda i: (0, 0)
    return pl.pallas_call(
        _router_kernel,
        grid=(N_MOE_TILES,),
        in_specs=[
            pl.BlockSpec((MOE_TM, D_MODEL), row),
            pl.BlockSpec((1, MOE_TM), col),
            pl.BlockSpec((1, D_MODEL), const),
            pl.BlockSpec((D_MODEL, LANES), const),
            pl.BlockSpec((1, LANES), const),
        ],
        out_specs=[pl.BlockSpec((INT_ROWS, MOE_TM), col),
                   pl.BlockSpec((MOE_TM, LANES), row),
                   pl.BlockSpec((MOE_EXPERTS, LANES), row),
                   pl.BlockSpec((MOE_EXPERTS, LANES), row)],
        out_shape=[jax.ShapeDtypeStruct((INT_ROWS, TP), jnp.int32),
                   jax.ShapeDtypeStruct((TP, LANES), F32),
                   jax.ShapeDtypeStruct((N_MOE_TILES * MOE_EXPERTS, LANES), jnp.int32),
                   jax.ShapeDtypeStruct((N_MOE_TILES * MOE_EXPERTS, LANES), jnp.int32)],
        compiler_params=pltpu.CompilerParams(
            dimension_semantics=("arbitrary",), vmem_limit_bytes=VMEM_LIMIT),
        name="moe_router",
    )(h, valid, gain, wr, br)


def _moe_plan(runlen, runoff):
    i32 = jnp.int32
    lens = runlen[:, 0].reshape(N_MOE_TILES, MOE_EXPERTS)
    total = jnp.sum(lens, axis=0)
    padded = (total + (EXPERT_TILE - 1)) // EXPERT_TILE * EXPERT_TILE
    ends = jnp.cumsum(padded)
    base = ends - padded
    goff = base[None, :] + jnp.cumsum(lens, axis=0) - lens
    nval = ends[-1] // EXPERT_TILE
    tidx = jnp.minimum(jnp.arange(NT_MAX, dtype=i32), nval - 1)
    texp = jnp.minimum(jnp.sum(tidx[:, None] >= (ends // EXPERT_TILE)[None, :], axis=1), MOE_EXPERTS - 1)
    return dict(
        goff=goff.reshape(-1).astype(i32), lens=lens.reshape(-1).astype(i32),
        loff=runoff[:, 0].astype(i32), gap_start=(base + total).astype(i32),
        gap_len=(padded - total).astype(i32), nval=nval.reshape(1).astype(i32),
        tidx=tidx.astype(i32), texp=texp.astype(i32),
        ngran=(jnp.sum(lens, axis=1) // RUN_ALIGN).astype(i32))


def _aligned(x):
    return pl.multiple_of(x, RUN_ALIGN)


def _for_each_run(step, lens_ref, fn):
    def expert_body(e, carry):
        k = step * MOE_EXPERTS + e

        def granule_body(q, c):
            fn(k, q * RUN_ALIGN)
            return c

        lax.fori_loop(0, lax.shift_right_logical(lens_ref[k], RUN_SHIFT), granule_body, 0)
        return carry

    lax.fori_loop(0, MOE_EXPERTS, expert_body, 0)


def _dispatch_kernel(goff_ref, lens_ref, loff_ref, gaps_ref, gapl_ref, nval_ref, ngran_ref,
                     h_ref, g_ref, lpos_ref, tok_ref, xs_ref, local, zeros, sems):
    i = pl.program_id(0)
    last = pl.num_programs(0) - 1
    slot = lax.rem(i, 2)

    def run_copy(s, src, dst):
        return pltpu.make_async_copy(local.at[s, pl.ds(_aligned(src), RUN_ALIGN), :],
                                     xs_ref.at[pl.ds(_aligned(dst), RUN_ALIGN), :], sems.at[s])

    def wait_tile(step):
        s = lax.rem(step, 2)

        def body(k, c):
            run_copy(s, 0, 0).wait()
            return c

        lax.fori_loop(0, ngran_ref[step], body, 0)

    @pl.when(i >= 2)
    def _():
        wait_tile(i - 2)

    hn = _rms(h_ref[...], g_ref[...]).astype(BF16)
    riota = lax.broadcasted_iota(jnp.int32, (LOCAL_ROWS, MOE_TM), 0)
    c1 = riota == lpos_ref[0:1, :]
    c2 = riota == lpos_ref[1:2, :]
    local[slot, :, 0:D_MODEL] = _dot(jnp.where(c1 | c2, 1.0, 0.0).astype(BF16), hn).astype(BF16)
    tok = tok_ref[...]
    lane = lax.broadcasted_iota(jnp.int32, (MOE_TM, LANES), 1)
    t1 = jnp.where(lane < TOK_GATE2, tok, 0.0).astype(BF16)
    t2 = jnp.where((lane >= TOK_GATE2) & (lane < TOK_POS1), tok, 0.0).astype(BF16)
    gate_cols = (_dot(jnp.where(c1, 1.0, 0.0).astype(BF16), t1)
                 + _dot(jnp.where(c2, 1.0, 0.0).astype(BF16), t2))
    local[slot, :, D_MODEL:XS_COLS] = gate_cols.astype(BF16)

    _for_each_run(i, lens_ref,
                  lambda k, q: run_copy(slot, loff_ref[k] + q, goff_ref[k] + q).start())

    @pl.when(i == last)
    def _():
        zeros[...] = jnp.zeros_like(zeros)
        zsem = sems.at[2]

        def gap_copy(dst):
            return pltpu.make_async_copy(zeros.at[pl.ds(0, RUN_ALIGN), :],
                                         xs_ref.at[pl.ds(_aligned(dst), RUN_ALIGN), :], zsem)

        def tile_copy(t):
            return pltpu.make_async_copy(
                zeros, xs_ref.at[pl.ds(pl.multiple_of(t * EXPERT_TILE, EXPERT_TILE), EXPERT_TILE), :], zsem)

        def expert_body(e, n):
            g = lax.shift_right_logical(gapl_ref[e], RUN_SHIFT)

            def body(q, c):
                gap_copy(gaps_ref[e] + q * RUN_ALIGN).start()
                return c

            lax.fori_loop(0, g, body, 0)
            return n + g

        ngap = lax.fori_loop(0, MOE_EXPERTS, expert_body, 0)

        def tail_start(t, c):
            tile_copy(t).start()
            return c

        lax.fori_loop(nval_ref[0], NT_MAX, tail_start, 0)

        def gap_wait(k, c):
            gap_copy(0).wait()
            return c

        lax.fori_loop(0, ngap, gap_wait, 0)

        def tail_wait(t, c):
            tile_copy(t).wait()
            return c

        lax.fori_loop(nval_ref[0], NT_MAX, tail_wait, 0)

        @pl.when(i >= 1)
        def _():
            wait_tile(i - 1)

        wait_tile(i)


def _dispatch(plan, h, gain, lpos, tok):
    row = lambda i, *_: (i, 0)
    return pl.pallas_call(
        _dispatch_kernel,
        grid_spec=pltpu.PrefetchScalarGridSpec(
            num_scalar_prefetch=7,
            grid=(N_MOE_TILES,),
            in_specs=[
                pl.BlockSpec((MOE_TM, D_MODEL), row),
                pl.BlockSpec((1, D_MODEL), lambda i, *_: (0, 0)),
                pl.BlockSpec((INT_ROWS, MOE_TM), lambda i, *_: (0, i)),
                pl.BlockSpec((MOE_TM, LANES), row),
            ],
            out_specs=pl.BlockSpec(memory_space=pl.ANY),
            scratch_shapes=[pltpu.VMEM((2, LOCAL_ROWS, XS_COLS), BF16),
                            pltpu.VMEM((EXPERT_TILE, XS_COLS), BF16),
                            pltpu.SemaphoreType.DMA((3,))],
        ),
        out_shape=jax.ShapeDtypeStruct((NR, XS_COLS), BF16),
        compiler_params=pltpu.CompilerParams(
            dimension_semantics=("arbitrary",), vmem_limit_bytes=VMEM_LIMIT),
        name="moe_dispatch",
    )(plan["goff"], plan["lens"], plan["loff"], plan["gap_start"], plan["gap_len"], plan["nval"],
      plan["ngran"], h, gain, lpos, tok)


def _experts_kernel(tidx_ref, texp_ref, nval_ref, xs_ref, wg_ref, wu_ref, wd_ref, ys_ref):
    active = pl.program_id(0) < nval_ref[0]

    @pl.when(active)
    def _():
        x = xs_ref[:, 0:D_MODEL]
        gate = jnp.sum(xs_ref[:, D_MODEL:XS_COLS].astype(F32), axis=1, keepdims=True)
        a = _dot(x, wg_ref[0])
        b = _dot(x, wu_ref[0])
        y = _dot((_silu(a) * b).astype(BF16), wd_ref[0])
        ys_ref[...] = (gate * y).astype(BF16)

    @pl.when(jnp.logical_not(active))
    def _():
        ys_ref[...] = jnp.zeros_like(ys_ref)


def _experts(plan, xs, wg, wu, wd):
    tile = lambda j, tidx, texp, nval: (tidx[j], 0)
    expert = lambda j, tidx, texp, nval: (texp[j], 0, 0)
    return pl.pallas_call(
        _experts_kernel,
        grid_spec=pltpu.PrefetchScalarGridSpec(
            num_scalar_prefetch=3,
            grid=(NT_MAX,),
            in_specs=[
                pl.BlockSpec((EXPERT_TILE, XS_COLS), tile),
                pl.BlockSpec((1, D_MODEL, D_EXPERT), expert),
                pl.BlockSpec((1, D_MODEL, D_EXPERT), expert),
                pl.BlockSpec((1, D_EXPERT, D_MODEL), expert),
            ],
            out_specs=pl.BlockSpec((EXPERT_TILE, D_MODEL), lambda j, *_: (j, 0)),
        ),
        out_shape=jax.ShapeDtypeStruct((NR, D_MODEL), BF16),
        compiler_params=pltpu.CompilerParams(
            dimension_semantics=("arbitrary",), vmem_limit_bytes=VMEM_LIMIT),
        name="moe_experts",
    )(plan["tidx"], plan["texp"], plan["nval"], xs, wg, wu, wd)


def _combine_kernel(goff_ref, lens_ref, loff_ref, ngran_ref, h_ref, tok_ref, ys_ref, o_ref, local, sems):
    i = pl.program_id(0)
    last = pl.num_programs(0) - 1
    slot = lax.rem(i, 2)

    def run_copy(s, src, dst):
        return pltpu.make_async_copy(ys_ref.at[pl.ds(_aligned(src), RUN_ALIGN), :],
                                     local.at[s, pl.ds(_aligned(dst), RUN_ALIGN), :], sems.at[s])

    def start_tile(step):
        s = lax.rem(step, 2)
        _for_each_run(step, lens_ref,
                      lambda k, q: run_copy(s, goff_ref[k] + q, loff_ref[k] + q).start())

    @pl.when(i == 0)
    def _():
        local[...] = jnp.zeros_like(local)
        start_tile(0)

    @pl.when(i < last)
    def _():
        start_tile(i + 1)

    def wait_body(k, c):
        run_copy(slot, 0, 0).wait()
        return c

    lax.fori_loop(0, ngran_ref[i], wait_body, 0)
    tok = tok_ref[...]
    p1 = tok[:, TOK_POS1:TOK_POS1 + 1].astype(jnp.int32)
    p2 = tok[:, TOK_POS2:TOK_POS2 + 1].astype(jnp.int32)
    ciota = lax.broadcasted_iota(jnp.int32, (MOE_TM, LOCAL_ROWS), 1)
    sel = jnp.where((ciota == p1) | (ciota == p2), 1.0, 0.0).astype(BF16)
    o_ref[...] = h_ref[...] + _dot(sel, local[slot])


def _combine(plan, h, tok, ys):
    row = lambda i, *_: (i, 0)
    return pl.pallas_call(
        _combine_kernel,
        grid_spec=pltpu.PrefetchScalarGridSpec(
            num_scalar_prefetch=4,
            grid=(N_MOE_TILES,),
            in_specs=[
                pl.BlockSpec((MOE_TM, D_MODEL), row),
                pl.BlockSpec((MOE_TM, LANES), row),
                pl.BlockSpec(memory_space=pl.ANY),
            ],
            out_specs=pl.BlockSpec((MOE_TM, D_MODEL), row),
            scratch_shapes=[pltpu.VMEM((2, LOCAL_ROWS, D_MODEL), BF16),
                            pltpu.SemaphoreType.DMA((2,))],
        ),
        out_shape=jax.ShapeDtypeStruct((TP, D_MODEL), F32),
        compiler_params=pltpu.CompilerParams(
            dimension_semantics=("arbitrary",), vmem_limit_bytes=VMEM_LIMIT),
        name="moe_combine",
    )(plan["goff"], plan["lens"], plan["loff"], plan["ngran"], h, tok, ys)


def _final_norm_kernel(h_ref, g_ref, o_ref):
    o_ref[0] = _rms(h_ref[0], g_ref[...])


def _final_norm(h, gain):
    return pl.pallas_call(
        _final_norm_kernel,
        grid=(BATCH, SEQ // CHUNK),
        in_specs=[pl.BlockSpec((1, CHUNK, D_MODEL), lambda b, c: (b, c + 1, 0)),
                  pl.BlockSpec((1, D_MODEL), lambda b, c: (0, 0))],
        out_specs=pl.BlockSpec((1, CHUNK, D_MODEL), lambda b, c: (b, c, 0)),
        out_shape=jax.ShapeDtypeStruct((BATCH, SEQ, D_MODEL), F32),
        compiler_params=pltpu.CompilerParams(dimension_semantics=("arbitrary", "arbitrary")),
        name="final_norm",
    )(h, gain)


def _row(v):
    return v.reshape(1, -1).astype(F32)


def _pad_lanes(v):
    return jnp.pad(_row(v), ((0, 0), (0, LANES - v.shape[-1])))


def _pair_blockdiag(w):
    w = w.reshape(LRU_WIDTH // LANES, 2, 64, 64)
    z = jnp.zeros_like(w[:, 0])
    top = jnp.concatenate([w[:, 0], z], axis=2)
    bot = jnp.concatenate([z, w[:, 1]], axis=2)
    return jnp.concatenate([top, bot], axis=1).astype(BF16)


def _moe_layer(h, layer, valid, ffn_norm, rgw, rgb, rew, reb, wg, wu, wd):
    wr = jnp.zeros((D_MODEL, LANES), F32)
    wr = wr.at[:, 0:MOE_GROUPS].set(rgw[layer])
    wr = wr.at[:, ROUTER_EXPERT_ROW:ROUTER_EXPERT_ROW + MOE_EXPERTS].set(rew[layer])
    br = jnp.zeros((1, LANES), F32)
    br = br.at[0, 0:MOE_GROUPS].set(rgb[layer])
    br = br.at[0, ROUTER_EXPERT_ROW:ROUTER_EXPERT_ROW + MOE_EXPERTS].set(reb[layer])
    gain = _row(ffn_norm[layer])
    lpos, tok, runlen, runoff = _router(h, valid, gain, wr, br)
    plan = _moe_plan(runlen, runoff)
    xs = _dispatch(plan, h, gain, lpos, tok)
    ys = _experts(plan, xs, wg[layer].astype(BF16), wu[layer].astype(BF16), wd[layer].astype(BF16))
    return _combine(plan, h, tok, ys)


def kernel(x, meta_tokens, norm_final, mix_norm_even, w_in, ssd_conv_w, ssd_conv_b, ssd_dt_bias, ssd_a_log, ssd_d, ssd_norm, lru_conv_w, lru_conv_b, lru_w_a, lru_b_a, lru_w_x, lru_b_x, lru_lambda, w_out, mix_norm_odd, pool_w, pool_b, pool_scale, ffn_norm, router_group_w, router_group_b, router_expert_w, router_expert_b, expert_w_gate, expert_w_up, expert_w_down):
    meta = jnp.broadcast_to(meta_tokens.astype(F32)[None], (BATCH, N_META, D_MODEL))
    h = jnp.concatenate([jnp.zeros((BATCH, PAD, D_MODEL), F32), meta, x.astype(F32)], axis=1)
    h = h.reshape(TP, D_MODEL)
    valid = ((jnp.arange(TP, dtype=jnp.int32) % LP) >= PAD).astype(jnp.int32).reshape(1, TP)
    moe_args = (valid, ffn_norm, router_group_w, router_group_b, router_expert_w, router_expert_b,
                expert_w_gate, expert_w_up, expert_w_down)

    wi = w_in[0]
    w_proj = jnp.concatenate(
        [wi[:, 0:1024], wi[:, 1024:2560], wi[:, 2576:3600], wi[:, 3600:4624], wi[:, 2560:2576],
         jnp.zeros((D_MODEL, LANES - SSD_HEADS), F32)], axis=1).astype(BF16)
    z, xbc, gate, lin, dt = _in_proj(h, _row(mix_norm_even[0]), w_proj)
    expand = (jnp.arange(LANES)[:, None] == (jnp.arange(SSD_WIDTH) // SSD_HEAD_DIM)[None, :]).astype(F32)
    params = [
        ssd_conv_w[0].astype(F32), _row(ssd_conv_b[0]), _pad_lanes(ssd_dt_bias[0]), _pad_lanes(ssd_a_log[0]),
        _row(jnp.repeat(ssd_d[0], SSD_HEAD_DIM)), _row(ssd_norm[0]),
        lru_conv_w[0].astype(F32), _row(lru_conv_b[0]), _pair_blockdiag(lru_w_a[0]), _row(lru_b_a[0]),
        _pair_blockdiag(lru_w_x[0]), _row(lru_b_x[0]), _row(lru_lambda[0]),
        w_out[0].astype(BF16), expand,
    ]
    r3 = lambda a: a.reshape(BATCH, LP, a.shape[-1])
    h = _mixer(r3(h), r3(z), r3(xbc), r3(gate), r3(lin), r3(dt), params).reshape(TP, D_MODEL)
    h = _moe_layer(h, 0, *moe_args)

    h = _pool_mixer(h.reshape(BATCH, LP, D_MODEL), _row(mix_norm_odd[0]), pool_w[0].astype(BF16),
                    _row(pool_b[0]), _row(pool_scale[0])).reshape(TP, D_MODEL)
    h = _moe_layer(h, 1, *moe_args)

    return _final_norm(h.reshape(BATCH, LP, D_MODEL), _row(norm_final))
```

```python
import functools
import math

import jax
import jax.numpy as jnp
from jax import lax
from jax.experimental import pallas as pl
from jax.experimental.pallas import tpu as pltpu

F32 = jnp.float32
BF16 = jnp.bfloat16
HIGHEST = lax.Precision.HIGHEST

D_MODEL = 1024
BATCH = 8
SEQ = 2048
N_META = 16
RMS_EPS = 1e-6
CONV_WIDTH = 4
CHUNK = 128
PAD = CHUNK - N_META
LP = PAD + N_META + SEQ
NCHUNK = LP // CHUNK
TP = BATCH * LP

SSD_HEADS = 16
SSD_HEAD_DIM = 64
SSD_WIDTH = 1024
SSD_GROUPS = 2
SSD_STATE = 128
SSD_GROUP_WIDTH = SSD_WIDTH // SSD_GROUPS
SSD_CONV_DIM = SSD_WIDTH + 2 * SSD_GROUPS * SSD_STATE
LRU_WIDTH = 1024
LRU_C = 8.0
LANES = 128
POOL_WINDOWS = (2, 4, 8, 16)
POOL_GROUP_DIM = 256
POOL_MAX_WINDOW = 16
MOE_GROUPS = 4
MOE_PER_GROUP = 4
MOE_EXPERTS = 16
D_EXPERT = 512
ROUTER_EXPERT_ROW = 8

VMEM_LIMIT = 56 * 1024 * 1024


def _dot(a, b, precision=None):
    return jnp.dot(a, b, preferred_element_type=F32, precision=precision)


def _rms(x, gain):
    ms = jnp.mean(x * x, axis=-1, keepdims=True)
    return x * lax.rsqrt(ms + RMS_EPS) * gain


def _silu(x):
    return x * jax.nn.sigmoid(x)


def _softplus(x):
    return jnp.maximum(x, 0.0) + jnp.log1p(jnp.exp(-jnp.abs(x)))


def _row_ids(shape, chunk):
    return lax.broadcasted_iota(jnp.int32, shape, 0) + chunk * CHUNK


PROJ_TM = 256
_Z0, _X0, _G0, _L0, _T0, _PEND = 0, 1024, 2560, 3584, 4608, 4736


def _in_proj_kernel(h_ref, g_ref, w_ref, z_ref, xbc_ref, gate_ref, lin_ref, dt_ref):
    hn = _rms(h_ref[...], g_ref[...]).astype(BF16)
    z_ref[...] = _dot(hn, w_ref[:, _Z0:_X0])
    xbc_ref[...] = _dot(hn, w_ref[:, _X0:_G0])
    gate_ref[...] = _dot(hn, w_ref[:, _G0:_L0])
    lin_ref[...] = _dot(hn, w_ref[:, _L0:_T0])
    dt_ref[...] = _dot(hn, w_ref[:, _T0:_PEND])


def _in_proj(h, gain, w):
    widths = (1024, SSD_CONV_DIM, 1024, 1024, LANES)
    row = lambda i: (i, 0)
    return pl.pallas_call(
        _in_proj_kernel,
        grid=(TP // PROJ_TM,),
        in_specs=[
            pl.BlockSpec((PROJ_TM, D_MODEL), row),
            pl.BlockSpec((1, D_MODEL), lambda i: (0, 0)),
            pl.BlockSpec((D_MODEL, _PEND), lambda i: (0, 0)),
        ],
        out_specs=[pl.BlockSpec((PROJ_TM, n), row) for n in widths],
        out_shape=[jax.ShapeDtypeStruct((TP, n), F32) for n in widths],
        compiler_params=pltpu.CompilerParams(
            dimension_semantics=("arbitrary",), vmem_limit_bytes=VMEM_LIMIT),
        name="in_proj",
    )(h, gain, w)


def _causal_conv(buf, x, w_ref, b_ref):
    buf[8:8 + CHUNK, :] = x
    acc = b_ref[...] + w_ref[0:1, :] * buf[5:5 + CHUNK, :]
    for k in range(1, CONV_WIDTH):
        acc = acc + w_ref[k:k + 1, :] * buf[5 + k:5 + k + CHUNK, :]
    buf[0:8, :] = buf[CHUNK:CHUNK + 8, :]
    return acc


def _mixer_kernel(h_ref, z_ref, xbc_ref, gate_ref, lin_ref, dt_ref,
                  cw_ref, cb_ref, dtb_ref, alog_ref, dsk_ref, ng_ref,
                  lcw_ref, lcb_ref, wa_ref, ba_ref, wx_ref, bx_ref, lam_ref,
                  wout_ref, exp_ref, o_ref, cbx, cbl, st, lc):
    c = pl.program_id(1)

    @pl.when(c == 0)
    def _():
        cbx[0:8, :] = jnp.zeros((8, SSD_CONV_DIM), F32)
        cbl[0:8, :] = jnp.zeros((8, LRU_WIDTH), F32)
        st[...] = jnp.zeros_like(st)
        lc[...] = jnp.zeros_like(lc)

    xc = _silu(_causal_conv(cbx, xbc_ref[0], cw_ref, cb_ref))
    xs = xc[:, 0:SSD_WIDTH]
    valid_s = _row_ids((CHUNK, 2 * SSD_STATE), c) >= PAD
    bm = jnp.where(valid_s, xc[:, SSD_WIDTH:SSD_WIDTH + 2 * SSD_STATE], 0.0)
    cm = jnp.where(valid_s, xc[:, SSD_WIDTH + 2 * SSD_STATE:], 0.0)

    li = lax.broadcasted_iota(jnp.int32, (CHUNK, CHUNK), 0)
    si = lax.broadcasted_iota(jnp.int32, (CHUNK, CHUNK), 1)
    causal = si <= li
    dt = _softplus(dt_ref[0] + dtb_ref[...])
    dt = jnp.where(li + c * CHUNK >= PAD, dt, 0.0)
    adt = dt * (-jnp.exp(alog_ref[...]))
    a_cs = _dot(causal.astype(F32), adt, HIGHEST)
    a_last = a_cs[CHUNK - 1:CHUNK, :]
    ea = jnp.exp(a_cs)
    expand = exp_ref[...]
    dt_x = _dot(dt, expand, HIGHEST)
    w_x = _dot(jnp.exp(a_last - a_cs) * dt, expand, HIGHEST)
    ea_x = _dot(ea, expand, HIGHEST)
    xdt = xs * dt_x
    a_cs_t = a_cs.T
    lane = lax.broadcasted_iota(jnp.int32, (CHUNK, LANES), 1)

    ys = []
    for g in range(SSD_GROUPS):
        gsl = slice(g * SSD_GROUP_WIDTH, (g + 1) * SSD_GROUP_WIDTH)
        bg = bm[:, g * SSD_STATE:(g + 1) * SSD_STATE]
        cg16 = cm[:, g * SSD_STATE:(g + 1) * SSD_STATE].astype(BF16)
        cbm = lax.dot_general(cg16, bg.astype(BF16), (((1,), (1,)), ((), ())),
                              preferred_element_type=F32)
        s_in = st[:, gsl]
        y_off = _dot(cg16, s_in.astype(BF16)) * ea_x[:, gsl]
        s_new = _dot(bg.T.astype(BF16), (w_x[:, gsl] * xs[:, gsl]).astype(BF16))
        st[:, gsl] = ea_x[CHUNK - 1:CHUNK, gsl] * s_in + s_new
        for j in range(SSD_GROUP_WIDTH // LANES):
            xp = xdt[:, g * SSD_GROUP_WIDTH + j * LANES:
                     g * SSD_GROUP_WIDTH + (j + 1) * LANES].astype(BF16)
            parts = []
            for hh in range(LANES // SSD_HEAD_DIM):
                hd = g * (SSD_HEADS // SSD_GROUPS) + j * (LANES // SSD_HEAD_DIM) + hh
                seg = a_cs[:, hd:hd + 1] - a_cs_t[hd:hd + 1, :]
                dec = jnp.exp(jnp.where(causal, seg, -1e30))
                parts.append(_dot((cbm * dec).astype(BF16), xp))
            ys.append(jnp.where(lane < SSD_HEAD_DIM, parts[0], parts[1])
                      + y_off[:, j * LANES:(j + 1) * LANES])
    y = jnp.concatenate(ys, axis=1) + dsk_ref[...] * xs
    y = y * _silu(z_ref[0])
    normed = []
    for g in range(SSD_GROUPS):
        yg = y[:, g * SSD_GROUP_WIDTH:(g + 1) * SSD_GROUP_WIDTH]
        normed.append(yg * lax.rsqrt(jnp.mean(yg * yg, axis=-1, keepdims=True) + RMS_EPS))
    y_ssd = jnp.concatenate(normed, axis=1) * ng_ref[...]

    xb = _causal_conv(cbl, lin_ref[0], lcw_ref, lcb_ref)
    xb16 = xb.astype(BF16)
    nblk = LRU_WIDTH // LANES
    ra = jnp.concatenate(
        [_dot(xb16[:, j * LANES:(j + 1) * LANES], wa_ref[j]) for j in range(nblk)], axis=1)
    ix = jnp.concatenate(
        [_dot(xb16[:, j * LANES:(j + 1) * LANES], wx_ref[j]) for j in range(nblk)], axis=1)
    r = jax.nn.sigmoid(ra + ba_ref[...])
    ig = jax.nn.sigmoid(ix + bx_ref[...])
    log_a = (-LRU_C * _softplus(-lam_ref[...])) * r
    a = jnp.exp(log_a)
    mult = jnp.sqrt(jnp.tanh(-log_a) * (a * a + 1.0))
    grow = _row_ids((CHUNK, LRU_WIDTH), c)
    mult = jnp.where(grow == PAD, 1.0, mult)
    u = jnp.where(grow >= PAD, mult * (ig * xb), 0.0)

    r8 = lax.broadcasted_iota(jnp.int32, (8, LRU_WIDTH), 0)
    carry = lc[0:1, :]
    hs = []
    for j in range(CHUNK // 8):
        a8 = a[j * 8:(j + 1) * 8, :]
        u8 = u[j * 8:(j + 1) * 8, :]
        for d in (1, 2, 4):
            a_sh = jnp.where(r8 >= d, pltpu.roll(a8, d, 0), 1.0)
            u_sh = jnp.where(r8 >= d, pltpu.roll(u8, d, 0), 0.0)
            u8 = a8 * u_sh + u8
            a8 = a8 * a_sh
        h8 = a8 * carry + u8
        carry = h8[7:8, :]
        hs.append(h8)
    lc[0:1, :] = carry
    gt = gate_ref[0]
    gelu = 0.5 * gt * (1.0 + jnp.tanh(math.sqrt(2.0 / math.pi) * (gt + 0.044715 * (gt * gt * gt))))
    y_lru = jnp.concatenate(hs, axis=0) * gelu

    ycat = jnp.concatenate([y_ssd, y_lru], axis=1).astype(BF16)
    out = _dot(ycat, wout_ref[...]) + h_ref[0]
    o_ref[0] = jnp.where(grow >= PAD, out, 0.0)


def _mixer(h, z, xbc, gate, lin, dt, params):
    blk = lambda n: pl.BlockSpec((1, CHUNK, n), lambda b, c: (b, c, 0))
    full = lambda a: pl.BlockSpec(a.shape, lambda b, c: (0,) * a.ndim)
    acts = [h, z, xbc, gate, lin, dt]
    act_specs = [blk(a.shape[-1]) for a in acts]
    return pl.pallas_call(
        _mixer_kernel,
        grid=(BATCH, NCHUNK),
        in_specs=act_specs + [full(p) for p in params],
        out_specs=blk(D_MODEL),
        out_shape=jax.ShapeDtypeStruct((BATCH, LP, D_MODEL), F32),
        scratch_shapes=[
            pltpu.VMEM((CHUNK + 8, SSD_CONV_DIM), F32),
            pltpu.VMEM((CHUNK + 8, LRU_WIDTH), F32),
            pltpu.VMEM((SSD_STATE, SSD_WIDTH), F32),
            pltpu.VMEM((8, LRU_WIDTH), F32),
        ],
        compiler_params=pltpu.CompilerParams(
            dimension_semantics=("arbitrary", "arbitrary"), vmem_limit_bytes=VMEM_LIMIT),
        name="ssd_lru_mixer",
    )(*acts, *params)


POOL_TM = 512


def _pool_kernel(h_ref, g_ref, pw_ref, pb_ref, ps_ref, o_ref, buf):
    base = POOL_MAX_WINDOW
    for k in range(POOL_TM // CHUNK):
        chunk = pl.program_id(0) * (POOL_TM // CHUNK) + k
        c = lax.rem(chunk, NCHUNK)

        @pl.when(c == 0)
        def _():
            buf[0:base, :] = jnp.zeros((base, D_MODEL), F32)

        h = h_ref[k * CHUNK:(k + 1) * CHUNK, :]
        hn = _rms(h, g_ref[...])
        buf[base:base + CHUNK, :] = hn
        pos = _row_ids((CHUNK, POOL_GROUP_DIM), c) - PAD
        outs = []
        for g, w in enumerate(POOL_WINDOWS):
            sl = slice(g * POOL_GROUP_DIM, (g + 1) * POOL_GROUP_DIM)
            ws = buf[:, sl]
            s = 1
            while s < w:
                ws = ws + pltpu.roll(ws, s, 0)
                s *= 2
            ws = ws[base:, :]
            count = jnp.clip(pos + 1, 1, w).astype(F32)
            pooled = ws / count - hn[:, sl]
            outs.append(_dot(pooled.astype(BF16), pw_ref[g]))
        buf[0:base, :] = buf[CHUNK:CHUNK + base, :]
        y = (jnp.concatenate(outs, axis=1) + pb_ref[...]) * ps_ref[...]
        o_ref[k * CHUNK:(k + 1) * CHUNK, :] = jnp.where(_row_ids((CHUNK, D_MODEL), c) >= PAD, h + y, 0.0)


def _pool_mixer(h, gain, pw, pb, ps):
    blk = pl.BlockSpec((POOL_TM, D_MODEL), lambda i: (i, 0))
    full = lambda a: pl.BlockSpec(a.shape, lambda i: (0,) * a.ndim)
    return pl.pallas_call(
        _pool_kernel,
        grid=(TP // POOL_TM,),
        in_specs=[blk, full(gain), full(pw), full(pb), full(ps)],
        out_specs=blk,
        out_shape=jax.ShapeDtypeStruct((TP, D_MODEL), F32),
        scratch_shapes=[pltpu.VMEM((CHUNK + POOL_MAX_WINDOW, D_MODEL), F32)],
        compiler_params=pltpu.CompilerParams(
            dimension_semantics=("arbitrary",), vmem_limit_bytes=VMEM_LIMIT),
        name="pool_mixer",
    )(h, gain, pw, pb, ps)


MOE_TM = 512
N_MOE_TILES = TP // MOE_TM
RUN_ALIGN = 16
RUN_SHIFT = 4
LOCAL_ROWS = 1280
EXPERT_TILE = 512
N_ROUTED = BATCH * (N_META + SEQ)
MAX_ROWS = (2 * N_ROUTED + N_MOE_TILES * MOE_EXPERTS * (RUN_ALIGN - 1)
            + MOE_EXPERTS * (EXPERT_TILE - 1))
NT_MAX = -(-MAX_ROWS // EXPERT_TILE)
NR = NT_MAX * EXPERT_TILE
XS_COLS = D_MODEL + LANES
INT_ROWS = 8
TOK_GATE1, TOK_GATE2, TOK_POS1, TOK_POS2 = 0, 3, 6, 7
assert LOCAL_ROWS >= 2 * MOE_TM + MOE_EXPERTS * (RUN_ALIGN - 1) and LOCAL_ROWS % LANES == 0


def _first_argmax(vals):
    best, idx = vals[0], jnp.zeros(vals[0].shape, jnp.int32)
    for k in range(1, len(vals)):
        better = vals[k] > best
        idx = jnp.where(better, k, idx)
        best = jnp.where(better, vals[k], best)
    return idx, best


def _softmax_rows(vals):
    m = functools.reduce(jnp.maximum, vals)
    ex = [jnp.exp(v - m) for v in vals]
    tot = functools.reduce(lambda p, q: p + q, ex)
    return [e / tot for e in ex]


def _bf16_pieces(x):
    hi = x.astype(BF16).astype(F32)
    rest = x - hi
    mid = rest.astype(BF16).astype(F32)
    lo = (rest - mid).astype(BF16).astype(F32)
    return [hi, mid, lo]


def _router_kernel(h_ref, valid_ref, g_ref, wrh_ref, wrl_ref, br_ref, lpos_ref, tok_ref, runlen_ref, runoff_ref):
    hn = _rms(h_ref[...], g_ref[...])
    hn_hi = hn.astype(BF16)
    hn_lo = (hn - hn_hi.astype(F32)).astype(BF16)
    logits = (_dot(hn_hi, wrh_ref[...]) + (_dot(hn_hi, wrl_ref[...]) + _dot(hn_lo, wrh_ref[...]))
              + br_ref[...])
    lt = logits.T
    p_group = _softmax_rows([lt[k:k + 1, :] for k in range(MOE_GROUPS)])
    g_sel, p_g = _first_argmax(p_group)
    fine = []
    for k in range(MOE_PER_GROUP):
        f = lt[ROUTER_EXPERT_ROW + k:ROUTER_EXPERT_ROW + k + 1, :]
        for g in range(1, MOE_GROUPS):
            r0 = ROUTER_EXPERT_ROW + g * MOE_PER_GROUP + k
            f = jnp.where(g_sel == g, lt[r0:r0 + 1, :], f)
        fine.append(f)
    q = _softmax_rows(fine)
    i1, t1 = _first_argmax(q)
    i2, t2 = _first_argmax([jnp.where(i1 == k, -1.0, q[k]) for k in range(MOE_PER_GROUP)])
    tot = t1 + t2
    gate1 = p_g * (t1 / tot)
    gate2 = p_g * (t2 / tot)

    valid = valid_ref[...] > 0
    e1 = jnp.where(valid, g_sel * MOE_PER_GROUP + i1, -1)
    e2 = jnp.where(valid, g_sel * MOE_PER_GROUP + i2, -1)
    erow = lax.broadcasted_iota(jnp.int32, (MOE_EXPERTS, MOE_TM), 0)
    hit1 = erow == e1
    hit2 = erow == e2
    onehot = jnp.where(hit1 | hit2, 1.0, 0.0)
    si = lax.broadcasted_iota(jnp.int32, (MOE_TM, MOE_TM), 0)
    ti = lax.broadcasted_iota(jnp.int32, (MOE_TM, MOE_TM), 1)
    before = jnp.where(si < ti, 1.0, 0.0).astype(BF16)
    seen = _dot(onehot.astype(BF16), before)
    count = jnp.sum(onehot, axis=1, keepdims=True).astype(jnp.int32)
    runlen = lax.shift_left(lax.shift_right_logical(count + (RUN_ALIGN - 1), RUN_SHIFT), RUN_SHIFT)
    runlen_b = jnp.broadcast_to(runlen, (MOE_EXPERTS, LANES))
    ei = lax.broadcasted_iota(jnp.int32, (MOE_EXPERTS, MOE_EXPERTS), 0)
    ej = lax.broadcasted_iota(jnp.int32, (MOE_EXPERTS, MOE_EXPERTS), 1)
    runoff_b = _dot(jnp.where(ej < ei, 1.0, 0.0), runlen_b.astype(F32), HIGHEST)
    runlen_ref[...] = runlen_b
    runoff_ref[...] = runoff_b.astype(jnp.int32)
    place = seen + runoff_b[:, 0:1]
    pos1 = jnp.where(valid, jnp.sum(jnp.where(hit1, place, 0.0), axis=0, keepdims=True), -1.0)
    pos2 = jnp.where(valid, jnp.sum(jnp.where(hit2, place, 0.0), axis=0, keepdims=True), -1.0)
    r8 = lax.broadcasted_iota(jnp.int32, (INT_ROWS, MOE_TM), 0)
    lpos_ref[...] = jnp.where(r8 == 0, pos1.astype(jnp.int32),
                              jnp.where(r8 == 1, pos2.astype(jnp.int32), 0))

    rows = lax.broadcasted_iota(jnp.int32, lt.shape, 0)
    table = jnp.zeros(lt.shape, F32)
    for k, piece in enumerate(_bf16_pieces(gate1)):
        table = jnp.where(rows == TOK_GATE1 + k, piece, table)
    for k, piece in enumerate(_bf16_pieces(gate2)):
        table = jnp.where(rows == TOK_GATE2 + k, piece, table)
    table = jnp.where(rows == TOK_POS1, pos1, jnp.where(rows == TOK_POS2, pos2, table))
    tok_ref[...] = table.T


def _router(h, valid, gain, wr, br):
    row = lambda i: (i, 0)
    col = lambda i: (0, i)
    const = lambda i: (0, 0)
    wr_hi = wr.astype(BF16)
    wr_lo = (wr - wr_hi.astype(F32)).astype(BF16)
    return pl.pallas_call(
        _router_kernel,
        grid=(N_MOE_TILES,),
        in_specs=[
            pl.BlockSpec((MOE_TM, D_MODEL), row),
            pl.BlockSpec((1, MOE_TM), col),
            pl.BlockSpec((1, D_MODEL), const),
            pl.BlockSpec((D_MODEL, LANES), const),
            pl.BlockSpec((D_MODEL, LANES), const),
            pl.BlockSpec((1, LANES), const),
        ],
        out_specs=[pl.BlockSpec((INT_ROWS, MOE_TM), col),
                   pl.BlockSpec((MOE_TM, LANES), row),
                   pl.BlockSpec((MOE_EXPERTS, LANES), row),
                   pl.BlockSpec((MOE_EXPERTS, LANES), row)],
        out_shape=[jax.ShapeDtypeStruct((INT_ROWS, TP), jnp.int32),
                   jax.ShapeDtypeStruct((TP, LANES), F32),
                   jax.ShapeDtypeStruct((N_MOE_TILES * MOE_EXPERTS, LANES), jnp.int32),
                   jax.ShapeDtypeStruct((N_MOE_TILES * MOE_EXPERTS, LANES), jnp.int32)],
        compiler_params=pltpu.CompilerParams(
            dimension_semantics=("arbitrary",), vmem_limit_bytes=VMEM_LIMIT),
        name="moe_router",
    )(h, valid, gain, wr_hi, wr_lo, br)


def _moe_plan(runlen, runoff):
    i32 = jnp.int32
    lens = runlen[:, 0].reshape(N_MOE_TILES, MOE_EXPERTS)
    total = jnp.sum(lens, axis=0)
    padded = (total + (EXPERT_TILE - 1)) // EXPERT_TILE * EXPERT_TILE
    ends = jnp.cumsum(padded)
    base = ends - padded
    goff = base[None, :] + jnp.cumsum(lens, axis=0) - lens
    nval = ends[-1] // EXPERT_TILE
    tidx = jnp.minimum(jnp.arange(NT_MAX, dtype=i32), nval - 1)
    texp = jnp.minimum(jnp.sum(tidx[:, None] >= (ends // EXPERT_TILE)[None, :], axis=1), MOE_EXPERTS - 1)
    return dict(
        goff=goff.reshape(-1).astype(i32), lens=lens.reshape(-1).astype(i32),
        loff=runoff[:, 0].astype(i32), gap_start=(base + total).astype(i32),
        gap_len=(padded - total).astype(i32), nval=nval.reshape(1).astype(i32),
        tidx=tidx.astype(i32), texp=texp.astype(i32),
        ngran=(jnp.sum(lens, axis=1) // RUN_ALIGN).astype(i32))


def _aligned(x):
    return pl.multiple_of(x, RUN_ALIGN)


def _for_each_run(step, lens_ref, fn):
    def expert_body(e, carry):
        k = step * MOE_EXPERTS + e

        def granule_body(q, c):
            fn(k, q * RUN_ALIGN)
            return c

        lax.fori_loop(0, lax.shift_right_logical(lens_ref[k], RUN_SHIFT), granule_body, 0)
        return carry

    lax.fori_loop(0, MOE_EXPERTS, expert_body, 0)


def _dispatch_kernel(goff_ref, lens_ref, loff_ref, gaps_ref, gapl_ref, nval_ref, ngran_ref,
                     h_ref, g_ref, lpos_ref, tok_ref, xs_ref, local, zeros, sems):
    i = pl.program_id(0)
    last = pl.num_programs(0) - 1
    slot = lax.rem(i, 2)

    def run_copy(s, src, dst):
        return pltpu.make_async_copy(local.at[s, pl.ds(_aligned(src), RUN_ALIGN), :],
                                     xs_ref.at[pl.ds(_aligned(dst), RUN_ALIGN), :], sems.at[s])

    def wait_tile(step):
        s = lax.rem(step, 2)

        def body(k, c):
            run_copy(s, 0, 0).wait()
            return c

        lax.fori_loop(0, ngran_ref[step], body, 0)

    @pl.when(i >= 2)
    def _():
        wait_tile(i - 2)

    hn = _rms(h_ref[...], g_ref[...]).astype(BF16)
    riota = lax.broadcasted_iota(jnp.int32, (LOCAL_ROWS, MOE_TM), 0)
    c1 = riota == lpos_ref[0:1, :]
    c2 = riota == lpos_ref[1:2, :]
    local[slot, :, 0:D_MODEL] = _dot(jnp.where(c1 | c2, 1.0, 0.0).astype(BF16), hn).astype(BF16)
    tok = tok_ref[...]
    lane = lax.broadcasted_iota(jnp.int32, (MOE_TM, LANES), 1)
    t1 = jnp.where(lane < TOK_GATE2, tok, 0.0).astype(BF16)
    t2 = jnp.where((lane >= TOK_GATE2) & (lane < TOK_POS1), tok, 0.0).astype(BF16)
    gate_cols = (_dot(jnp.where(c1, 1.0, 0.0).astype(BF16), t1)
                 + _dot(jnp.where(c2, 1.0, 0.0).astype(BF16), t2))
    local[slot, :, D_MODEL:XS_COLS] = gate_cols.astype(BF16)

    _for_each_run(i, lens_ref,
                  lambda k, q: run_copy(slot, loff_ref[k] + q, goff_ref[k] + q).start())

    @pl.when(i == last)
    def _():
        zeros[...] = jnp.zeros_like(zeros)
        zsem = sems.at[2]

        def gap_copy(dst):
            return pltpu.make_async_copy(zeros.at[pl.ds(0, RUN_ALIGN), :],
                                         xs_ref.at[pl.ds(_aligned(dst), RUN_ALIGN), :], zsem)

        def tile_copy(t):
            return pltpu.make_async_copy(
                zeros, xs_ref.at[pl.ds(pl.multiple_of(t * EXPERT_TILE, EXPERT_TILE), EXPERT_TILE), :], zsem)

        def expert_body(e, n):
            g = lax.shift_right_logical(gapl_ref[e], RUN_SHIFT)

            def body(q, c):
                gap_copy(gaps_ref[e] + q * RUN_ALIGN).start()
                return c

            lax.fori_loop(0, g, body, 0)
            return n + g

        ngap = lax.fori_loop(0, MOE_EXPERTS, expert_body, 0)

        def tail_start(t, c):
            tile_copy(t).start()
            return c

        lax.fori_loop(nval_ref[0], NT_MAX, tail_start, 0)

        def gap_wait(k, c):
            gap_copy(0).wait()
            return c

        lax.fori_loop(0, ngap, gap_wait, 0)

        def tail_wait(t, c):
            tile_copy(t).wait()
            return c

        lax.fori_loop(nval_ref[0], NT_MAX, tail_wait, 0)

        @pl.when(i >= 1)
        def _():
            wait_tile(i - 1)

        wait_tile(i)


def _dispatch(plan, h, gain, lpos, tok):
    row = lambda i, *_: (i, 0)
    return pl.pallas_call(
        _dispatch_kernel,
        grid_spec=pltpu.PrefetchScalarGridSpec(
            num_scalar_prefetch=7,
            grid=(N_MOE_TILES,),
            in_specs=[
                pl.BlockSpec((MOE_TM, D_MODEL), row),
                pl.BlockSpec((1, D_MODEL), lambda i, *_: (0, 0)),
                pl.BlockSpec((INT_ROWS, MOE_TM), lambda i, *_: (0, i)),
                pl.BlockSpec((MOE_TM, LANES), row),
            ],
            out_specs=pl.BlockSpec(memory_space=pl.ANY),
            scratch_shapes=[pltpu.VMEM((2, LOCAL_ROWS, XS_COLS), BF16),
                            pltpu.VMEM((EXPERT_TILE, XS_COLS), BF16),
                            pltpu.SemaphoreType.DMA((3,))],
        ),
        out_shape=jax.ShapeDtypeStruct((NR, XS_COLS), BF16),
        compiler_params=pltpu.CompilerParams(
            dimension_semantics=("arbitrary",), vmem_limit_bytes=VMEM_LIMIT),
        name="moe_dispatch",
    )(plan["goff"], plan["lens"], plan["loff"], plan["gap_start"], plan["gap_len"], plan["nval"],
      plan["ngran"], h, gain, lpos, tok)


def _experts_kernel(tidx_ref, texp_ref, nval_ref, xs_ref, wg_ref, wu_ref, wd_ref, ys_ref, wg16, wu16, wd16):
    j = pl.program_id(0)
    active = j < nval_ref[0]
    new_expert = (j == 0) | (texp_ref[j] != texp_ref[jnp.maximum(j - 1, 0)])

    @pl.when(active & new_expert)
    def _():
        wg16[...] = wg_ref[0].astype(BF16)
        wu16[...] = wu_ref[0].astype(BF16)
        wd16[...] = wd_ref[0].astype(BF16)

    @pl.when(active)
    def _():
        x = xs_ref[:, 0:D_MODEL]
        gate = jnp.sum(xs_ref[:, D_MODEL:XS_COLS].astype(F32), axis=1, keepdims=True)
        a = _dot(x, wg16[...])
        b = _dot(x, wu16[...])
        y = _dot((_silu(a) * b).astype(BF16), wd16[...])
        ys_ref[...] = (gate * y).astype(BF16)

    @pl.when(jnp.logical_not(active))
    def _():
        ys_ref[...] = jnp.zeros_like(ys_ref)


def _experts(plan, xs, layer, wg, wu, wd):
    tile = lambda j, tidx, texp, nval: (tidx[j], 0)
    expert = lambda j, tidx, texp, nval: (layer, texp[j], 0, 0)
    return pl.pallas_call(
        _experts_kernel,
        grid_spec=pltpu.PrefetchScalarGridSpec(
            num_scalar_prefetch=3,
            grid=(NT_MAX,),
            in_specs=[
                pl.BlockSpec((EXPERT_TILE, XS_COLS), tile),
                pl.BlockSpec((None, 1, D_MODEL, D_EXPERT), expert),
                pl.BlockSpec((None, 1, D_MODEL, D_EXPERT), expert),
                pl.BlockSpec((None, 1, D_EXPERT, D_MODEL), expert),
            ],
            out_specs=pl.BlockSpec((EXPERT_TILE, D_MODEL), lambda j, *_: (j, 0)),
            scratch_shapes=[pltpu.VMEM((D_MODEL, D_EXPERT), BF16),
                            pltpu.VMEM((D_MODEL, D_EXPERT), BF16),
                            pltpu.VMEM((D_EXPERT, D_MODEL), BF16)],
        ),
        out_shape=jax.ShapeDtypeStruct((NR, D_MODEL), BF16),
        compiler_params=pltpu.CompilerParams(
            dimension_semantics=("arbitrary",), vmem_limit_bytes=VMEM_LIMIT),
        name="moe_experts",
    )(plan["tidx"], plan["texp"], plan["nval"], xs, wg, wu, wd)


def _combine_kernel(goff_ref, lens_ref, loff_ref, ngran_ref, h_ref, tok_ref, g_ref, ys_ref, o_ref, local, sems,
                    *, final):
    i = pl.program_id(0)
    last = pl.num_programs(0) - 1
    slot = lax.rem(i, 2)
    first_sub = pl.program_id(1) == 0

    def run_copy(s, src, dst):
        return pltpu.make_async_copy(ys_ref.at[pl.ds(_aligned(src), RUN_ALIGN), :],
                                     local.at[s, pl.ds(_aligned(dst), RUN_ALIGN), :], sems.at[s])

    def start_tile(step):
        s = lax.rem(step, 2)
        _for_each_run(step, lens_ref,
                      lambda k, q: run_copy(s, goff_ref[k] + q, loff_ref[k] + q).start())

    @pl.when(first_sub & (i == 0))
    def _():
        local[...] = jnp.zeros_like(local)
        start_tile(0)

    @pl.when(first_sub & (i < last))
    def _():
        start_tile(i + 1)

    @pl.when(first_sub)
    def _():
        def wait_body(k, c):
            run_copy(slot, 0, 0).wait()
            return c

        lax.fori_loop(0, ngran_ref[i], wait_body, 0)

    tok = tok_ref[...]
    p1 = tok[:, TOK_POS1:TOK_POS1 + 1].astype(jnp.int32)
    p2 = tok[:, TOK_POS2:TOK_POS2 + 1].astype(jnp.int32)
    ciota = lax.broadcasted_iota(jnp.int32, (tok.shape[0], LOCAL_ROWS), 1)
    sel = jnp.where((ciota == p1) | (ciota == p2), 1.0, 0.0).astype(BF16)
    out = h_ref[...] + _dot(sel, local[slot])
    o_ref[...] = _rms(out, g_ref[...]) if final else out


def _final_block(i, j):
    c = i * (MOE_TM // CHUNK) + j
    return (c // NCHUNK) * (SEQ // CHUNK) + jnp.maximum(c % NCHUNK - 1, 0)


def _combine(plan, h, tok, ys, final_gain=None):
    final = final_gain is not None
    rows = CHUNK if final else MOE_TM
    nsub = MOE_TM // rows
    row = lambda i, j, *_: (i * nsub + j, 0)
    out_row = (lambda i, j, *_: (_final_block(i, j), 0)) if final else row
    gain = final_gain if final else jnp.ones((1, D_MODEL), F32)
    return pl.pallas_call(
        functools.partial(_combine_kernel, final=final),
        grid_spec=pltpu.PrefetchScalarGridSpec(
            num_scalar_prefetch=4,
            grid=(N_MOE_TILES, nsub),
            in_specs=[
                pl.BlockSpec((rows, D_MODEL), row),
                pl.BlockSpec((rows, LANES), row),
                pl.BlockSpec((1, D_MODEL), lambda i, j, *_: (0, 0)),
                pl.BlockSpec(memory_space=pl.ANY),
            ],
            out_specs=pl.BlockSpec((rows, D_MODEL), out_row),
            scratch_shapes=[pltpu.VMEM((2, LOCAL_ROWS, D_MODEL), BF16),
                            pltpu.SemaphoreType.DMA((2,))],
        ),
        out_shape=jax.ShapeDtypeStruct((BATCH * SEQ if final else TP, D_MODEL), F32),
        compiler_params=pltpu.CompilerParams(
            dimension_semantics=("arbitrary", "arbitrary"), vmem_limit_bytes=VMEM_LIMIT),
        name="moe_combine_final" if final else "moe_combine",
    )(plan["goff"], plan["lens"], plan["loff"], plan["ngran"], h, tok, gain, ys)


def _row(v):
    return v.reshape(1, -1).astype(F32)


def _pad_lanes(v):
    return jnp.pad(_row(v), ((0, 0), (0, LANES - v.shape[-1])))


def _pair_blockdiag(w):
    w = w.reshape(LRU_WIDTH // LANES, 2, 64, 64)
    z = jnp.zeros_like(w[:, 0])
    top = jnp.concatenate([w[:, 0], z], axis=2)
    bot = jnp.concatenate([z, w[:, 1]], axis=2)
    return jnp.concatenate([top, bot], axis=1).astype(BF16)


def _moe_layer(h, layer, final_gain, valid, ffn_norm, rgw, rgb, rew, reb, wg, wu, wd):
    wr = jnp.zeros((D_MODEL, LANES), F32)
    wr = wr.at[:, 0:MOE_GROUPS].set(rgw[layer])
    wr = wr.at[:, ROUTER_EXPERT_ROW:ROUTER_EXPERT_ROW + MOE_EXPERTS].set(rew[layer])
    br = jnp.zeros((1, LANES), F32)
    br = br.at[0, 0:MOE_GROUPS].set(rgb[layer])
    br = br.at[0, ROUTER_EXPERT_ROW:ROUTER_EXPERT_ROW + MOE_EXPERTS].set(reb[layer])
    gain = _row(ffn_norm[layer])
    lpos, tok, runlen, runoff = _router(h, valid, gain, wr, br)
    plan = _moe_plan(runlen, runoff)
    xs = _dispatch(plan, h, gain, lpos, tok)
    ys = _experts(plan, xs, layer, wg, wu, wd)
    return _combine(plan, h, tok, ys, final_gain)


def kernel(x, meta_tokens, norm_final, mix_norm_even, w_in, ssd_conv_w, ssd_conv_b, ssd_dt_bias, ssd_a_log, ssd_d, ssd_norm, lru_conv_w, lru_conv_b, lru_w_a, lru_b_a, lru_w_x, lru_b_x, lru_lambda, w_out, mix_norm_odd, pool_w, pool_b, pool_scale, ffn_norm, router_group_w, router_group_b, router_expert_w, router_expert_b, expert_w_gate, expert_w_up, expert_w_down):
    meta = jnp.broadcast_to(meta_tokens.astype(F32)[None], (BATCH, N_META, D_MODEL))
    h = jnp.concatenate([jnp.zeros((BATCH, PAD, D_MODEL), F32), meta, x.astype(F32)], axis=1)
    h = h.reshape(TP, D_MODEL)
    valid = ((jnp.arange(TP, dtype=jnp.int32) % LP) >= PAD).astype(jnp.int32).reshape(1, TP)
    moe_args = (valid, ffn_norm, router_group_w, router_group_b, router_expert_w, router_expert_b,
                expert_w_gate, expert_w_up, expert_w_down)

    wi = w_in[0]
    w_proj = jnp.concatenate(
        [wi[:, 0:1024], wi[:, 1024:2560], wi[:, 2576:3600], wi[:, 3600:4624], wi[:, 2560:2576],
         jnp.zeros((D_MODEL, LANES - SSD_HEADS), F32)], axis=1).astype(BF16)
    z, xbc, gate, lin, dt = _in_proj(h, _row(mix_norm_even[0]), w_proj)
    expand = (jnp.arange(LANES)[:, None] == (jnp.arange(SSD_WIDTH) // SSD_HEAD_DIM)[None, :]).astype(F32)
    params = [
        ssd_conv_w[0].astype(F32), _row(ssd_conv_b[0]), _pad_lanes(ssd_dt_bias[0]), _pad_lanes(ssd_a_log[0]),
        _row(jnp.repeat(ssd_d[0], SSD_HEAD_DIM)), _row(ssd_norm[0]),
        lru_conv_w[0].astype(F32), _row(lru_conv_b[0]), _pair_blockdiag(lru_w_a[0]), _row(lru_b_a[0]),
        _pair_blockdiag(lru_w_x[0]), _row(lru_b_x[0]), _row(lru_lambda[0]),
        w_out[0].astype(BF16), expand,
    ]
    r3 = lambda a: a.reshape(BATCH, LP, a.shape[-1])
    h = _mixer(r3(h), r3(z), r3(xbc), r3(gate), r3(lin), r3(dt), params).reshape(TP, D_MODEL)
    h = _moe_layer(h, 0, None, *moe_args)

    h = _pool_mixer(h, _row(mix_norm_odd[0]), pool_w[0].astype(BF16), _row(pool_b[0]), _row(pool_scale[0]))
    out = _moe_layer(h, 1, _row(norm_final), *moe_args)
    return out.reshape(BATCH, SEQ, D_MODEL)
```

```python
import functools
import math

import jax
import jax.numpy as jnp
from jax import lax
from jax.experimental import pallas as pl
from jax.experimental.pallas import tpu as pltpu

F32 = jnp.float32
BF16 = jnp.bfloat16
HIGHEST = lax.Precision.HIGHEST

D_MODEL = 1024
BATCH = 8
SEQ = 2048
N_META = 16
RMS_EPS = 1e-6
CONV_WIDTH = 4
CHUNK = 128
PAD = CHUNK - N_META
LP = PAD + N_META + SEQ
NCHUNK = LP // CHUNK
TP = BATCH * LP

SSD_HEADS = 16
SSD_HEAD_DIM = 64
SSD_WIDTH = 1024
SSD_GROUPS = 2
SSD_STATE = 128
SSD_GROUP_WIDTH = SSD_WIDTH // SSD_GROUPS
SSD_CONV_DIM = SSD_WIDTH + 2 * SSD_GROUPS * SSD_STATE
LRU_WIDTH = 1024
LRU_C = 8.0
LANES = 128
POOL_WINDOWS = (2, 4, 8, 16)
POOL_GROUP_DIM = 256
POOL_MAX_WINDOW = 16
MOE_GROUPS = 4
MOE_PER_GROUP = 4
MOE_EXPERTS = 16
D_EXPERT = 512
ROUTER_EXPERT_ROW = 8

VMEM_LIMIT = 56 * 1024 * 1024


def _dot(a, b, precision=None):
    return jnp.dot(a, b, preferred_element_type=F32, precision=precision)


def _rms(x, gain):
    ms = jnp.mean(x * x, axis=-1, keepdims=True)
    return x * lax.rsqrt(ms + RMS_EPS) * gain


def _silu(x):
    return x * jax.nn.sigmoid(x)


def _softplus(x):
    return jnp.maximum(x, 0.0) + jnp.log1p(jnp.exp(-jnp.abs(x)))


def _row_ids(shape, chunk):
    return lax.broadcasted_iota(jnp.int32, shape, 0) + chunk * CHUNK


MIX_B = 2
_Z0, _X0, _G0, _L0, _T0, _PEND = 0, 1024, 2560, 3584, 4608, 4736
HEAD_PIECES = 3


def _causal_conv(hist, x, w_ref, b_ref):
    prev = hist[...]
    hist[...] = x[CHUNK - 8:, :]
    r8 = lax.broadcasted_iota(jnp.int32, prev.shape, 0)
    acc = b_ref[...] + w_ref[CONV_WIDTH - 1:CONV_WIDTH, :] * x
    for j in range(1, CONV_WIDTH):
        shifted = pltpu.roll(x, j, 0)
        head = jnp.where(r8 < j, pltpu.roll(prev, j, 0), shifted[0:8, :])
        shifted = jnp.concatenate([head, shifted[8:, :]], axis=0)
        acc = acc + w_ref[CONV_WIDTH - 1 - j:CONV_WIDTH - j, :] * shifted
    return acc


def _expand_heads(v, exp_ref):
    lane = lax.broadcasted_iota(jnp.int32, v.shape, 1)
    rest = jnp.where(lane < SSD_HEADS, v, 0.0)
    packed = None
    for k in range(HEAD_PIECES):
        piece = rest.astype(BF16).astype(F32)
        rest = rest - piece
        moved = piece if k == 0 else pltpu.roll(piece, k * SSD_HEADS, 1)
        packed = moved if packed is None else packed + moved
    return _dot(packed.astype(BF16), exp_ref[...])


def _ssd_chunk(c, z, xbc, dt_raw, hist, st, cw_ref, cb_ref, dtb_ref, alog_ref, dsk_ref, ng_ref, exp_ref):
    xc = _silu(_causal_conv(hist, xbc, cw_ref, cb_ref))
    xs = xc[:, 0:SSD_WIDTH]
    valid_s = _row_ids((CHUNK, 2 * SSD_STATE), c) >= PAD
    bm = jnp.where(valid_s, xc[:, SSD_WIDTH:SSD_WIDTH + 2 * SSD_STATE], 0.0)
    cm = jnp.where(valid_s, xc[:, SSD_WIDTH + 2 * SSD_STATE:], 0.0)

    li = lax.broadcasted_iota(jnp.int32, (CHUNK, CHUNK), 0)
    si = lax.broadcasted_iota(jnp.int32, (CHUNK, CHUNK), 1)
    causal = si <= li
    dt = _softplus(dt_raw + dtb_ref[...])
    dt = jnp.where(li + c * CHUNK >= PAD, dt, 0.0)
    adt = dt * (-jnp.exp(alog_ref[...]))
    a_cs = _dot(causal.astype(F32), adt, HIGHEST)
    a_last = a_cs[CHUNK - 1:CHUNK, :]
    ea = jnp.exp(a_cs)
    dt_x = _expand_heads(dt, exp_ref)
    w_x = _expand_heads(jnp.exp(a_last - a_cs) * dt, exp_ref)
    ea_x = _expand_heads(ea, exp_ref)
    xdt = xs * dt_x
    a_cs_t = a_cs.T
    lane = lax.broadcasted_iota(jnp.int32, (CHUNK, LANES), 1)

    ys = []
    for g in range(SSD_GROUPS):
        gsl = slice(g * SSD_GROUP_WIDTH, (g + 1) * SSD_GROUP_WIDTH)
        bg = bm[:, g * SSD_STATE:(g + 1) * SSD_STATE]
        cg16 = cm[:, g * SSD_STATE:(g + 1) * SSD_STATE].astype(BF16)
        cbm = lax.dot_general(cg16, bg.astype(BF16), (((1,), (1,)), ((), ())),
                              preferred_element_type=F32)
        s_in = st[:, gsl]
        y_off = _dot(cg16, s_in.astype(BF16)) * ea_x[:, gsl]
        s_new = _dot(bg.T.astype(BF16), (w_x[:, gsl] * xs[:, gsl]).astype(BF16))
        st[:, gsl] = ea_x[CHUNK - 1:CHUNK, gsl] * s_in + s_new
        for j in range(SSD_GROUP_WIDTH // LANES):
            xp = xdt[:, g * SSD_GROUP_WIDTH + j * LANES:
                     g * SSD_GROUP_WIDTH + (j + 1) * LANES].astype(BF16)
            parts = []
            for hh in range(LANES // SSD_HEAD_DIM):
                hd = g * (SSD_HEADS // SSD_GROUPS) + j * (LANES // SSD_HEAD_DIM) + hh
                seg = a_cs[:, hd:hd + 1] - a_cs_t[hd:hd + 1, :]
                dec = jnp.exp(jnp.where(causal, seg, -1e30))
                parts.append(_dot((cbm * dec).astype(BF16), xp))
            ys.append(jnp.where(lane < SSD_HEAD_DIM, parts[0], parts[1])
                      + y_off[:, j * LANES:(j + 1) * LANES])
    y = jnp.concatenate(ys, axis=1) + dsk_ref[...] * xs
    y = y * _silu(z)
    normed = []
    for g in range(SSD_GROUPS):
        yg = y[:, g * SSD_GROUP_WIDTH:(g + 1) * SSD_GROUP_WIDTH]
        normed.append(yg * lax.rsqrt(jnp.mean(yg * yg, axis=-1, keepdims=True) + RMS_EPS))
    return jnp.concatenate(normed, axis=1) * ng_ref[...]


def _lru_chunk(c, gt, lin, hist, lc, lcw_ref, lcb_ref, wa_ref, ba_ref, wx_ref, bx_ref, lam_ref):
    xb = _causal_conv(hist, lin, lcw_ref, lcb_ref)
    xb16 = xb.astype(BF16)
    nblk = LRU_WIDTH // LANES
    ra = jnp.concatenate(
        [_dot(xb16[:, j * LANES:(j + 1) * LANES], wa_ref[j]) for j in range(nblk)], axis=1)
    ix = jnp.concatenate(
        [_dot(xb16[:, j * LANES:(j + 1) * LANES], wx_ref[j]) for j in range(nblk)], axis=1)
    r = jax.nn.sigmoid(ra + ba_ref[...])
    ig = jax.nn.sigmoid(ix + bx_ref[...])
    log_a = (-LRU_C * _softplus(-lam_ref[...])) * r
    a = jnp.exp(log_a)
    mult = jnp.sqrt(jnp.tanh(-log_a) * (a * a + 1.0))
    grow = _row_ids((CHUNK, LRU_WIDTH), c)
    mult = jnp.where(grow == PAD, 1.0, mult)
    u = jnp.where(grow >= PAD, mult * (ig * xb), 0.0)

    r8 = lax.broadcasted_iota(jnp.int32, (8, LRU_WIDTH), 0)
    carry = lc[0:1, :]
    hs = []
    for j in range(CHUNK // 8):
        a8 = a[j * 8:(j + 1) * 8, :]
        u8 = u[j * 8:(j + 1) * 8, :]
        for d in (1, 2, 4):
            a_sh = jnp.where(r8 >= d, pltpu.roll(a8, d, 0), 1.0)
            u_sh = jnp.where(r8 >= d, pltpu.roll(u8, d, 0), 0.0)
            u8 = a8 * u_sh + u8
            a8 = a8 * a_sh
        h8 = a8 * carry + u8
        carry = h8[7:8, :]
        hs.append(h8)
    lc[0:1, :] = carry
    gelu = 0.5 * gt * (1.0 + jnp.tanh(math.sqrt(2.0 / math.pi) * (gt + 0.044715 * (gt * gt * gt))))
    return jnp.concatenate(hs, axis=0) * gelu


def _mixer_kernel(x_ref, meta_ref, gin_ref, wproj_ref,
                  cw_ref, cb_ref, dtb_ref, alog_ref, dsk_ref, ng_ref,
                  lcw_ref, lcb_ref, wa_ref, ba_ref, wx_ref, bx_ref, lam_ref,
                  wout_ref, exp_ref, o_ref, hbuf, hist_x, hist_l, st, lc):
    c = pl.program_id(1)

    @pl.when(c == 0)
    def _():
        for b in range(MIX_B):
            hbuf[b * CHUNK:(b + 1) * CHUNK, :] = meta_ref[...]
        hist_x[...] = jnp.zeros_like(hist_x)
        hist_l[...] = jnp.zeros_like(hist_l)
        st[...] = jnp.zeros_like(st)
        lc[...] = jnp.zeros_like(lc)

    @pl.when(c > 0)
    def _():
        for b in range(MIX_B):
            hbuf[b * CHUNK:(b + 1) * CHUNK, :] = x_ref[b]

    h = hbuf[...]
    hn = _rms(h, gin_ref[...]).astype(BF16)
    z = _dot(hn, wproj_ref[:, _Z0:_X0])
    xbc = _dot(hn, wproj_ref[:, _X0:_G0])
    gate = _dot(hn, wproj_ref[:, _G0:_L0])
    lin = _dot(hn, wproj_ref[:, _L0:_T0])
    dt_raw = _dot(hn, wproj_ref[:, _T0:_PEND])

    mixed = []
    for b in range(MIX_B):
        rows = slice(b * CHUNK, (b + 1) * CHUNK)
        y_ssd = _ssd_chunk(c, z[rows], xbc[rows], dt_raw[rows], hist_x.at[b], st.at[b],
                           cw_ref, cb_ref, dtb_ref, alog_ref, dsk_ref, ng_ref, exp_ref)
        y_lru = _lru_chunk(c, gate[rows], lin[rows], hist_l.at[b], lc.at[b],
                           lcw_ref, lcb_ref, wa_ref, ba_ref, wx_ref, bx_ref, lam_ref)
        mixed.append(jnp.concatenate([y_ssd, y_lru], axis=1).astype(BF16))
    out = _dot(jnp.concatenate(mixed, axis=0), wout_ref[...]) + h
    keep = _row_ids((CHUNK, D_MODEL), c) >= PAD
    for b in range(MIX_B):
        o_ref[b] = jnp.where(keep, out[b * CHUNK:(b + 1) * CHUNK, :], 0.0)


def _mixer(x, meta_chunk, params):
    full = lambda a: pl.BlockSpec(a.shape, lambda b, c: (0,) * a.ndim, pipeline_mode=pl.Buffered(1))
    return pl.pallas_call(
        _mixer_kernel,
        grid=(BATCH // MIX_B, NCHUNK),
        in_specs=[pl.BlockSpec((MIX_B, CHUNK, D_MODEL), lambda b, c: (b, jnp.maximum(c - 1, 0), 0)),
                  full(meta_chunk)] + [full(p) for p in params],
        out_specs=pl.BlockSpec((MIX_B, CHUNK, D_MODEL), lambda b, c: (b, c, 0)),
        out_shape=jax.ShapeDtypeStruct((BATCH, LP, D_MODEL), F32),
        scratch_shapes=[
            pltpu.VMEM((MIX_B * CHUNK, D_MODEL), F32),
            pltpu.VMEM((MIX_B, 8, SSD_CONV_DIM), F32),
            pltpu.VMEM((MIX_B, 8, LRU_WIDTH), F32),
            pltpu.VMEM((MIX_B, SSD_STATE, SSD_WIDTH), F32),
            pltpu.VMEM((MIX_B, 8, LRU_WIDTH), F32),
        ],
        compiler_params=pltpu.CompilerParams(
            dimension_semantics=("arbitrary", "arbitrary"), vmem_limit_bytes=VMEM_LIMIT),
        name="ssd_lru_mixer",
    )(x, meta_chunk, *params)


POOL_TM = 512


def _pool_kernel(h_ref, g_ref, pw_ref, pb_ref, ps_ref, o_ref, buf):
    base = POOL_MAX_WINDOW
    for k in range(POOL_TM // CHUNK):
        chunk = pl.program_id(0) * (POOL_TM // CHUNK) + k
        c = lax.rem(chunk, NCHUNK)

        @pl.when(c == 0)
        def _():
            buf[0:base, :] = jnp.zeros((base, D_MODEL), F32)

        h = h_ref[k * CHUNK:(k + 1) * CHUNK, :]
        hn = _rms(h, g_ref[...])
        buf[base:base + CHUNK, :] = hn
        pos = _row_ids((CHUNK, POOL_GROUP_DIM), c) - PAD
        outs = []
        for g, w in enumerate(POOL_WINDOWS):
            sl = slice(g * POOL_GROUP_DIM, (g + 1) * POOL_GROUP_DIM)
            ws = buf[:, sl]
            s = 1
            while s < w:
                ws = ws + pltpu.roll(ws, s, 0)
                s *= 2
            ws = ws[base:, :]
            count = jnp.clip(pos + 1, 1, w).astype(F32)
            pooled = ws / count - hn[:, sl]
            outs.append(_dot(pooled.astype(BF16), pw_ref[g]))
        buf[0:base, :] = buf[CHUNK:CHUNK + base, :]
        y = (jnp.concatenate(outs, axis=1) + pb_ref[...]) * ps_ref[...]
        o_ref[k * CHUNK:(k + 1) * CHUNK, :] = jnp.where(_row_ids((CHUNK, D_MODEL), c) >= PAD, h + y, 0.0)


def _pool_mixer(h, gain, pw, pb, ps):
    blk = pl.BlockSpec((POOL_TM, D_MODEL), lambda i: (i, 0))
    full = lambda a: pl.BlockSpec(a.shape, lambda i: (0,) * a.ndim)
    return pl.pallas_call(
        _pool_kernel,
        grid=(TP // POOL_TM,),
        in_specs=[blk, full(gain), full(pw), full(pb), full(ps)],
        out_specs=blk,
        out_shape=jax.ShapeDtypeStruct((TP, D_MODEL), F32),
        scratch_shapes=[pltpu.VMEM((CHUNK + POOL_MAX_WINDOW, D_MODEL), F32)],
        compiler_params=pltpu.CompilerParams(
            dimension_semantics=("arbitrary",), vmem_limit_bytes=VMEM_LIMIT),
        name="pool_mixer",
    )(h, gain, pw, pb, ps)


MOE_TM = 512
N_MOE_TILES = TP // MOE_TM
RUN_ALIGN = 16
RUN_SHIFT = 4
LOCAL_ROWS = 1280
EXPERT_TILE = 512
N_ROUTED = BATCH * (N_META + SEQ)
MAX_ROWS = (2 * N_ROUTED + N_MOE_TILES * MOE_EXPERTS * (RUN_ALIGN - 1)
            + MOE_EXPERTS * (EXPERT_TILE - 1))
NT_MAX = -(-MAX_ROWS // EXPERT_TILE)
NR = NT_MAX * EXPERT_TILE
XS_COLS = D_MODEL + LANES
INT_ROWS = 8
TOK_GATE1, TOK_GATE2, TOK_POS1, TOK_POS2 = 0, 3, 6, 7
assert LOCAL_ROWS >= 2 * MOE_TM + MOE_EXPERTS * (RUN_ALIGN - 1) and LOCAL_ROWS % LANES == 0


def _first_argmax(vals):
    best, idx = vals[0], jnp.zeros(vals[0].shape, jnp.int32)
    for k in range(1, len(vals)):
        better = vals[k] > best
        idx = jnp.where(better, k, idx)
        best = jnp.where(better, vals[k], best)
    return idx, best


def _softmax_rows(vals):
    m = functools.reduce(jnp.maximum, vals)
    ex = [jnp.exp(v - m) for v in vals]
    tot = functools.reduce(lambda p, q: p + q, ex)
    return [e / tot for e in ex]


def _bf16_pieces(x):
    hi = x.astype(BF16).astype(F32)
    rest = x - hi
    mid = rest.astype(BF16).astype(F32)
    lo = (rest - mid).astype(BF16).astype(F32)
    return [hi, mid, lo]


def _router_kernel(h_ref, valid_ref, g_ref, wrh_ref, wrl_ref, br_ref, lpos_ref, tok_ref, runlen_ref, runoff_ref):
    hn = _rms(h_ref[...], g_ref[...])
    hn_hi = hn.astype(BF16)
    hn_lo = (hn - hn_hi.astype(F32)).astype(BF16)
    logits = (_dot(hn_hi, wrh_ref[...]) + (_dot(hn_hi, wrl_ref[...]) + _dot(hn_lo, wrh_ref[...]))
              + br_ref[...])
    lt = logits.T
    p_group = _softmax_rows([lt[k:k + 1, :] for k in range(MOE_GROUPS)])
    g_sel, p_g = _first_argmax(p_group)
    fine = []
    for k in range(MOE_PER_GROUP):
        f = lt[ROUTER_EXPERT_ROW + k:ROUTER_EXPERT_ROW + k + 1, :]
        for g in range(1, MOE_GROUPS):
            r0 = ROUTER_EXPERT_ROW + g * MOE_PER_GROUP + k
            f = jnp.where(g_sel == g, lt[r0:r0 + 1, :], f)
        fine.append(f)
    q = _softmax_rows(fine)
    i1, t1 = _first_argmax(q)
    i2, t2 = _first_argmax([jnp.where(i1 == k, -1.0, q[k]) for k in range(MOE_PER_GROUP)])
    tot = t1 + t2
    gate1 = p_g * (t1 / tot)
    gate2 = p_g * (t2 / tot)

    valid = valid_ref[...] > 0
    e1 = jnp.where(valid, g_sel * MOE_PER_GROUP + i1, -1)
    e2 = jnp.where(valid, g_sel * MOE_PER_GROUP + i2, -1)
    erow = lax.broadcasted_iota(jnp.int32, (MOE_EXPERTS, MOE_TM), 0)
    hit1 = erow == e1
    hit2 = erow == e2
    onehot = jnp.where(hit1 | hit2, 1.0, 0.0)
    si = lax.broadcasted_iota(jnp.int32, (MOE_TM, MOE_TM), 0)
    ti = lax.broadcasted_iota(jnp.int32, (MOE_TM, MOE_TM), 1)
    before = jnp.where(si < ti, 1.0, 0.0).astype(BF16)
    seen = _dot(onehot.astype(BF16), before)
    count = jnp.sum(onehot, axis=1, keepdims=True).astype(jnp.int32)
    runlen = lax.shift_left(lax.shift_right_logical(count + (RUN_ALIGN - 1), RUN_SHIFT), RUN_SHIFT)
    runlen_b = jnp.broadcast_to(runlen, (MOE_EXPERTS, LANES))
    ei = lax.broadcasted_iota(jnp.int32, (MOE_EXPERTS, MOE_EXPERTS), 0)
    ej = lax.broadcasted_iota(jnp.int32, (MOE_EXPERTS, MOE_EXPERTS), 1)
    runoff_b = _dot(jnp.where(ej < ei, 1.0, 0.0), runlen_b.astype(F32), HIGHEST)
    runlen_ref[...] = runlen_b
    runoff_ref[...] = runoff_b.astype(jnp.int32)
    place = seen + runoff_b[:, 0:1]
    pos1 = jnp.where(valid, jnp.sum(jnp.where(hit1, place, 0.0), axis=0, keepdims=True), -1.0)
    pos2 = jnp.where(valid, jnp.sum(jnp.where(hit2, place, 0.0), axis=0, keepdims=True), -1.0)
    r8 = lax.broadcasted_iota(jnp.int32, (INT_ROWS, MOE_TM), 0)
    lpos_ref[...] = jnp.where(r8 == 0, pos1.astype(jnp.int32),
                              jnp.where(r8 == 1, pos2.astype(jnp.int32), 0))

    rows = lax.broadcasted_iota(jnp.int32, lt.shape, 0)
    table = jnp.zeros(lt.shape, F32)
    for k, piece in enumerate(_bf16_pieces(gate1)):
        table = jnp.where(rows == TOK_GATE1 + k, piece, table)
    for k, piece in enumerate(_bf16_pieces(gate2)):
        table = jnp.where(rows == TOK_GATE2 + k, piece, table)
    table = jnp.where(rows == TOK_POS1, pos1, jnp.where(rows == TOK_POS2, pos2, table))
    tok_ref[...] = table.T


def _router(h, valid, gain, wr, br):
    row = lambda i: (i, 0)
    col = lambda i: (0, i)
    const = lambda i: (0, 0)
    wr_hi = wr.astype(BF16)
    wr_lo = (wr - wr_hi.astype(F32)).astype(BF16)
    return pl.pallas_call(
        _router_kernel,
        grid=(N_MOE_TILES,),
        in_specs=[
            pl.BlockSpec((MOE_TM, D_MODEL), row),
            pl.BlockSpec((1, MOE_TM), col),
            pl.BlockSpec((1, D_MODEL), const),
            pl.BlockSpec((D_MODEL, LANES), const),
            pl.BlockSpec((D_MODEL, LANES), const),
            pl.BlockSpec((1, LANES), const),
        ],
        out_specs=[pl.BlockSpec((INT_ROWS, MOE_TM), col),
                   pl.BlockSpec((MOE_TM, LANES), row),
                   pl.BlockSpec((MOE_EXPERTS, LANES), row),
                   pl.BlockSpec((MOE_EXPERTS, LANES), row)],
        out_shape=[jax.ShapeDtypeStruct((INT_ROWS, TP), jnp.int32),
                   jax.ShapeDtypeStruct((TP, LANES), F32),
                   jax.ShapeDtypeStruct((N_MOE_TILES * MOE_EXPERTS, LANES), jnp.int32),
                   jax.ShapeDtypeStruct((N_MOE_TILES * MOE_EXPERTS, LANES), jnp.int32)],
        compiler_params=pltpu.CompilerParams(
            dimension_semantics=("arbitrary",), vmem_limit_bytes=VMEM_LIMIT),
        name="moe_router",
    )(h, valid, gain, wr_hi, wr_lo, br)


def _moe_plan(runlen, runoff):
    i32 = jnp.int32
    lens = runlen[:, 0].reshape(N_MOE_TILES, MOE_EXPERTS)
    total = jnp.sum(lens, axis=0)
    padded = (total + (EXPERT_TILE - 1)) // EXPERT_TILE * EXPERT_TILE
    ends = jnp.cumsum(padded)
    base = ends - padded
    goff = base[None, :] + jnp.cumsum(lens, axis=0) - lens
    nval = ends[-1] // EXPERT_TILE
    tidx = jnp.minimum(jnp.arange(NT_MAX, dtype=i32), nval - 1)
    texp = jnp.minimum(jnp.sum(tidx[:, None] >= (ends // EXPERT_TILE)[None, :], axis=1), MOE_EXPERTS - 1)
    return dict(
        goff=goff.reshape(-1).astype(i32), lens=lens.reshape(-1).astype(i32),
        loff=runoff[:, 0].astype(i32), gap_start=(base + total).astype(i32),
        gap_len=(padded - total).astype(i32), nval=nval.reshape(1).astype(i32),
        tidx=tidx.astype(i32), texp=texp.astype(i32),
        ngran=(jnp.sum(lens, axis=1) // RUN_ALIGN).astype(i32))


def _aligned(x):
    return pl.multiple_of(x, RUN_ALIGN)


def _for_each_run(step, lens_ref, fn):
    def expert_body(e, carry):
        k = step * MOE_EXPERTS + e

        def granule_body(q, c):
            fn(k, q * RUN_ALIGN)
            return c

        lax.fori_loop(0, lax.shift_right_logical(lens_ref[k], RUN_SHIFT), granule_body, 0)
        return carry

    lax.fori_loop(0, MOE_EXPERTS, expert_body, 0)


def _dispatch_kernel(goff_ref, lens_ref, loff_ref, gaps_ref, gapl_ref, nval_ref, ngran_ref,
                     h_ref, g_ref, lpos_ref, tok_ref, xs_ref, local, zeros, sems):
    i = pl.program_id(0)
    last = pl.num_programs(0) - 1
    slot = lax.rem(i, 2)

    def run_copy(s, src, dst):
        return pltpu.make_async_copy(local.at[s, pl.ds(_aligned(src), RUN_ALIGN), :],
                                     xs_ref.at[pl.ds(_aligned(dst), RUN_ALIGN), :], sems.at[s])

    def wait_tile(step):
        s = lax.rem(step, 2)

        def body(k, c):
            run_copy(s, 0, 0).wait()
            return c

        lax.fori_loop(0, ngran_ref[step], body, 0)

    @pl.when(i >= 2)
    def _():
        wait_tile(i - 2)

    hn = _rms(h_ref[...], g_ref[...]).astype(BF16)
    riota = lax.broadcasted_iota(jnp.int32, (LOCAL_ROWS, MOE_TM), 0)
    c1 = riota == lpos_ref[0:1, :]
    c2 = riota == lpos_ref[1:2, :]
    local[slot, :, 0:D_MODEL] = _dot(jnp.where(c1 | c2, 1.0, 0.0).astype(BF16), hn).astype(BF16)
    tok = tok_ref[...]
    lane = lax.broadcasted_iota(jnp.int32, (MOE_TM, LANES), 1)
    t1 = jnp.where(lane < TOK_GATE2, tok, 0.0).astype(BF16)
    t2 = jnp.where((lane >= TOK_GATE2) & (lane < TOK_POS1), tok, 0.0).astype(BF16)
    gate_cols = (_dot(jnp.where(c1, 1.0, 0.0).astype(BF16), t1)
                 + _dot(jnp.where(c2, 1.0, 0.0).astype(BF16), t2))
    local[slot, :, D_MODEL:XS_COLS] = gate_cols.astype(BF16)

    _for_each_run(i, lens_ref,
                  lambda k, q: run_copy(slot, loff_ref[k] + q, goff_ref[k] + q).start())

    @pl.when(i == last)
    def _():
        zeros[...] = jnp.zeros_like(zeros)
        zsem = sems.at[2]

        def gap_copy(dst):
            return pltpu.make_async_copy(zeros.at[pl.ds(0, RUN_ALIGN), :],
                                         xs_ref.at[pl.ds(_aligned(dst), RUN_ALIGN), :], zsem)

        def tile_copy(t):
            return pltpu.make_async_copy(
                zeros, xs_ref.at[pl.ds(pl.multiple_of(t * EXPERT_TILE, EXPERT_TILE), EXPERT_TILE), :], zsem)

        def expert_body(e, n):
            g = lax.shift_right_logical(gapl_ref[e], RUN_SHIFT)

            def body(q, c):
                gap_copy(gaps_ref[e] + q * RUN_ALIGN).start()
                return c

            lax.fori_loop(0, g, body, 0)
            return n + g

        ngap = lax.fori_loop(0, MOE_EXPERTS, expert_body, 0)

        def tail_start(t, c):
            tile_copy(t).start()
            return c

        lax.fori_loop(nval_ref[0], NT_MAX, tail_start, 0)

        def gap_wait(k, c):
            gap_copy(0).wait()
            return c

        lax.fori_loop(0, ngap, gap_wait, 0)

        def tail_wait(t, c):
            tile_copy(t).wait()
            return c

        lax.fori_loop(nval_ref[0], NT_MAX, tail_wait, 0)

        @pl.when(i >= 1)
        def _():
            wait_tile(i - 1)

        wait_tile(i)


def _dispatch(plan, h, gain, lpos, tok):
    row = lambda i, *_: (i, 0)
    return pl.pallas_call(
        _dispatch_kernel,
        grid_spec=pltpu.PrefetchScalarGridSpec(
            num_scalar_prefetch=7,
            grid=(N_MOE_TILES,),
            in_specs=[
                pl.BlockSpec((MOE_TM, D_MODEL), row),
                pl.BlockSpec((1, D_MODEL), lambda i, *_: (0, 0)),
                pl.BlockSpec((INT_ROWS, MOE_TM), lambda i, *_: (0, i)),
                pl.BlockSpec((MOE_TM, LANES), row),
            ],
            out_specs=pl.BlockSpec(memory_space=pl.ANY),
            scratch_shapes=[pltpu.VMEM((2, LOCAL_ROWS, XS_COLS), BF16),
                            pltpu.VMEM((EXPERT_TILE, XS_COLS), BF16),
                            pltpu.SemaphoreType.DMA((3,))],
        ),
        out_shape=jax.ShapeDtypeStruct((NR, XS_COLS), BF16),
        compiler_params=pltpu.CompilerParams(
            dimension_semantics=("arbitrary",), vmem_limit_bytes=VMEM_LIMIT),
        name="moe_dispatch",
    )(plan["goff"], plan["lens"], plan["loff"], plan["gap_start"], plan["gap_len"], plan["nval"],
      plan["ngran"], h, gain, lpos, tok)


def _experts_kernel(tidx_ref, texp_ref, nval_ref, xs_ref, wg_ref, wu_ref, wd_ref, ys_ref, wg16, wu16, wd16):
    j = pl.program_id(0)
    active = j < nval_ref[0]
    new_expert = (j == 0) | (texp_ref[j] != texp_ref[jnp.maximum(j - 1, 0)])

    @pl.when(active & new_expert)
    def _():
        wg16[...] = wg_ref[0].astype(BF16)
        wu16[...] = wu_ref[0].astype(BF16)
        wd16[...] = wd_ref[0].astype(BF16)

    @pl.when(active)
    def _():
        x = xs_ref[:, 0:D_MODEL]
        gate = jnp.sum(xs_ref[:, D_MODEL:XS_COLS].astype(F32), axis=1, keepdims=True)
        a = _dot(x, wg16[...])
        b = _dot(x, wu16[...])
        y = _dot((_silu(a) * b).astype(BF16), wd16[...])
        ys_ref[...] = (gate * y).astype(BF16)

    @pl.when(jnp.logical_not(active))
    def _():
        ys_ref[...] = jnp.zeros_like(ys_ref)


def _experts(plan, xs, layer, wg, wu, wd):
    tile = lambda j, tidx, texp, nval: (tidx[j], 0)
    expert = lambda j, tidx, texp, nval: (layer, texp[j], 0, 0)
    return pl.pallas_call(
        _experts_kernel,
        grid_spec=pltpu.PrefetchScalarGridSpec(
            num_scalar_prefetch=3,
            grid=(NT_MAX,),
            in_specs=[
                pl.BlockSpec((EXPERT_TILE, XS_COLS), tile),
                pl.BlockSpec((None, 1, D_MODEL, D_EXPERT), expert),
                pl.BlockSpec((None, 1, D_MODEL, D_EXPERT), expert),
                pl.BlockSpec((None, 1, D_EXPERT, D_MODEL), expert),
            ],
            out_specs=pl.BlockSpec((EXPERT_TILE, D_MODEL), lambda j, *_: (j, 0)),
            scratch_shapes=[pltpu.VMEM((D_MODEL, D_EXPERT), BF16),
                            pltpu.VMEM((D_MODEL, D_EXPERT), BF16),
                            pltpu.VMEM((D_EXPERT, D_MODEL), BF16)],
        ),
        out_shape=jax.ShapeDtypeStruct((NR, D_MODEL), BF16),
        compiler_params=pltpu.CompilerParams(
            dimension_semantics=("arbitrary",), vmem_limit_bytes=VMEM_LIMIT),
        name="moe_experts",
    )(plan["tidx"], plan["texp"], plan["nval"], xs, wg, wu, wd)


def _combine_kernel(goff_ref, lens_ref, loff_ref, ngran_ref, h_ref, tok_ref, g_ref, ys_ref, o_ref, local, sems,
                    *, final):
    i = pl.program_id(0)
    last = pl.num_programs(0) - 1
    slot = lax.rem(i, 2)
    first_sub = pl.program_id(1) == 0

    def run_copy(s, src, dst):
        return pltpu.make_async_copy(ys_ref.at[pl.ds(_aligned(src), RUN_ALIGN), :],
                                     local.at[s, pl.ds(_aligned(dst), RUN_ALIGN), :], sems.at[s])

    def start_tile(step):
        s = lax.rem(step, 2)
        _for_each_run(step, lens_ref,
                      lambda k, q: run_copy(s, goff_ref[k] + q, loff_ref[k] + q).start())

    @pl.when(first_sub & (i == 0))
    def _():
        local[...] = jnp.zeros_like(local)
        start_tile(0)

    @pl.when(first_sub & (i < last))
    def _():
        start_tile(i + 1)

    @pl.when(first_sub)
    def _():
        def wait_body(k, c):
            run_copy(slot, 0, 0).wait()
            return c

        lax.fori_loop(0, ngran_ref[i], wait_body, 0)

    tok = tok_ref[...]
    p1 = tok[:, TOK_POS1:TOK_POS1 + 1].astype(jnp.int32)
    p2 = tok[:, TOK_POS2:TOK_POS2 + 1].astype(jnp.int32)
    ciota = lax.broadcasted_iota(jnp.int32, (tok.shape[0], LOCAL_ROWS), 1)
    sel = jnp.where((ciota == p1) | (ciota == p2), 1.0, 0.0).astype(BF16)
    out = h_ref[...] + _dot(sel, local[slot])
    o_ref[...] = _rms(out, g_ref[...]) if final else out


def _final_block(i, j):
    c = i * (MOE_TM // CHUNK) + j
    return (c // NCHUNK) * (SEQ // CHUNK) + jnp.maximum(c % NCHUNK - 1, 0)


def _combine(plan, h, tok, ys, final_gain=None):
    final = final_gain is not None
    rows = CHUNK if final else MOE_TM
    nsub = MOE_TM // rows
    row = lambda i, j, *_: (i * nsub + j, 0)
    out_row = (lambda i, j, *_: (_final_block(i, j), 0)) if final else row
    gain = final_gain if final else jnp.ones((1, D_MODEL), F32)
    return pl.pallas_call(
        functools.partial(_combine_kernel, final=final),
        grid_spec=pltpu.PrefetchScalarGridSpec(
            num_scalar_prefetch=4,
            grid=(N_MOE_TILES, nsub),
            in_specs=[
                pl.BlockSpec((rows, D_MODEL), row),
                pl.BlockSpec((rows, LANES), row),
                pl.BlockSpec((1, D_MODEL), lambda i, j, *_: (0, 0)),
                pl.BlockSpec(memory_space=pl.ANY),
            ],
            out_specs=pl.BlockSpec((rows, D_MODEL), out_row),
            scratch_shapes=[pltpu.VMEM((2, LOCAL_ROWS, D_MODEL), BF16),
                            pltpu.SemaphoreType.DMA((2,))],
        ),
        out_shape=jax.ShapeDtypeStruct((BATCH * SEQ if final else TP, D_MODEL), F32),
        compiler_params=pltpu.CompilerParams(
            dimension_semantics=("arbitrary", "arbitrary"), vmem_limit_bytes=VMEM_LIMIT),
        name="moe_combine_final" if final else "moe_combine",
    )(plan["goff"], plan["lens"], plan["loff"], plan["ngran"], h, tok, gain, ys)


def _row(v):
    return v.reshape(1, -1).astype(F32)


def _pad_lanes(v):
    return jnp.pad(_row(v), ((0, 0), (0, LANES - v.shape[-1])))


def _pair_blockdiag(w):
    w = w.reshape(LRU_WIDTH // LANES, 2, 64, 64)
    z = jnp.zeros_like(w[:, 0])
    top = jnp.concatenate([w[:, 0], z], axis=2)
    bot = jnp.concatenate([z, w[:, 1]], axis=2)
    return jnp.concatenate([top, bot], axis=1).astype(BF16)


def _moe_layer(h, layer, final_gain, valid, ffn_norm, rgw, rgb, rew, reb, wg, wu, wd):
    wr = jnp.zeros((D_MODEL, LANES), F32)
    wr = wr.at[:, 0:MOE_GROUPS].set(rgw[layer])
    wr = wr.at[:, ROUTER_EXPERT_ROW:ROUTER_EXPERT_ROW + MOE_EXPERTS].set(rew[layer])
    br = jnp.zeros((1, LANES), F32)
    br = br.at[0, 0:MOE_GROUPS].set(rgb[layer])
    br = br.at[0, ROUTER_EXPERT_ROW:ROUTER_EXPERT_ROW + MOE_EXPERTS].set(reb[layer])
    gain = _row(ffn_norm[layer])
    lpos, tok, runlen, runoff = _router(h, valid, gain, wr, br)
    plan = _moe_plan(runlen, runoff)
    xs = _dispatch(plan, h, gain, lpos, tok)
    ys = _experts(plan, xs, layer, wg, wu, wd)
    return _combine(plan, h, tok, ys, final_gain)


def kernel(x, meta_tokens, norm_final, mix_norm_even, w_in, ssd_conv_w, ssd_conv_b, ssd_dt_bias, ssd_a_log, ssd_d, ssd_norm, lru_conv_w, lru_conv_b, lru_w_a, lru_b_a, lru_w_x, lru_b_x, lru_lambda, w_out, mix_norm_odd, pool_w, pool_b, pool_scale, ffn_norm, router_group_w, router_group_b, router_expert_w, router_expert_b, expert_w_gate, expert_w_up, expert_w_down):
    meta_chunk = jnp.concatenate([jnp.zeros((PAD, D_MODEL), F32), meta_tokens.astype(F32)], axis=0)
    valid = ((jnp.arange(TP, dtype=jnp.int32) % LP) >= PAD).astype(jnp.int32).reshape(1, TP)
    moe_args = (valid, ffn_norm, router_group_w, router_group_b, router_expert_w, router_expert_b,
                expert_w_gate, expert_w_up, expert_w_down)

    wi = w_in[0]
    w_proj = jnp.concatenate(
        [wi[:, 0:1024], wi[:, 1024:2560], wi[:, 2576:3600], wi[:, 3600:4624], wi[:, 2560:2576],
         jnp.zeros((D_MODEL, LANES - SSD_HEADS), F32)], axis=1).astype(BF16)
    piece_head = jnp.where(jnp.arange(LANES) < HEAD_PIECES * SSD_HEADS, jnp.arange(LANES) % SSD_HEADS, -1)
    expand = (piece_head[:, None] == (jnp.arange(SSD_WIDTH) // SSD_HEAD_DIM)[None, :]).astype(BF16)
    params = [
        _row(mix_norm_even[0]), w_proj,
        ssd_conv_w[0].astype(F32), _row(ssd_conv_b[0]), _pad_lanes(ssd_dt_bias[0]), _pad_lanes(ssd_a_log[0]),
        _row(jnp.repeat(ssd_d[0], SSD_HEAD_DIM)), _row(ssd_norm[0]),
        lru_conv_w[0].astype(F32), _row(lru_conv_b[0]), _pair_blockdiag(lru_w_a[0]), _row(lru_b_a[0]),
        _pair_blockdiag(lru_w_x[0]), _row(lru_b_x[0]), _row(lru_lambda[0]),
        w_out[0].astype(BF16), expand,
    ]
    h = _mixer(x.astype(F32), meta_chunk, params).reshape(TP, D_MODEL)
    h = _moe_layer(h, 0, None, *moe_args)

    h = _pool_mixer(h, _row(mix_norm_odd[0]), pool_w[0].astype(BF16), _row(pool_b[0]), _row(pool_scale[0]))
    out = _moe_layer(h, 1, _row(norm_final), *moe_args)
    return out.reshape(BATCH, SEQ, D_MODEL)
```

```python
import functools
import math

import jax
import jax.numpy as jnp
from jax import lax
from jax.experimental import pallas as pl
from jax.experimental.pallas import tpu as pltpu

F32 = jnp.float32
BF16 = jnp.bfloat16
HIGHEST = lax.Precision.HIGHEST

D_MODEL = 1024
BATCH = 8
SEQ = 2048
N_META = 16
RMS_EPS = 1e-6
CONV_WIDTH = 4
CHUNK = 128
PAD = CHUNK - N_META
LP = PAD + N_META + SEQ
NCHUNK = LP // CHUNK
TP = BATCH * LP

SSD_HEADS = 16
SSD_HEAD_DIM = 64
SSD_WIDTH = 1024
SSD_GROUPS = 2
SSD_STATE = 128
SSD_GROUP_WIDTH = SSD_WIDTH // SSD_GROUPS
SSD_CONV_DIM = SSD_WIDTH + 2 * SSD_GROUPS * SSD_STATE
LRU_WIDTH = 1024
LRU_C = 8.0
LANES = 128
POOL_WINDOWS = (2, 4, 8, 16)
POOL_GROUP_DIM = 256
POOL_MAX_WINDOW = 16
MOE_GROUPS = 4
MOE_PER_GROUP = 4
MOE_EXPERTS = 16
D_EXPERT = 512
ROUTER_EXPERT_ROW = 8

VMEM_LIMIT = 56 * 1024 * 1024


def _dot(a, b, precision=None):
    return jnp.dot(a, b, preferred_element_type=F32, precision=precision)


def _rms(x, gain):
    ms = jnp.mean(x * x, axis=-1, keepdims=True)
    return x * lax.rsqrt(ms + RMS_EPS) * gain


def _silu(x):
    return x * jax.nn.sigmoid(x)


def _softplus(x):
    return jnp.maximum(x, 0.0) + jnp.log1p(jnp.exp(-jnp.abs(x)))


def _row_ids(shape, chunk):
    return lax.broadcasted_iota(jnp.int32, shape, 0) + chunk * CHUNK


MIX_B = 2
_Z0, _X0, _G0, _L0, _T0, _PEND = 0, 1024, 2560, 3584, 4608, 4736
HEAD_PIECES = 3


def _causal_conv(hist, x, w_ref, b_ref):
    prev = hist[...]
    hist[...] = x[CHUNK - 8:, :]
    r8 = lax.broadcasted_iota(jnp.int32, prev.shape, 0)
    acc = b_ref[...] + w_ref[CONV_WIDTH - 1:CONV_WIDTH, :] * x
    for j in range(1, CONV_WIDTH):
        shifted = pltpu.roll(x, j, 0)
        head = jnp.where(r8 < j, pltpu.roll(prev, j, 0), shifted[0:8, :])
        shifted = jnp.concatenate([head, shifted[8:, :]], axis=0)
        acc = acc + w_ref[CONV_WIDTH - 1 - j:CONV_WIDTH - j, :] * shifted
    return acc


def _expand_heads(v, exp_ref):
    lane = lax.broadcasted_iota(jnp.int32, v.shape, 1)
    rest = jnp.where(lane < SSD_HEADS, v, 0.0)
    packed = None
    for k in range(HEAD_PIECES):
        piece = rest.astype(BF16).astype(F32)
        rest = rest - piece
        moved = piece if k == 0 else pltpu.roll(piece, k * SSD_HEADS, 1)
        packed = moved if packed is None else packed + moved
    return _dot(packed.astype(BF16), exp_ref[...])


def _ssd_chunk(c, z, xbc, dt_raw, hist, st, cw_ref, cb_ref, dtb_ref, alog_ref, dsk_ref, ng_ref, exp_ref):
    xc = _silu(_causal_conv(hist, xbc, cw_ref, cb_ref))
    xs = xc[:, 0:SSD_WIDTH]
    valid_s = _row_ids((CHUNK, 2 * SSD_STATE), c) >= PAD
    bm = jnp.where(valid_s, xc[:, SSD_WIDTH:SSD_WIDTH + 2 * SSD_STATE], 0.0)
    cm = jnp.where(valid_s, xc[:, SSD_WIDTH + 2 * SSD_STATE:], 0.0)

    li = lax.broadcasted_iota(jnp.int32, (CHUNK, CHUNK), 0)
    si = lax.broadcasted_iota(jnp.int32, (CHUNK, CHUNK), 1)
    causal = si <= li
    dt = _softplus(dt_raw + dtb_ref[...])
    dt = jnp.where(li + c * CHUNK >= PAD, dt, 0.0)
    adt = dt * (-jnp.exp(alog_ref[...]))
    a_cs = _dot(causal.astype(F32), adt, HIGHEST)
    a_last = a_cs[CHUNK - 1:CHUNK, :]
    ea = jnp.exp(a_cs)
    dt_x = _expand_heads(dt, exp_ref)
    w_x = _expand_heads(jnp.exp(a_last - a_cs) * dt, exp_ref)
    ea_x = _expand_heads(ea, exp_ref)
    xdt = xs * dt_x
    a_cs_t = a_cs.T
    lane = lax.broadcasted_iota(jnp.int32, (CHUNK, LANES), 1)

    ys = []
    for g in range(SSD_GROUPS):
        gsl = slice(g * SSD_GROUP_WIDTH, (g + 1) * SSD_GROUP_WIDTH)
        bg = bm[:, g * SSD_STATE:(g + 1) * SSD_STATE]
        cg16 = cm[:, g * SSD_STATE:(g + 1) * SSD_STATE].astype(BF16)
        cbm = lax.dot_general(cg16, bg.astype(BF16), (((1,), (1,)), ((), ())),
                              preferred_element_type=F32)
        s_in = st[:, gsl]
        y_off = _dot(cg16, s_in.astype(BF16)) * ea_x[:, gsl]
        s_new = _dot(bg.T.astype(BF16), (w_x[:, gsl] * xs[:, gsl]).astype(BF16))
        st[:, gsl] = ea_x[CHUNK - 1:CHUNK, gsl] * s_in + s_new
        for j in range(SSD_GROUP_WIDTH // LANES):
            xp = xdt[:, g * SSD_GROUP_WIDTH + j * LANES:
                     g * SSD_GROUP_WIDTH + (j + 1) * LANES].astype(BF16)
            parts = []
            for hh in range(LANES // SSD_HEAD_DIM):
                hd = g * (SSD_HEADS // SSD_GROUPS) + j * (LANES // SSD_HEAD_DIM) + hh
                seg = a_cs[:, hd:hd + 1] - a_cs_t[hd:hd + 1, :]
                dec = jnp.exp(jnp.where(causal, seg, -1e30))
                parts.append(_dot((cbm * dec).astype(BF16), xp))
            ys.append(jnp.where(lane < SSD_HEAD_DIM, parts[0], parts[1])
                      + y_off[:, j * LANES:(j + 1) * LANES])
    y = jnp.concatenate(ys, axis=1) + dsk_ref[...] * xs
    y = y * _silu(z)
    normed = []
    for g in range(SSD_GROUPS):
        yg = y[:, g * SSD_GROUP_WIDTH:(g + 1) * SSD_GROUP_WIDTH]
        normed.append(yg * lax.rsqrt(jnp.mean(yg * yg, axis=-1, keepdims=True) + RMS_EPS))
    return jnp.concatenate(normed, axis=1) * ng_ref[...]


def _lru_chunk(c, gt, lin, hist, lc, lcw_ref, lcb_ref, wa_ref, ba_ref, wx_ref, bx_ref, lam_ref):
    xb = _causal_conv(hist, lin, lcw_ref, lcb_ref)
    xb16 = xb.astype(BF16)
    nblk = LRU_WIDTH // LANES
    ra = jnp.concatenate(
        [_dot(xb16[:, j * LANES:(j + 1) * LANES], wa_ref[j]) for j in range(nblk)], axis=1)
    ix = jnp.concatenate(
        [_dot(xb16[:, j * LANES:(j + 1) * LANES], wx_ref[j]) for j in range(nblk)], axis=1)
    r = jax.nn.sigmoid(ra + ba_ref[...])
    ig = jax.nn.sigmoid(ix + bx_ref[...])
    log_a = (-LRU_C * _softplus(-lam_ref[...])) * r
    a = jnp.exp(log_a)
    mult = jnp.sqrt(jnp.tanh(-log_a) * (a * a + 1.0))
    grow = _row_ids((CHUNK, LRU_WIDTH), c)
    mult = jnp.where(grow == PAD, 1.0, mult)
    u = jnp.where(grow >= PAD, mult * (ig * xb), 0.0)

    r8 = lax.broadcasted_iota(jnp.int32, (8, LRU_WIDTH), 0)
    carry = lc[0:1, :]
    hs = []
    for j in range(CHUNK // 8):
        a8 = a[j * 8:(j + 1) * 8, :]
        u8 = u[j * 8:(j + 1) * 8, :]
        for d in (1, 2, 4):
            a_sh = jnp.where(r8 >= d, pltpu.roll(a8, d, 0), 1.0)
            u_sh = jnp.where(r8 >= d, pltpu.roll(u8, d, 0), 0.0)
            u8 = a8 * u_sh + u8
            a8 = a8 * a_sh
        h8 = a8 * carry + u8
        carry = h8[7:8, :]
        hs.append(h8)
    lc[0:1, :] = carry
    gelu = 0.5 * gt * (1.0 + jnp.tanh(math.sqrt(2.0 / math.pi) * (gt + 0.044715 * (gt * gt * gt))))
    return jnp.concatenate(hs, axis=0) * gelu


def _mixer_kernel(x_ref, meta_ref, gin_ref, wproj_ref,
                  cw_ref, cb_ref, dtb_ref, alog_ref, dsk_ref, ng_ref,
                  lcw_ref, lcb_ref, wa_ref, ba_ref, wx_ref, bx_ref, lam_ref,
                  wout_ref, exp_ref, o_ref, hbuf, hist_x, hist_l, st, lc):
    c = pl.program_id(1)

    @pl.when(c == 0)
    def _():
        for b in range(MIX_B):
            hbuf[b * CHUNK:(b + 1) * CHUNK, :] = meta_ref[...]
        hist_x[...] = jnp.zeros_like(hist_x)
        hist_l[...] = jnp.zeros_like(hist_l)
        st[...] = jnp.zeros_like(st)
        lc[...] = jnp.zeros_like(lc)

    @pl.when(c > 0)
    def _():
        for b in range(MIX_B):
            hbuf[b * CHUNK:(b + 1) * CHUNK, :] = x_ref[b]

    h = hbuf[...]
    hn = _rms(h, gin_ref[...]).astype(BF16)
    z = _dot(hn, wproj_ref[:, _Z0:_X0])
    xbc = _dot(hn, wproj_ref[:, _X0:_G0])
    gate = _dot(hn, wproj_ref[:, _G0:_L0])
    lin = _dot(hn, wproj_ref[:, _L0:_T0])
    dt_raw = _dot(hn, wproj_ref[:, _T0:_PEND])

    mixed = []
    for b in range(MIX_B):
        rows = slice(b * CHUNK, (b + 1) * CHUNK)
        y_ssd = _ssd_chunk(c, z[rows], xbc[rows], dt_raw[rows], hist_x.at[b], st.at[b],
                           cw_ref, cb_ref, dtb_ref, alog_ref, dsk_ref, ng_ref, exp_ref)
        y_lru = _lru_chunk(c, gate[rows], lin[rows], hist_l.at[b], lc.at[b],
                           lcw_ref, lcb_ref, wa_ref, ba_ref, wx_ref, bx_ref, lam_ref)
        mixed.append(jnp.concatenate([y_ssd, y_lru], axis=1).astype(BF16))
    out = _dot(jnp.concatenate(mixed, axis=0), wout_ref[...]) + h
    keep = _row_ids((CHUNK, D_MODEL), c) >= PAD
    for b in range(MIX_B):
        o_ref[b] = jnp.where(keep, out[b * CHUNK:(b + 1) * CHUNK, :], 0.0)


def _mixer(x, meta_chunk, params):
    full = lambda a: pl.BlockSpec(a.shape, lambda b, c: (0,) * a.ndim, pipeline_mode=pl.Buffered(1))
    return pl.pallas_call(
        _mixer_kernel,
        grid=(BATCH // MIX_B, NCHUNK),
        in_specs=[pl.BlockSpec((MIX_B, CHUNK, D_MODEL), lambda b, c: (b, jnp.maximum(c - 1, 0), 0)),
                  full(meta_chunk)] + [full(p) for p in params],
        out_specs=pl.BlockSpec((MIX_B, CHUNK, D_MODEL), lambda b, c: (b, c, 0)),
        out_shape=jax.ShapeDtypeStruct((BATCH, LP, D_MODEL), F32),
        scratch_shapes=[
            pltpu.VMEM((MIX_B * CHUNK, D_MODEL), F32),
            pltpu.VMEM((MIX_B, 8, SSD_CONV_DIM), F32),
            pltpu.VMEM((MIX_B, 8, LRU_WIDTH), F32),
            pltpu.VMEM((MIX_B, SSD_STATE, SSD_WIDTH), F32),
            pltpu.VMEM((MIX_B, 8, LRU_WIDTH), F32),
        ],
        compiler_params=pltpu.CompilerParams(
            dimension_semantics=("arbitrary", "arbitrary"), vmem_limit_bytes=VMEM_LIMIT),
        name="ssd_lru_mixer",
    )(x, meta_chunk, *params)


POOL_TM = 512


def _pool_kernel(h_ref, g_ref, pw_ref, pb_ref, ps_ref, o_ref, buf):
    base = POOL_MAX_WINDOW
    for k in range(POOL_TM // CHUNK):
        chunk = pl.program_id(0) * (POOL_TM // CHUNK) + k
        c = lax.rem(chunk, NCHUNK)

        @pl.when(c == 0)
        def _():
            buf[0:base, :] = jnp.zeros((base, D_MODEL), F32)

        h = h_ref[k * CHUNK:(k + 1) * CHUNK, :]
        hn = _rms(h, g_ref[...])
        buf[base:base + CHUNK, :] = hn
        pos = _row_ids((CHUNK, POOL_GROUP_DIM), c) - PAD
        outs = []
        for g, w in enumerate(POOL_WINDOWS):
            sl = slice(g * POOL_GROUP_DIM, (g + 1) * POOL_GROUP_DIM)
            ws = buf[:, sl]
            s = 1
            while s < w:
                ws = ws + pltpu.roll(ws, s, 0)
                s *= 2
            ws = ws[base:, :]
            count = jnp.clip(pos + 1, 1, w).astype(F32)
            pooled = ws / count - hn[:, sl]
            outs.append(_dot(pooled.astype(BF16), pw_ref[g]))
        buf[0:base, :] = buf[CHUNK:CHUNK + base, :]
        y = (jnp.concatenate(outs, axis=1) + pb_ref[...]) * ps_ref[...]
        o_ref[k * CHUNK:(k + 1) * CHUNK, :] = jnp.where(_row_ids((CHUNK, D_MODEL), c) >= PAD, h + y, 0.0)


def _pool_mixer(h, gain, pw, pb, ps):
    blk = pl.BlockSpec((POOL_TM, D_MODEL), lambda i: (i, 0))
    full = lambda a: pl.BlockSpec(a.shape, lambda i: (0,) * a.ndim)
    return pl.pallas_call(
        _pool_kernel,
        grid=(TP // POOL_TM,),
        in_specs=[blk, full(gain), full(pw), full(pb), full(ps)],
        out_specs=blk,
        out_shape=jax.ShapeDtypeStruct((TP, D_MODEL), F32),
        scratch_shapes=[pltpu.VMEM((CHUNK + POOL_MAX_WINDOW, D_MODEL), F32)],
        compiler_params=pltpu.CompilerParams(
            dimension_semantics=("arbitrary",), vmem_limit_bytes=VMEM_LIMIT),
        name="pool_mixer",
    )(h, gain, pw, pb, ps)


MOE_TM = 512
N_MOE_TILES = TP // MOE_TM
RUN_ALIGN = 16
RUN_SHIFT = 4
RUN_BIG = 64
RUN_BIG_SHIFT = 6
LOCAL_ROWS = 1280
EXPERT_TILE = 512
N_ROUTED = BATCH * (N_META + SEQ)
MAX_ROWS = (2 * N_ROUTED + N_MOE_TILES * MOE_EXPERTS * (RUN_ALIGN - 1)
            + MOE_EXPERTS * (EXPERT_TILE - 1))
NT_MAX = -(-MAX_ROWS // EXPERT_TILE)
NR = NT_MAX * EXPERT_TILE
XS_COLS = D_MODEL + LANES
INT_ROWS = 8
TOK_GATE1, TOK_GATE2, TOK_POS1, TOK_POS2 = 0, 3, 6, 7
assert LOCAL_ROWS >= 2 * MOE_TM + MOE_EXPERTS * (RUN_ALIGN - 1) and LOCAL_ROWS % LANES == 0


def _first_argmax(vals):
    best, idx = vals[0], jnp.zeros(vals[0].shape, jnp.int32)
    for k in range(1, len(vals)):
        better = vals[k] > best
        idx = jnp.where(better, k, idx)
        best = jnp.where(better, vals[k], best)
    return idx, best


def _softmax_rows(vals):
    m = functools.reduce(jnp.maximum, vals)
    ex = [jnp.exp(v - m) for v in vals]
    tot = functools.reduce(lambda p, q: p + q, ex)
    return [e / tot for e in ex]


def _bf16_pieces(x):
    hi = x.astype(BF16).astype(F32)
    rest = x - hi
    mid = rest.astype(BF16).astype(F32)
    lo = (rest - mid).astype(BF16).astype(F32)
    return [hi, mid, lo]


def _router_kernel(h_ref, valid_ref, g_ref, wrh_ref, wrl_ref, br_ref, lpos_ref, tok_ref, runlen_ref, runoff_ref):
    hn = _rms(h_ref[...], g_ref[...])
    hn_hi = hn.astype(BF16)
    hn_lo = (hn - hn_hi.astype(F32)).astype(BF16)
    logits = (_dot(hn_hi, wrh_ref[...]) + (_dot(hn_hi, wrl_ref[...]) + _dot(hn_lo, wrh_ref[...]))
              + br_ref[...])
    lt = logits.T
    p_group = _softmax_rows([lt[k:k + 1, :] for k in range(MOE_GROUPS)])
    g_sel, p_g = _first_argmax(p_group)
    fine = []
    for k in range(MOE_PER_GROUP):
        f = lt[ROUTER_EXPERT_ROW + k:ROUTER_EXPERT_ROW + k + 1, :]
        for g in range(1, MOE_GROUPS):
            r0 = ROUTER_EXPERT_ROW + g * MOE_PER_GROUP + k
            f = jnp.where(g_sel == g, lt[r0:r0 + 1, :], f)
        fine.append(f)
    q = _softmax_rows(fine)
    i1, t1 = _first_argmax(q)
    i2, t2 = _first_argmax([jnp.where(i1 == k, -1.0, q[k]) for k in range(MOE_PER_GROUP)])
    tot = t1 + t2
    gate1 = p_g * (t1 / tot)
    gate2 = p_g * (t2 / tot)

    valid = valid_ref[...] > 0
    e1 = jnp.where(valid, g_sel * MOE_PER_GROUP + i1, -1)
    e2 = jnp.where(valid, g_sel * MOE_PER_GROUP + i2, -1)
    erow = lax.broadcasted_iota(jnp.int32, (MOE_EXPERTS, MOE_TM), 0)
    hit1 = erow == e1
    hit2 = erow == e2
    onehot = jnp.where(hit1 | hit2, 1.0, 0.0)
    si = lax.broadcasted_iota(jnp.int32, (MOE_TM, MOE_TM), 0)
    ti = lax.broadcasted_iota(jnp.int32, (MOE_TM, MOE_TM), 1)
    before = jnp.where(si < ti, 1.0, 0.0).astype(BF16)
    seen = _dot(onehot.astype(BF16), before)
    count = jnp.sum(onehot, axis=1, keepdims=True).astype(jnp.int32)
    runlen = lax.shift_left(lax.shift_right_logical(count + (RUN_ALIGN - 1), RUN_SHIFT), RUN_SHIFT)
    runlen_b = jnp.broadcast_to(runlen, (MOE_EXPERTS, LANES))
    ei = lax.broadcasted_iota(jnp.int32, (MOE_EXPERTS, MOE_EXPERTS), 0)
    ej = lax.broadcasted_iota(jnp.int32, (MOE_EXPERTS, MOE_EXPERTS), 1)
    runoff_b = _dot(jnp.where(ej < ei, 1.0, 0.0), runlen_b.astype(F32), HIGHEST)
    runlen_ref[...] = runlen_b
    runoff_ref[...] = runoff_b.astype(jnp.int32)
    place = seen + runoff_b[:, 0:1]
    pos1 = jnp.where(valid, jnp.sum(jnp.where(hit1, place, 0.0), axis=0, keepdims=True), -1.0)
    pos2 = jnp.where(valid, jnp.sum(jnp.where(hit2, place, 0.0), axis=0, keepdims=True), -1.0)
    r8 = lax.broadcasted_iota(jnp.int32, (INT_ROWS, MOE_TM), 0)
    lpos_ref[...] = jnp.where(r8 == 0, pos1.astype(jnp.int32),
                              jnp.where(r8 == 1, pos2.astype(jnp.int32), 0))

    rows = lax.broadcasted_iota(jnp.int32, lt.shape, 0)
    table = jnp.zeros(lt.shape, F32)
    for k, piece in enumerate(_bf16_pieces(gate1)):
        table = jnp.where(rows == TOK_GATE1 + k, piece, table)
    for k, piece in enumerate(_bf16_pieces(gate2)):
        table = jnp.where(rows == TOK_GATE2 + k, piece, table)
    table = jnp.where(rows == TOK_POS1, pos1, jnp.where(rows == TOK_POS2, pos2, table))
    tok_ref[...] = table.T


def _router(h, valid, gain, wr, br):
    row = lambda i: (i, 0)
    col = lambda i: (0, i)
    const = lambda i: (0, 0)
    wr_hi = wr.astype(BF16)
    wr_lo = (wr - wr_hi.astype(F32)).astype(BF16)
    return pl.pallas_call(
        _router_kernel,
        grid=(N_MOE_TILES,),
        in_specs=[
            pl.BlockSpec((MOE_TM, D_MODEL), row),
            pl.BlockSpec((1, MOE_TM), col),
            pl.BlockSpec((1, D_MODEL), const),
            pl.BlockSpec((D_MODEL, LANES), const),
            pl.BlockSpec((D_MODEL, LANES), const),
            pl.BlockSpec((1, LANES), const),
        ],
        out_specs=[pl.BlockSpec((INT_ROWS, MOE_TM), col),
                   pl.BlockSpec((MOE_TM, LANES), row),
                   pl.BlockSpec((MOE_EXPERTS, LANES), row),
                   pl.BlockSpec((MOE_EXPERTS, LANES), row)],
        out_shape=[jax.ShapeDtypeStruct((INT_ROWS, TP), jnp.int32),
                   jax.ShapeDtypeStruct((TP, LANES), F32),
                   jax.ShapeDtypeStruct((N_MOE_TILES * MOE_EXPERTS, LANES), jnp.int32),
                   jax.ShapeDtypeStruct((N_MOE_TILES * MOE_EXPERTS, LANES), jnp.int32)],
        compiler_params=pltpu.CompilerParams(
            dimension_semantics=("arbitrary",), vmem_limit_bytes=VMEM_LIMIT),
        name="moe_router",
    )(h, valid, gain, wr_hi, wr_lo, br)


def _moe_plan(runlen, runoff):
    i32 = jnp.int32
    lens = runlen[:, 0].reshape(N_MOE_TILES, MOE_EXPERTS)
    total = jnp.sum(lens, axis=0)
    padded = (total + (EXPERT_TILE - 1)) // EXPERT_TILE * EXPERT_TILE
    ends = jnp.cumsum(padded)
    base = ends - padded
    goff = base[None, :] + jnp.cumsum(lens, axis=0) - lens
    nval = ends[-1] // EXPERT_TILE
    tidx = jnp.minimum(jnp.arange(NT_MAX, dtype=i32), nval - 1)
    texp = jnp.minimum(jnp.sum(tidx[:, None] >= (ends // EXPERT_TILE)[None, :], axis=1), MOE_EXPERTS - 1)
    return dict(
        goff=goff.reshape(-1).astype(i32), lens=lens.reshape(-1).astype(i32),
        loff=runoff[:, 0].astype(i32), gap_start=(base + total).astype(i32),
        gap_len=(padded - total).astype(i32), nval=nval.reshape(1).astype(i32),
        tidx=tidx.astype(i32), texp=texp.astype(i32),
        nbig=jnp.sum(lens // RUN_BIG, axis=1).astype(i32),
        nsmall=jnp.sum(lens % RUN_BIG // RUN_ALIGN, axis=1).astype(i32))


def _aligned(x):
    return pl.multiple_of(x, RUN_ALIGN)


def _for_each_run(step, lens_ref, fn):
    def expert_body(e, carry):
        k = step * MOE_EXPERTS + e
        nbig = lax.shift_right_logical(lens_ref[k], RUN_BIG_SHIFT)
        rest = nbig * RUN_BIG

        def big_body(q, c):
            fn(k, q * RUN_BIG, RUN_BIG)
            return c

        def small_body(q, c):
            fn(k, rest + q * RUN_ALIGN, RUN_ALIGN)
            return c

        lax.fori_loop(0, nbig, big_body, 0)
        lax.fori_loop(0, lax.shift_right_logical(lens_ref[k] - rest, RUN_SHIFT), small_body, 0)
        return carry

    lax.fori_loop(0, MOE_EXPERTS, expert_body, 0)


def _wait_runs(copy, step, nbig_ref, nsmall_ref):
    def big_body(q, c):
        copy(RUN_BIG).wait()
        return c

    def small_body(q, c):
        copy(RUN_ALIGN).wait()
        return c

    lax.fori_loop(0, nbig_ref[step], big_body, 0)
    lax.fori_loop(0, nsmall_ref[step], small_body, 0)


def _dispatch_kernel(goff_ref, lens_ref, loff_ref, gaps_ref, gapl_ref, nval_ref, nbig_ref, nsmall_ref,
                     h_ref, g_ref, lpos_ref, tok_ref, xs_ref, local, zeros, sems):
    i = pl.program_id(0)
    last = pl.num_programs(0) - 1
    slot = lax.rem(i, 2)

    def run_copy(s, src, dst, rows):
        return pltpu.make_async_copy(local.at[s, pl.ds(_aligned(src), rows), :],
                                     xs_ref.at[pl.ds(_aligned(dst), rows), :], sems.at[s])

    def wait_tile(step):
        s = lax.rem(step, 2)
        _wait_runs(lambda rows: run_copy(s, 0, 0, rows), step, nbig_ref, nsmall_ref)

    @pl.when(i >= 2)
    def _():
        wait_tile(i - 2)

    hn = _rms(h_ref[...], g_ref[...]).astype(BF16)
    riota = lax.broadcasted_iota(jnp.int32, (LOCAL_ROWS, MOE_TM), 0)
    c1 = riota == lpos_ref[0:1, :]
    c2 = riota == lpos_ref[1:2, :]
    local[slot, :, 0:D_MODEL] = _dot(jnp.where(c1 | c2, 1.0, 0.0).astype(BF16), hn).astype(BF16)
    tok = tok_ref[...]
    lane = lax.broadcasted_iota(jnp.int32, (MOE_TM, LANES), 1)
    t1 = jnp.where(lane < TOK_GATE2, tok, 0.0).astype(BF16)
    t2 = jnp.where((lane >= TOK_GATE2) & (lane < TOK_POS1), tok, 0.0).astype(BF16)
    gate_cols = (_dot(jnp.where(c1, 1.0, 0.0).astype(BF16), t1)
                 + _dot(jnp.where(c2, 1.0, 0.0).astype(BF16), t2))
    local[slot, :, D_MODEL:XS_COLS] = gate_cols.astype(BF16)

    _for_each_run(i, lens_ref,
                  lambda k, q, rows: run_copy(slot, loff_ref[k] + q, goff_ref[k] + q, rows).start())

    @pl.when(i == last)
    def _():
        zeros[...] = jnp.zeros_like(zeros)
        zsem = sems.at[2]

        def gap_copy(dst):
            return pltpu.make_async_copy(zeros.at[pl.ds(0, RUN_ALIGN), :],
                                         xs_ref.at[pl.ds(_aligned(dst), RUN_ALIGN), :], zsem)

        def tile_copy(t):
            return pltpu.make_async_copy(
                zeros, xs_ref.at[pl.ds(pl.multiple_of(t * EXPERT_TILE, EXPERT_TILE), EXPERT_TILE), :], zsem)

        def expert_body(e, n):
            g = lax.shift_right_logical(gapl_ref[e], RUN_SHIFT)

            def body(q, c):
                gap_copy(gaps_ref[e] + q * RUN_ALIGN).start()
                return c

            lax.fori_loop(0, g, body, 0)
            return n + g

        ngap = lax.fori_loop(0, MOE_EXPERTS, expert_body, 0)

        def tail_start(t, c):
            tile_copy(t).start()
            return c

        lax.fori_loop(nval_ref[0], NT_MAX, tail_start, 0)

        def gap_wait(k, c):
            gap_copy(0).wait()
            return c

        lax.fori_loop(0, ngap, gap_wait, 0)

        def tail_wait(t, c):
            tile_copy(t).wait()
            return c

        lax.fori_loop(nval_ref[0], NT_MAX, tail_wait, 0)

        @pl.when(i >= 1)
        def _():
            wait_tile(i - 1)

        wait_tile(i)


def _dispatch(plan, h, gain, lpos, tok):
    row = lambda i, *_: (i, 0)
    return pl.pallas_call(
        _dispatch_kernel,
        grid_spec=pltpu.PrefetchScalarGridSpec(
            num_scalar_prefetch=8,
            grid=(N_MOE_TILES,),
            in_specs=[
                pl.BlockSpec((MOE_TM, D_MODEL), row),
                pl.BlockSpec((1, D_MODEL), lambda i, *_: (0, 0)),
                pl.BlockSpec((INT_ROWS, MOE_TM), lambda i, *_: (0, i)),
                pl.BlockSpec((MOE_TM, LANES), row),
            ],
            out_specs=pl.BlockSpec(memory_space=pl.ANY),
            scratch_shapes=[pltpu.VMEM((2, LOCAL_ROWS, XS_COLS), BF16),
                            pltpu.VMEM((EXPERT_TILE, XS_COLS), BF16),
                            pltpu.SemaphoreType.DMA((3,))],
        ),
        out_shape=jax.ShapeDtypeStruct((NR, XS_COLS), BF16),
        compiler_params=pltpu.CompilerParams(
            dimension_semantics=("arbitrary",), vmem_limit_bytes=VMEM_LIMIT),
        name="moe_dispatch",
    )(plan["goff"], plan["lens"], plan["loff"], plan["gap_start"], plan["gap_len"], plan["nval"],
      plan["nbig"], plan["nsmall"], h, gain, lpos, tok)


def _experts_kernel(tidx_ref, texp_ref, nval_ref, xs_ref, wg_ref, wu_ref, wd_ref, ys_ref, wg16, wu16, wd16):
    j = pl.program_id(0)
    active = j < nval_ref[0]
    new_expert = (j == 0) | (texp_ref[j] != texp_ref[jnp.maximum(j - 1, 0)])

    @pl.when(active & new_expert)
    def _():
        wg16[...] = wg_ref[0].astype(BF16)
        wu16[...] = wu_ref[0].astype(BF16)
        wd16[...] = wd_ref[0].astype(BF16)

    @pl.when(active)
    def _():
        x = xs_ref[:, 0:D_MODEL]
        gate = jnp.sum(xs_ref[:, D_MODEL:XS_COLS].astype(F32), axis=1, keepdims=True)
        a = _dot(x, wg16[...])
        b = _dot(x, wu16[...])
        y = _dot((_silu(a) * b).astype(BF16), wd16[...])
        ys_ref[...] = (gate * y).astype(BF16)

    @pl.when(jnp.logical_not(active))
    def _():
        ys_ref[...] = jnp.zeros_like(ys_ref)


def _experts(plan, xs, layer, wg, wu, wd):
    tile = lambda j, tidx, texp, nval: (tidx[j], 0)
    expert = lambda j, tidx, texp, nval: (layer, texp[j], 0, 0)
    return pl.pallas_call(
        _experts_kernel,
        grid_spec=pltpu.PrefetchScalarGridSpec(
            num_scalar_prefetch=3,
            grid=(NT_MAX,),
            in_specs=[
                pl.BlockSpec((EXPERT_TILE, XS_COLS), tile),
                pl.BlockSpec((None, 1, D_MODEL, D_EXPERT), expert),
                pl.BlockSpec((None, 1, D_MODEL, D_EXPERT), expert),
                pl.BlockSpec((None, 1, D_EXPERT, D_MODEL), expert),
            ],
            out_specs=pl.BlockSpec((EXPERT_TILE, D_MODEL), lambda j, *_: (j, 0)),
            scratch_shapes=[pltpu.VMEM((D_MODEL, D_EXPERT), BF16),
                            pltpu.VMEM((D_MODEL, D_EXPERT), BF16),
                            pltpu.VMEM((D_EXPERT, D_MODEL), BF16)],
        ),
        out_shape=jax.ShapeDtypeStruct((NR, D_MODEL), BF16),
        compiler_params=pltpu.CompilerParams(
            dimension_semantics=("arbitrary",), vmem_limit_bytes=VMEM_LIMIT),
        name="moe_experts",
    )(plan["tidx"], plan["texp"], plan["nval"], xs, wg, wu, wd)


def _combined_rows(goff_ref, lens_ref, loff_ref, nbig_ref, nsmall_ref, h_ref, tok_ref, ys_ref, local, sems):
    i = pl.program_id(0)
    slot = lax.rem(i, 2)

    def run_copy(s, src, dst, rows):
        return pltpu.make_async_copy(ys_ref.at[pl.ds(_aligned(src), rows), :],
                                     local.at[s, pl.ds(_aligned(dst), rows), :], sems.at[s])

    def start_tile(step):
        s = lax.rem(step, 2)
        _for_each_run(step, lens_ref,
                      lambda k, q, rows: run_copy(s, goff_ref[k] + q, loff_ref[k] + q, rows).start())

    @pl.when(i == 0)
    def _():
        local[...] = jnp.zeros_like(local)
        start_tile(0)

    @pl.when(i < pl.num_programs(0) - 1)
    def _():
        start_tile(i + 1)

    _wait_runs(lambda rows: run_copy(slot, 0, 0, rows), i, nbig_ref, nsmall_ref)
    tok = tok_ref[...]
    p1 = tok[:, TOK_POS1:TOK_POS1 + 1].astype(jnp.int32)
    p2 = tok[:, TOK_POS2:TOK_POS2 + 1].astype(jnp.int32)
    ciota = lax.broadcasted_iota(jnp.int32, (MOE_TM, LOCAL_ROWS), 1)
    sel = jnp.where((ciota == p1) | (ciota == p2), 1.0, 0.0).astype(BF16)
    return h_ref[...] + _dot(sel, local[slot])


def _combine_kernel(goff_ref, lens_ref, loff_ref, nbig_ref, nsmall_ref, h_ref, tok_ref, ys_ref, o_ref,
                    local, sems):
    o_ref[...] = _combined_rows(goff_ref, lens_ref, loff_ref, nbig_ref, nsmall_ref, h_ref, tok_ref, ys_ref,
                                local, sems)


def _combine_final_kernel(goff_ref, lens_ref, loff_ref, nbig_ref, nsmall_ref, h_ref, tok_ref, g_ref, ys_ref,
                          o_ref, local, stage, sems, out_sems):
    i = pl.program_id(0)
    last = pl.num_programs(0) - 1
    slot = lax.rem(i, 2)
    per_tile = MOE_TM // CHUNK

    def for_each_seq_chunk(step, fn):
        s = lax.rem(step, 2)
        for k in range(per_tile):
            chunk = step * per_tile + k
            c = lax.rem(chunk, NCHUNK)
            dst = (lax.div(chunk, NCHUNK) * (SEQ // CHUNK) + c - 1) * CHUNK

            @pl.when(c > 0)
            def _():
                fn(pltpu.make_async_copy(stage.at[s, pl.ds(k * CHUNK, CHUNK), :],
                                         o_ref.at[pl.ds(pl.multiple_of(dst, CHUNK), CHUNK), :],
                                         out_sems.at[s]))

    @pl.when(i >= 2)
    def _():
        for_each_seq_chunk(i - 2, lambda copy: copy.wait())

    out = _combined_rows(goff_ref, lens_ref, loff_ref, nbig_ref, nsmall_ref, h_ref, tok_ref, ys_ref,
                         local, sems)
    stage[slot] = _rms(out, g_ref[...])
    for_each_seq_chunk(i, lambda copy: copy.start())

    @pl.when(i == last)
    def _():
        @pl.when(i >= 1)
        def _():
            for_each_seq_chunk(i - 1, lambda copy: copy.wait())

        for_each_seq_chunk(i, lambda copy: copy.wait())


def _combine(plan, h, tok, ys, final_gain=None):
    final = final_gain is not None
    row = lambda i, *_: (i, 0)
    in_specs = [pl.BlockSpec((MOE_TM, D_MODEL), row), pl.BlockSpec((MOE_TM, LANES), row)]
    scratch = [pltpu.VMEM((2, LOCAL_ROWS, D_MODEL), BF16)]
    args = [h, tok]
    if final:
        in_specs.append(pl.BlockSpec((1, D_MODEL), lambda i, *_: (0, 0)))
        scratch.append(pltpu.VMEM((2, MOE_TM, D_MODEL), F32))
        args.append(final_gain)
    return pl.pallas_call(
        _combine_final_kernel if final else _combine_kernel,
        grid_spec=pltpu.PrefetchScalarGridSpec(
            num_scalar_prefetch=5,
            grid=(N_MOE_TILES,),
            in_specs=in_specs + [pl.BlockSpec(memory_space=pl.ANY)],
            out_specs=(pl.BlockSpec(memory_space=pl.ANY) if final
                       else pl.BlockSpec((MOE_TM, D_MODEL), row)),
            scratch_shapes=scratch + [pltpu.SemaphoreType.DMA((2,))] * (2 if final else 1),
        ),
        out_shape=jax.ShapeDtypeStruct((BATCH * SEQ if final else TP, D_MODEL), F32),
        compiler_params=pltpu.CompilerParams(
            dimension_semantics=("arbitrary",), vmem_limit_bytes=VMEM_LIMIT),
        name="moe_combine_final" if final else "moe_combine",
    )(plan["goff"], plan["lens"], plan["loff"], plan["nbig"], plan["nsmall"], *args, ys)


def _row(v):
    return v.reshape(1, -1).astype(F32)


def _pad_lanes(v):
    return jnp.pad(_row(v), ((0, 0), (0, LANES - v.shape[-1])))


def _pair_blockdiag(w):
    w = w.reshape(LRU_WIDTH // LANES, 2, 64, 64)
    z = jnp.zeros_like(w[:, 0])
    top = jnp.concatenate([w[:, 0], z], axis=2)
    bot = jnp.concatenate([z, w[:, 1]], axis=2)
    return jnp.concatenate([top, bot], axis=1).astype(BF16)


def _moe_layer(h, layer, final_gain, valid, ffn_norm, rgw, rgb, rew, reb, wg, wu, wd):
    wr = jnp.zeros((D_MODEL, LANES), F32)
    wr = wr.at[:, 0:MOE_GROUPS].set(rgw[layer])
    wr = wr.at[:, ROUTER_EXPERT_ROW:ROUTER_EXPERT_ROW + MOE_EXPERTS].set(rew[layer])
    br = jnp.zeros((1, LANES), F32)
    br = br.at[0, 0:MOE_GROUPS].set(rgb[layer])
    br = br.at[0, ROUTER_EXPERT_ROW:ROUTER_EXPERT_ROW + MOE_EXPERTS].set(reb[layer])
    gain = _row(ffn_norm[layer])
    lpos, tok, runlen, runoff = _router(h, valid, gain, wr, br)
    plan = _moe_plan(runlen, runoff)
    xs = _dispatch(plan, h, gain, lpos, tok)
    ys = _experts(plan, xs, layer, wg, wu, wd)
    return _combine(plan, h, tok, ys, final_gain)


def kernel(x, meta_tokens, norm_final, mix_norm_even, w_in, ssd_conv_w, ssd_conv_b, ssd_dt_bias, ssd_a_log, ssd_d, ssd_norm, lru_conv_w, lru_conv_b, lru_w_a, lru_b_a, lru_w_x, lru_b_x, lru_lambda, w_out, mix_norm_odd, pool_w, pool_b, pool_scale, ffn_norm, router_group_w, router_group_b, router_expert_w, router_expert_b, expert_w_gate, expert_w_up, expert_w_down):
    meta_chunk = jnp.concatenate([jnp.zeros((PAD, D_MODEL), F32), meta_tokens.astype(F32)], axis=0)
    valid = ((jnp.arange(TP, dtype=jnp.int32) % LP) >= PAD).astype(jnp.int32).reshape(1, TP)
    moe_args = (valid, ffn_norm, router_group_w, router_group_b, router_expert_w, router_expert_b,
                expert_w_gate, expert_w_up, expert_w_down)

    wi = w_in[0]
    w_proj = jnp.concatenate(
        [wi[:, 0:1024], wi[:, 1024:2560], wi[:, 2576:3600], wi[:, 3600:4624], wi[:, 2560:2576],
         jnp.zeros((D_MODEL, LANES - SSD_HEADS), F32)], axis=1).astype(BF16)
    piece_head = jnp.where(jnp.arange(LANES) < HEAD_PIECES * SSD_HEADS, jnp.arange(LANES) % SSD_HEADS, -1)
    expand = (piece_head[:, None] == (jnp.arange(SSD_WIDTH) // SSD_HEAD_DIM)[None, :]).astype(BF16)
    params = [
        _row(mix_norm_even[0]), w_proj,
        ssd_conv_w[0].astype(F32), _row(ssd_conv_b[0]), _pad_lanes(ssd_dt_bias[0]), _pad_lanes(ssd_a_log[0]),
        _row(jnp.repeat(ssd_d[0], SSD_HEAD_DIM)), _row(ssd_norm[0]),
        lru_conv_w[0].astype(F32), _row(lru_conv_b[0]), _pair_blockdiag(lru_w_a[0]), _row(lru_b_a[0]),
        _pair_blockdiag(lru_w_x[0]), _row(lru_b_x[0]), _row(lru_lambda[0]),
        w_out[0].astype(BF16), expand,
    ]
    h = _mixer(x.astype(F32), meta_chunk, params).reshape(TP, D_MODEL)
    h = _moe_layer(h, 0, None, *moe_args)

    h = _pool_mixer(h, _row(mix_norm_odd[0]), pool_w[0].astype(BF16), _row(pool_b[0]), _row(pool_scale[0]))
    out = _moe_layer(h, 1, _row(norm_final), *moe_args)
    return out.reshape(BATCH, SEQ, D_MODEL)
```

```python
import functools
import math

import jax
import jax.numpy as jnp
from jax import lax
from jax.experimental import pallas as pl
from jax.experimental.pallas import tpu as pltpu

F32 = jnp.float32
BF16 = jnp.bfloat16
HIGHEST = lax.Precision.HIGHEST

D_MODEL = 1024
BATCH = 8
SEQ = 2048
N_META = 16
RMS_EPS = 1e-6
CONV_WIDTH = 4
CHUNK = 128
PAD = CHUNK - N_META
LP = PAD + N_META + SEQ
NCHUNK = LP // CHUNK
TP = BATCH * LP

SSD_HEADS = 16
SSD_HEAD_DIM = 64
SSD_WIDTH = 1024
SSD_GROUPS = 2
SSD_STATE = 128
SSD_GROUP_WIDTH = SSD_WIDTH // SSD_GROUPS
SSD_CONV_DIM = SSD_WIDTH + 2 * SSD_GROUPS * SSD_STATE
LRU_WIDTH = 1024
LRU_C = 8.0
LANES = 128
POOL_WINDOWS = (2, 4, 8, 16)
POOL_GROUP_DIM = 256
POOL_MAX_WINDOW = 16
MOE_GROUPS = 4
MOE_PER_GROUP = 4
MOE_EXPERTS = 16
D_EXPERT = 512
ROUTER_EXPERT_ROW = 8

VMEM_LIMIT = 56 * 1024 * 1024


def _dot(a, b, precision=None):
    return jnp.dot(a, b, preferred_element_type=F32, precision=precision)


def _rms(x, gain):
    ms = jnp.mean(x * x, axis=-1, keepdims=True)
    return x * lax.rsqrt(ms + RMS_EPS) * gain


def _silu(x):
    return x * jax.nn.sigmoid(x)


def _softplus(x):
    return jnp.maximum(x, 0.0) + jnp.log1p(jnp.exp(-jnp.abs(x)))


def _row_ids(shape, chunk):
    return lax.broadcasted_iota(jnp.int32, shape, 0) + chunk * CHUNK


MIX_B = 2
N_MIX_ITEMS = (BATCH // MIX_B) * NCHUNK
_Z0, _X0, _G0, _L0, _T0, _PEND = 0, 1024, 2560, 3584, 4608, 4736
HEAD_PIECES = 3
PROJ_PIECE = 256


def _carried(c, state):
    return jnp.where(jnp.full(state.shape, c, jnp.int32) == 0, 0.0, state)


def _causal_conv(hist, c, x, w_ref, b_ref):
    prev = _carried(c, hist[...])
    hist[...] = x[CHUNK - 8:, :]
    r8 = lax.broadcasted_iota(jnp.int32, prev.shape, 0)
    acc = b_ref[...] + w_ref[CONV_WIDTH - 1:CONV_WIDTH, :] * x
    for j in range(1, CONV_WIDTH):
        shifted = pltpu.roll(x, j, 0)
        head = jnp.where(r8 < j, pltpu.roll(prev, j, 0), shifted[0:8, :])
        shifted = jnp.concatenate([head, shifted[8:, :]], axis=0)
        acc = acc + w_ref[CONV_WIDTH - 1 - j:CONV_WIDTH - j, :] * shifted
    return acc


def _expand_heads(v, exp_ref):
    lane = lax.broadcasted_iota(jnp.int32, v.shape, 1)
    rest = jnp.where(lane < SSD_HEADS, v, 0.0)
    packed = None
    for k in range(HEAD_PIECES):
        piece = rest.astype(BF16).astype(F32)
        rest = rest - piece
        moved = piece if k == 0 else pltpu.roll(piece, k * SSD_HEADS, 1)
        packed = moved if packed is None else packed + moved
    return _dot(packed.astype(BF16), exp_ref[...])


def _ssd_chunk(c, tick, z, xbc, dt_raw, hist, st, cw_ref, cb_ref, dtb_ref, alog_ref, dsk_ref, ng_ref, exp_ref):
    xc = _silu(_causal_conv(hist, c, xbc, cw_ref, cb_ref))
    tick()
    xs = xc[:, 0:SSD_WIDTH]
    valid_s = _row_ids((CHUNK, 2 * SSD_STATE), c) >= PAD
    bm = jnp.where(valid_s, xc[:, SSD_WIDTH:SSD_WIDTH + 2 * SSD_STATE], 0.0)
    cm = jnp.where(valid_s, xc[:, SSD_WIDTH + 2 * SSD_STATE:], 0.0)

    li = lax.broadcasted_iota(jnp.int32, (CHUNK, CHUNK), 0)
    si = lax.broadcasted_iota(jnp.int32, (CHUNK, CHUNK), 1)
    causal = si <= li
    dt = _softplus(dt_raw + dtb_ref[...])
    dt = jnp.where(li + c * CHUNK >= PAD, dt, 0.0)
    adt = dt * (-jnp.exp(alog_ref[...]))
    a_cs = _dot(causal.astype(F32), adt, HIGHEST)
    a_last = a_cs[CHUNK - 1:CHUNK, :]
    ea = jnp.exp(a_cs)
    dt_x = _expand_heads(dt, exp_ref)
    w_x = _expand_heads(jnp.exp(a_last - a_cs) * dt, exp_ref)
    ea_x = _expand_heads(ea, exp_ref)
    xdt = xs * dt_x
    a_cs_t = a_cs.T
    tick()
    lane = lax.broadcasted_iota(jnp.int32, (CHUNK, LANES), 1)

    ys = []
    for g in range(SSD_GROUPS):
        gsl = slice(g * SSD_GROUP_WIDTH, (g + 1) * SSD_GROUP_WIDTH)
        bg = bm[:, g * SSD_STATE:(g + 1) * SSD_STATE]
        cg16 = cm[:, g * SSD_STATE:(g + 1) * SSD_STATE].astype(BF16)
        cbm = lax.dot_general(cg16, bg.astype(BF16), (((1,), (1,)), ((), ())),
                              preferred_element_type=F32)
        s_in = _carried(c, st[:, gsl])
        y_off = _dot(cg16, s_in.astype(BF16)) * ea_x[:, gsl]
        s_new = _dot(bg.T.astype(BF16), (w_x[:, gsl] * xs[:, gsl]).astype(BF16))
        st[:, gsl] = ea_x[CHUNK - 1:CHUNK, gsl] * s_in + s_new
        for j in range(SSD_GROUP_WIDTH // LANES):
            xp = xdt[:, g * SSD_GROUP_WIDTH + j * LANES:
                     g * SSD_GROUP_WIDTH + (j + 1) * LANES].astype(BF16)
            parts = []
            for hh in range(LANES // SSD_HEAD_DIM):
                hd = g * (SSD_HEADS // SSD_GROUPS) + j * (LANES // SSD_HEAD_DIM) + hh
                seg = a_cs[:, hd:hd + 1] - a_cs_t[hd:hd + 1, :]
                dec = jnp.exp(jnp.where(causal, seg, -1e30))
                parts.append(_dot((cbm * dec).astype(BF16), xp))
            ys.append(jnp.where(lane < SSD_HEAD_DIM, parts[0], parts[1])
                      + y_off[:, j * LANES:(j + 1) * LANES])
            if j % 2:
                tick()
    y = jnp.concatenate(ys, axis=1) + dsk_ref[...] * xs
    y = y * _silu(z)
    normed = []
    for g in range(SSD_GROUPS):
        yg = y[:, g * SSD_GROUP_WIDTH:(g + 1) * SSD_GROUP_WIDTH]
        normed.append(yg * lax.rsqrt(jnp.mean(yg * yg, axis=-1, keepdims=True) + RMS_EPS))
    return jnp.concatenate(normed, axis=1) * ng_ref[...]


def _lru_chunk(c, tick, gt, lin, hist, lc, lcw_ref, lcb_ref, wa_ref, ba_ref, wx_ref, bx_ref, lam_ref):
    xb = _causal_conv(hist, c, lin, lcw_ref, lcb_ref)
    tick()
    xb16 = xb.astype(BF16)
    nblk = LRU_WIDTH // LANES
    ra = jnp.concatenate(
        [_dot(xb16[:, j * LANES:(j + 1) * LANES], wa_ref[j]) for j in range(nblk)], axis=1)
    ix = jnp.concatenate(
        [_dot(xb16[:, j * LANES:(j + 1) * LANES], wx_ref[j]) for j in range(nblk)], axis=1)
    r = jax.nn.sigmoid(ra + ba_ref[...])
    ig = jax.nn.sigmoid(ix + bx_ref[...])
    log_a = (-LRU_C * _softplus(-lam_ref[...])) * r
    a = jnp.exp(log_a)
    mult = jnp.sqrt(jnp.tanh(-log_a) * (a * a + 1.0))
    grow = _row_ids((CHUNK, LRU_WIDTH), c)
    mult = jnp.where(grow == PAD, 1.0, mult)
    u = jnp.where(grow >= PAD, mult * (ig * xb), 0.0)
    tick()

    r8 = lax.broadcasted_iota(jnp.int32, (8, LRU_WIDTH), 0)
    carry = _carried(c, lc[0:1, :])
    hs = []
    for j in range(CHUNK // 8):
        a8 = a[j * 8:(j + 1) * 8, :]
        u8 = u[j * 8:(j + 1) * 8, :]
        for d in (1, 2, 4):
            a_sh = jnp.where(r8 >= d, pltpu.roll(a8, d, 0), 1.0)
            u_sh = jnp.where(r8 >= d, pltpu.roll(u8, d, 0), 0.0)
            u8 = a8 * u_sh + u8
            a8 = a8 * a_sh
        h8 = a8 * carry + u8
        carry = h8[7:8, :]
        hs.append(h8)
        if j == CHUNK // 16:
            tick()
    lc[0:1, :] = carry
    gelu = 0.5 * gt * (1.0 + jnp.tanh(math.sqrt(2.0 / math.pi) * (gt + 0.044715 * (gt * gt * gt))))
    return jnp.concatenate(hs, axis=0) * gelu


def _mixer_kernel(x_ref, meta_ref, gin_ref, wproj_ref,
                  cw_ref, cb_ref, dtb_ref, alog_ref, dsk_ref, ng_ref,
                  lcw_ref, lcb_ref, wa_ref, ba_ref, wx_ref, bx_ref, lam_ref,
                  wout_ref, exp_ref, o_ref, h_cur, h_next, p_cur, p_next, hist_x, hist_l, st, lc):
    s = pl.program_id(0)
    p_chunk = lax.rem(jnp.minimum(s, N_MIX_ITEMS - 1), NCHUNK)
    c = lax.rem(jnp.maximum(s - 1, 0), NCHUNK)

    @pl.when(s == 0)
    def _():
        h_next[...] = jnp.zeros_like(h_next)
        p_next[...] = jnp.zeros_like(p_next)
        hist_x[...] = jnp.zeros_like(hist_x)
        hist_l[...] = jnp.zeros_like(hist_l)
        st[...] = jnp.zeros_like(st)
        lc[...] = jnp.zeros_like(lc)

    h_cur[...] = h_next[...]
    p_cur[...] = p_next[...]

    from_meta = jnp.full((CHUNK, D_MODEL), p_chunk, jnp.int32) == 0
    for b in range(MIX_B):
        h_next[b * CHUNK:(b + 1) * CHUNK, :] = jnp.where(from_meta, meta_ref[...], x_ref[b])
    hn = _rms(h_next[...], gin_ref[...]).astype(BF16)
    pieces = [(lo, min(lo + PROJ_PIECE, _PEND)) for lo in range(0, _PEND, PROJ_PIECE)]

    def tick():
        if pieces:
            lo, hi = pieces.pop(0)
            p_next[:, lo:hi] = _dot(hn, wproj_ref[:, lo:hi])

    h = h_cur[...]
    mixed = []
    tick()
    for b in range(MIX_B):
        rows = slice(b * CHUNK, (b + 1) * CHUNK)
        y_ssd = _ssd_chunk(c, tick, p_cur[rows, _Z0:_X0], p_cur[rows, _X0:_G0], p_cur[rows, _T0:_PEND],
                           hist_x.at[b], st.at[b],
                           cw_ref, cb_ref, dtb_ref, alog_ref, dsk_ref, ng_ref, exp_ref)
        y_lru = _lru_chunk(c, tick, p_cur[rows, _G0:_L0], p_cur[rows, _L0:_T0], hist_l.at[b], lc.at[b],
                           lcw_ref, lcb_ref, wa_ref, ba_ref, wx_ref, bx_ref, lam_ref)
        mixed.append(jnp.concatenate([y_ssd, y_lru], axis=1).astype(BF16))
    while pieces:
        tick()
    out = _dot(jnp.concatenate(mixed, axis=0), wout_ref[...]) + h
    keep = _row_ids((CHUNK, D_MODEL), c) >= PAD
    for b in range(MIX_B):
        o_ref[b] = jnp.where(keep, out[b * CHUNK:(b + 1) * CHUNK, :], 0.0)


def _mixer(x, meta_chunk, params):
    full = lambda a: pl.BlockSpec(a.shape, lambda s: (0,) * a.ndim, pipeline_mode=pl.Buffered(1))

    def x_block(s):
        item = jnp.minimum(s, N_MIX_ITEMS - 1)
        return item // NCHUNK, jnp.maximum(item % NCHUNK - 1, 0), 0

    def out_block(s):
        item = jnp.maximum(s - 1, 0)
        return item // NCHUNK, item % NCHUNK, 0

    return pl.pallas_call(
        _mixer_kernel,
        grid=(N_MIX_ITEMS + 1,),
        in_specs=[pl.BlockSpec((MIX_B, CHUNK, D_MODEL), x_block), full(meta_chunk)] + [full(p) for p in params],
        out_specs=pl.BlockSpec((MIX_B, CHUNK, D_MODEL), out_block),
        out_shape=jax.ShapeDtypeStruct((BATCH, LP, D_MODEL), F32),
        scratch_shapes=[
            pltpu.VMEM((MIX_B * CHUNK, D_MODEL), F32),
            pltpu.VMEM((MIX_B * CHUNK, D_MODEL), F32),
            pltpu.VMEM((MIX_B * CHUNK, _PEND), F32),
            pltpu.VMEM((MIX_B * CHUNK, _PEND), F32),
            pltpu.VMEM((MIX_B, 8, SSD_CONV_DIM), F32),
            pltpu.VMEM((MIX_B, 8, LRU_WIDTH), F32),
            pltpu.VMEM((MIX_B, SSD_STATE, SSD_WIDTH), F32),
            pltpu.VMEM((MIX_B, 8, LRU_WIDTH), F32),
        ],
        compiler_params=pltpu.CompilerParams(
            dimension_semantics=("arbitrary",), vmem_limit_bytes=VMEM_LIMIT),
        name="ssd_lru_mixer",
    )(x, meta_chunk, *params)


POOL_TM = 512


def _pool_kernel(h_ref, g_ref, pw_ref, pb_ref, ps_ref, o_ref, buf):
    base = POOL_MAX_WINDOW
    for k in range(POOL_TM // CHUNK):
        chunk = pl.program_id(0) * (POOL_TM // CHUNK) + k
        c = lax.rem(chunk, NCHUNK)

        @pl.when(c == 0)
        def _():
            buf[0:base, :] = jnp.zeros((base, D_MODEL), F32)

        h = h_ref[k * CHUNK:(k + 1) * CHUNK, :]
        hn = _rms(h, g_ref[...])
        buf[base:base + CHUNK, :] = hn
        pos = _row_ids((CHUNK, POOL_GROUP_DIM), c) - PAD
        outs = []
        for g, w in enumerate(POOL_WINDOWS):
            sl = slice(g * POOL_GROUP_DIM, (g + 1) * POOL_GROUP_DIM)
            ws = buf[:, sl]
            s = 1
            while s < w:
                ws = ws + pltpu.roll(ws, s, 0)
                s *= 2
            ws = ws[base:, :]
            count = jnp.clip(pos + 1, 1, w).astype(F32)
            pooled = ws / count - hn[:, sl]
            outs.append(_dot(pooled.astype(BF16), pw_ref[g]))
        buf[0:base, :] = buf[CHUNK:CHUNK + base, :]
        y = (jnp.concatenate(outs, axis=1) + pb_ref[...]) * ps_ref[...]
        o_ref[k * CHUNK:(k + 1) * CHUNK, :] = jnp.where(_row_ids((CHUNK, D_MODEL), c) >= PAD, h + y, 0.0)


def _pool_mixer(h, gain, pw, pb, ps):
    blk = pl.BlockSpec((POOL_TM, D_MODEL), lambda i: (i, 0))
    full = lambda a: pl.BlockSpec(a.shape, lambda i: (0,) * a.ndim)
    return pl.pallas_call(
        _pool_kernel,
        grid=(TP // POOL_TM,),
        in_specs=[blk, full(gain), full(pw), full(pb), full(ps)],
        out_specs=blk,
        out_shape=jax.ShapeDtypeStruct((TP, D_MODEL), F32),
        scratch_shapes=[pltpu.VMEM((CHUNK + POOL_MAX_WINDOW, D_MODEL), F32)],
        compiler_params=pltpu.CompilerParams(
            dimension_semantics=("arbitrary",), vmem_limit_bytes=VMEM_LIMIT),
        name="pool_mixer",
    )(h, gain, pw, pb, ps)


MOE_TM = 512
N_MOE_TILES = TP // MOE_TM
RUN_ALIGN = 16
RUN_SHIFT = 4
RUN_BIG = 64
RUN_BIG_SHIFT = 6
LOCAL_ROWS = 1280
EXPERT_TILE = 512
N_ROUTED = BATCH * (N_META + SEQ)
MAX_ROWS = (2 * N_ROUTED + N_MOE_TILES * MOE_EXPERTS * (RUN_ALIGN - 1)
            + MOE_EXPERTS * (EXPERT_TILE - 1))
NT_MAX = -(-MAX_ROWS // EXPERT_TILE)
NR = NT_MAX * EXPERT_TILE
XS_COLS = D_MODEL + LANES
INT_ROWS = 8
TOK_GATE1, TOK_GATE2, TOK_POS1, TOK_POS2, TOK_EXPERT1 = 0, 3, 6, 7, 8
assert LOCAL_ROWS >= 2 * MOE_TM + MOE_EXPERTS * (RUN_ALIGN - 1) and LOCAL_ROWS % LANES == 0


def _first_argmax(vals):
    best, idx = vals[0], jnp.zeros(vals[0].shape, jnp.int32)
    for k in range(1, len(vals)):
        better = vals[k] > best
        idx = jnp.where(better, k, idx)
        best = jnp.where(better, vals[k], best)
    return idx, best


def _softmax_rows(vals):
    m = functools.reduce(jnp.maximum, vals)
    ex = [jnp.exp(v - m) for v in vals]
    tot = functools.reduce(lambda p, q: p + q, ex)
    return [e / tot for e in ex]


def _bf16_pieces(x):
    hi = x.astype(BF16).astype(F32)
    rest = x - hi
    mid = rest.astype(BF16).astype(F32)
    lo = (rest - mid).astype(BF16).astype(F32)
    return [hi, mid, lo]


def _router_kernel(h_ref, valid_ref, g_ref, wrh_ref, wrl_ref, br_ref, lpos_ref, tok_ref, runlen_ref, runoff_ref):
    hn = _rms(h_ref[...], g_ref[...])
    hn_hi = hn.astype(BF16)
    hn_lo = (hn - hn_hi.astype(F32)).astype(BF16)
    logits = (_dot(hn_hi, wrh_ref[...]) + (_dot(hn_hi, wrl_ref[...]) + _dot(hn_lo, wrh_ref[...]))
              + br_ref[...])
    lt = logits.T
    p_group = _softmax_rows([lt[k:k + 1, :] for k in range(MOE_GROUPS)])
    g_sel, p_g = _first_argmax(p_group)
    fine = []
    for k in range(MOE_PER_GROUP):
        f = lt[ROUTER_EXPERT_ROW + k:ROUTER_EXPERT_ROW + k + 1, :]
        for g in range(1, MOE_GROUPS):
            r0 = ROUTER_EXPERT_ROW + g * MOE_PER_GROUP + k
            f = jnp.where(g_sel == g, lt[r0:r0 + 1, :], f)
        fine.append(f)
    q = _softmax_rows(fine)
    i1, t1 = _first_argmax(q)
    i2, t2 = _first_argmax([jnp.where(i1 == k, -1.0, q[k]) for k in range(MOE_PER_GROUP)])
    tot = t1 + t2
    gate1 = p_g * (t1 / tot)
    gate2 = p_g * (t2 / tot)

    valid = valid_ref[...] > 0
    e1 = jnp.where(valid, g_sel * MOE_PER_GROUP + i1, -1)
    e2 = jnp.where(valid, g_sel * MOE_PER_GROUP + i2, -1)
    erow = lax.broadcasted_iota(jnp.int32, (MOE_EXPERTS, MOE_TM), 0)
    hit1 = erow == e1
    hit2 = erow == e2
    onehot = jnp.where(hit1 | hit2, 1.0, 0.0)
    si = lax.broadcasted_iota(jnp.int32, (MOE_TM, MOE_TM), 0)
    ti = lax.broadcasted_iota(jnp.int32, (MOE_TM, MOE_TM), 1)
    before = jnp.where(si < ti, 1.0, 0.0).astype(BF16)
    seen = _dot(onehot.astype(BF16), before)
    count = jnp.sum(onehot, axis=1, keepdims=True).astype(jnp.int32)
    runlen = lax.shift_left(lax.shift_right_logical(count + (RUN_ALIGN - 1), RUN_SHIFT), RUN_SHIFT)
    runlen_b = jnp.broadcast_to(runlen, (MOE_EXPERTS, LANES))
    ei = lax.broadcasted_iota(jnp.int32, (MOE_EXPERTS, MOE_EXPERTS), 0)
    ej = lax.broadcasted_iota(jnp.int32, (MOE_EXPERTS, MOE_EXPERTS), 1)
    runoff_b = _dot(jnp.where(ej < ei, 1.0, 0.0), runlen_b.astype(F32), HIGHEST)
    runlen_ref[...] = runlen_b
    runoff_ref[...] = runoff_b.astype(jnp.int32)
    place = seen + runoff_b[:, 0:1]
    pos1 = jnp.where(valid, jnp.sum(jnp.where(hit1, place, 0.0), axis=0, keepdims=True), -1.0)
    pos2 = jnp.where(valid, jnp.sum(jnp.where(hit2, place, 0.0), axis=0, keepdims=True), -1.0)
    r8 = lax.broadcasted_iota(jnp.int32, (INT_ROWS, MOE_TM), 0)
    lpos_ref[...] = jnp.where(r8 == 0, pos1.astype(jnp.int32),
                              jnp.where(r8 == 1, pos2.astype(jnp.int32), 0))

    rows = lax.broadcasted_iota(jnp.int32, lt.shape, 0)
    table = jnp.zeros(lt.shape, F32)
    for k, piece in enumerate(_bf16_pieces(gate1)):
        table = jnp.where(rows == TOK_GATE1 + k, piece, table)
    for k, piece in enumerate(_bf16_pieces(gate2)):
        table = jnp.where(rows == TOK_GATE2 + k, piece, table)
    table = jnp.where(rows == TOK_POS1, pos1, jnp.where(rows == TOK_POS2, pos2, table))
    table = jnp.where(rows == TOK_EXPERT1, e1.astype(F32), table)
    tok_ref[...] = table.T


def _router(h, valid, gain, wr, br):
    row = lambda i: (i, 0)
    col = lambda i: (0, i)
    const = lambda i: (0, 0)
    wr_hi = wr.astype(BF16)
    wr_lo = (wr - wr_hi.astype(F32)).astype(BF16)
    return pl.pallas_call(
        _router_kernel,
        grid=(N_MOE_TILES,),
        in_specs=[
            pl.BlockSpec((MOE_TM, D_MODEL), row),
            pl.BlockSpec((1, MOE_TM), col),
            pl.BlockSpec((1, D_MODEL), const),
            pl.BlockSpec((D_MODEL, LANES), const),
            pl.BlockSpec((D_MODEL, LANES), const),
            pl.BlockSpec((1, LANES), const),
        ],
        out_specs=[pl.BlockSpec((INT_ROWS, MOE_TM), col),
                   pl.BlockSpec((MOE_TM, LANES), row),
                   pl.BlockSpec((MOE_EXPERTS, LANES), row),
                   pl.BlockSpec((MOE_EXPERTS, LANES), row)],
        out_shape=[jax.ShapeDtypeStruct((INT_ROWS, TP), jnp.int32),
                   jax.ShapeDtypeStruct((TP, LANES), F32),
                   jax.ShapeDtypeStruct((N_MOE_TILES * MOE_EXPERTS, LANES), jnp.int32),
                   jax.ShapeDtypeStruct((N_MOE_TILES * MOE_EXPERTS, LANES), jnp.int32)],
        compiler_params=pltpu.CompilerParams(
            dimension_semantics=("arbitrary",), vmem_limit_bytes=VMEM_LIMIT),
        name="moe_router",
    )(h, valid, gain, wr_hi, wr_lo, br)


def _moe_plan(runlen, runoff):
    i32 = jnp.int32
    lens = runlen[:, 0].reshape(N_MOE_TILES, MOE_EXPERTS)
    total = jnp.sum(lens, axis=0)
    padded = (total + (EXPERT_TILE - 1)) // EXPERT_TILE * EXPERT_TILE
    ends = jnp.cumsum(padded)
    base = ends - padded
    goff = base[None, :] + jnp.cumsum(lens, axis=0) - lens
    nval = ends[-1] // EXPERT_TILE
    tidx = jnp.minimum(jnp.arange(NT_MAX, dtype=i32), nval - 1)
    texp = jnp.minimum(jnp.sum(tidx[:, None] >= (ends // EXPERT_TILE)[None, :], axis=1), MOE_EXPERTS - 1)
    return dict(
        goff=goff.reshape(-1).astype(i32), lens=lens.reshape(-1).astype(i32),
        loff=runoff[:, 0].astype(i32), gap_start=(base + total).astype(i32),
        gap_len=(padded - total).astype(i32), nval=nval.reshape(1).astype(i32),
        tidx=tidx.astype(i32), texp=texp.astype(i32),
        nbig=jnp.sum(lens // RUN_BIG, axis=1).astype(i32),
        nsmall=jnp.sum(lens % RUN_BIG // RUN_ALIGN, axis=1).astype(i32))


def _aligned(x):
    return pl.multiple_of(x, RUN_ALIGN)


def _for_each_run(step, lens_ref, fn):
    def expert_body(e, carry):
        k = step * MOE_EXPERTS + e
        nbig = lax.shift_right_logical(lens_ref[k], RUN_BIG_SHIFT)
        rest = nbig * RUN_BIG

        def big_body(q, c):
            fn(k, q * RUN_BIG, RUN_BIG)
            return c

        def small_body(q, c):
            fn(k, rest + q * RUN_ALIGN, RUN_ALIGN)
            return c

        lax.fori_loop(0, nbig, big_body, 0)
        lax.fori_loop(0, lax.shift_right_logical(lens_ref[k] - rest, RUN_SHIFT), small_body, 0)
        return carry

    lax.fori_loop(0, MOE_EXPERTS, expert_body, 0)


def _wait_runs(copy, step, nbig_ref, nsmall_ref):
    def big_body(q, c):
        copy(RUN_BIG).wait()
        return c

    def small_body(q, c):
        copy(RUN_ALIGN).wait()
        return c

    lax.fori_loop(0, nbig_ref[step], big_body, 0)
    lax.fori_loop(0, nsmall_ref[step], small_body, 0)


def _dispatch_kernel(goff_ref, lens_ref, loff_ref, gaps_ref, gapl_ref, nval_ref, nbig_ref, nsmall_ref,
                     h_ref, g_ref, lpos_ref, tok_ref, xs_ref, local, zeros, sems):
    i = pl.program_id(0)
    last = pl.num_programs(0) - 1
    slot = lax.rem(i, 2)

    def run_copy(s, src, dst, rows):
        return pltpu.make_async_copy(local.at[s, pl.ds(_aligned(src), rows), :],
                                     xs_ref.at[pl.ds(_aligned(dst), rows), :], sems.at[s])

    def wait_tile(step):
        s = lax.rem(step, 2)
        _wait_runs(lambda rows: run_copy(s, 0, 0, rows), step, nbig_ref, nsmall_ref)

    @pl.when(i >= 2)
    def _():
        wait_tile(i - 2)

    hn = _rms(h_ref[...], g_ref[...]).astype(BF16)
    riota = lax.broadcasted_iota(jnp.int32, (LOCAL_ROWS, MOE_TM), 0)
    sel = jnp.where((riota == lpos_ref[0:1, :]) | (riota == lpos_ref[1:2, :]), 1.0, 0.0).astype(BF16)
    lane = lax.broadcasted_iota(jnp.int32, (MOE_TM, LANES), 1)
    extra = jnp.where((lane < TOK_POS1) | (lane == TOK_EXPERT1), tok_ref[...], 0.0).astype(BF16)
    local[slot] = _dot(sel, jnp.concatenate([hn, extra], axis=1)).astype(BF16)

    _for_each_run(i, lens_ref,
                  lambda k, q, rows: run_copy(slot, loff_ref[k] + q, goff_ref[k] + q, rows).start())

    @pl.when(i == last)
    def _():
        zeros[...] = jnp.zeros_like(zeros)
        zsem = sems.at[2]

        def gap_copy(dst):
            return pltpu.make_async_copy(zeros.at[pl.ds(0, RUN_ALIGN), :],
                                         xs_ref.at[pl.ds(_aligned(dst), RUN_ALIGN), :], zsem)

        def tile_copy(t):
            return pltpu.make_async_copy(
                zeros, xs_ref.at[pl.ds(pl.multiple_of(t * EXPERT_TILE, EXPERT_TILE), EXPERT_TILE), :], zsem)

        def expert_body(e, n):
            g = lax.shift_right_logical(gapl_ref[e], RUN_SHIFT)

            def body(q, c):
                gap_copy(gaps_ref[e] + q * RUN_ALIGN).start()
                return c

            lax.fori_loop(0, g, body, 0)
            return n + g

        ngap = lax.fori_loop(0, MOE_EXPERTS, expert_body, 0)

        def tail_start(t, c):
            tile_copy(t).start()
            return c

        lax.fori_loop(nval_ref[0], NT_MAX, tail_start, 0)

        def gap_wait(k, c):
            gap_copy(0).wait()
            return c

        lax.fori_loop(0, ngap, gap_wait, 0)

        def tail_wait(t, c):
            tile_copy(t).wait()
            return c

        lax.fori_loop(nval_ref[0], NT_MAX, tail_wait, 0)

        @pl.when(i >= 1)
        def _():
            wait_tile(i - 1)

        wait_tile(i)


def _dispatch(plan, h, gain, lpos, tok):
    row = lambda i, *_: (i, 0)
    return pl.pallas_call(
        _dispatch_kernel,
        grid_spec=pltpu.PrefetchScalarGridSpec(
            num_scalar_prefetch=8,
            grid=(N_MOE_TILES,),
            in_specs=[
                pl.BlockSpec((MOE_TM, D_MODEL), row),
                pl.BlockSpec((1, D_MODEL), lambda i, *_: (0, 0)),
                pl.BlockSpec((INT_ROWS, MOE_TM), lambda i, *_: (0, i)),
                pl.BlockSpec((MOE_TM, LANES), row),
            ],
            out_specs=pl.BlockSpec(memory_space=pl.ANY),
            scratch_shapes=[pltpu.VMEM((2, LOCAL_ROWS, XS_COLS), BF16),
                            pltpu.VMEM((EXPERT_TILE, XS_COLS), BF16),
                            pltpu.SemaphoreType.DMA((3,))],
        ),
        out_shape=jax.ShapeDtypeStruct((NR, XS_COLS), BF16),
        compiler_params=pltpu.CompilerParams(
            dimension_semantics=("arbitrary",), vmem_limit_bytes=VMEM_LIMIT),
        name="moe_dispatch",
    )(plan["goff"], plan["lens"], plan["loff"], plan["gap_start"], plan["gap_len"], plan["nval"],
      plan["nbig"], plan["nsmall"], h, gain, lpos, tok)


def _experts_kernel(tidx_ref, texp_ref, nval_ref, xs_ref, wg_ref, wu_ref, wd_ref, ys_ref, wg16, wu16, wd16):
    j = pl.program_id(0)
    active = j < nval_ref[0]
    new_expert = (j == 0) | (texp_ref[j] != texp_ref[jnp.maximum(j - 1, 0)])

    @pl.when(active & new_expert)
    def _():
        wg16[...] = wg_ref[0].astype(BF16)
        wu16[...] = wu_ref[0].astype(BF16)
        wd16[...] = wd_ref[0].astype(BF16)

    @pl.when(active)
    def _():
        x = xs_ref[:, 0:D_MODEL]
        extra = xs_ref[:, D_MODEL:XS_COLS].astype(F32)
        lane = lax.broadcasted_iota(jnp.int32, extra.shape, 1)

        def lanes(lo, hi):
            return jnp.sum(jnp.where((lane >= lo) & (lane < hi), extra, 0.0), axis=1, keepdims=True)

        first = lanes(TOK_EXPERT1, TOK_EXPERT1 + 1) == texp_ref[j].astype(F32)
        gate = jnp.where(first, lanes(TOK_GATE1, TOK_GATE2), lanes(TOK_GATE2, TOK_POS1))
        a = _dot(x, wg16[...])
        b = _dot(x, wu16[...])
        y = _dot((_silu(a) * b).astype(BF16), wd16[...])
        ys_ref[...] = (gate * y).astype(BF16)

    @pl.when(jnp.logical_not(active))
    def _():
        ys_ref[...] = jnp.zeros_like(ys_ref)


def _experts(plan, xs, layer, wg, wu, wd):
    tile = lambda j, tidx, texp, nval: (tidx[j], 0)
    expert = lambda j, tidx, texp, nval: (layer, texp[j], 0, 0)
    return pl.pallas_call(
        _experts_kernel,
        grid_spec=pltpu.PrefetchScalarGridSpec(
            num_scalar_prefetch=3,
            grid=(NT_MAX,),
            in_specs=[
                pl.BlockSpec((EXPERT_TILE, XS_COLS), tile),
                pl.BlockSpec((None, 1, D_MODEL, D_EXPERT), expert),
                pl.BlockSpec((None, 1, D_MODEL, D_EXPERT), expert),
                pl.BlockSpec((None, 1, D_EXPERT, D_MODEL), expert),
            ],
            out_specs=pl.BlockSpec((EXPERT_TILE, D_MODEL), lambda j, *_: (j, 0)),
            scratch_shapes=[pltpu.VMEM((D_MODEL, D_EXPERT), BF16),
                            pltpu.VMEM((D_MODEL, D_EXPERT), BF16),
                            pltpu.VMEM((D_EXPERT, D_MODEL), BF16)],
        ),
        out_shape=jax.ShapeDtypeStruct((NR, D_MODEL), BF16),
        compiler_params=pltpu.CompilerParams(
            dimension_semantics=("arbitrary",), vmem_limit_bytes=VMEM_LIMIT),
        name="moe_experts",
    )(plan["tidx"], plan["texp"], plan["nval"], xs, wg, wu, wd)


def _combined_rows(goff_ref, lens_ref, loff_ref, nbig_ref, nsmall_ref, h_ref, tok_ref, ys_ref, local, sems):
    i = pl.program_id(0)
    slot = lax.rem(i, 2)

    def run_copy(s, src, dst, rows):
        return pltpu.make_async_copy(ys_ref.at[pl.ds(_aligned(src), rows), :],
                                     local.at[s, pl.ds(_aligned(dst), rows), :], sems.at[s])

    def start_tile(step):
        s = lax.rem(step, 2)
        _for_each_run(step, lens_ref,
                      lambda k, q, rows: run_copy(s, goff_ref[k] + q, loff_ref[k] + q, rows).start())

    @pl.when(i == 0)
    def _():
        local[...] = jnp.zeros_like(local)
        start_tile(0)

    @pl.when(i < pl.num_programs(0) - 1)
    def _():
        start_tile(i + 1)

    _wait_runs(lambda rows: run_copy(slot, 0, 0, rows), i, nbig_ref, nsmall_ref)
    tok = tok_ref[...]
    p1 = tok[:, TOK_POS1:TOK_POS1 + 1].astype(jnp.int32)
    p2 = tok[:, TOK_POS2:TOK_POS2 + 1].astype(jnp.int32)
    ciota = lax.broadcasted_iota(jnp.int32, (MOE_TM, LOCAL_ROWS), 1)
    sel = jnp.where((ciota == p1) | (ciota == p2), 1.0, 0.0).astype(BF16)
    return h_ref[...] + _dot(sel, local[slot])


def _combine_kernel(goff_ref, lens_ref, loff_ref, nbig_ref, nsmall_ref, h_ref, tok_ref, ys_ref, o_ref,
                    local, sems):
    o_ref[...] = _combined_rows(goff_ref, lens_ref, loff_ref, nbig_ref, nsmall_ref, h_ref, tok_ref, ys_ref,
                                local, sems)


def _combine_final_kernel(goff_ref, lens_ref, loff_ref, nbig_ref, nsmall_ref, h_ref, tok_ref, g_ref, ys_ref,
                          o_ref, local, stage, sems, out_sems):
    i = pl.program_id(0)
    last = pl.num_programs(0) - 1
    slot = lax.rem(i, 2)
    per_tile = MOE_TM // CHUNK

    def for_each_seq_chunk(step, fn):
        s = lax.rem(step, 2)
        for k in range(per_tile):
            chunk = step * per_tile + k
            c = lax.rem(chunk, NCHUNK)
            dst = (lax.div(chunk, NCHUNK) * (SEQ // CHUNK) + c - 1) * CHUNK

            @pl.when(c > 0)
            def _():
                fn(pltpu.make_async_copy(stage.at[s, pl.ds(k * CHUNK, CHUNK), :],
                                         o_ref.at[pl.ds(pl.multiple_of(dst, CHUNK), CHUNK), :],
                                         out_sems.at[s]))

    @pl.when(i >= 2)
    def _():
        for_each_seq_chunk(i - 2, lambda copy: copy.wait())

    out = _combined_rows(goff_ref, lens_ref, loff_ref, nbig_ref, nsmall_ref, h_ref, tok_ref, ys_ref,
                         local, sems)
    stage[slot] = _rms(out, g_ref[...])
    for_each_seq_chunk(i, lambda copy: copy.start())

    @pl.when(i == last)
    def _():
        @pl.when(i >= 1)
        def _():
            for_each_seq_chunk(i - 1, lambda copy: copy.wait())

        for_each_seq_chunk(i, lambda copy: copy.wait())


def _combine(plan, h, tok, ys, final_gain=None):
    final = final_gain is not None
    row = lambda i, *_: (i, 0)
    in_specs = [pl.BlockSpec((MOE_TM, D_MODEL), row), pl.BlockSpec((MOE_TM, LANES), row)]
    scratch = [pltpu.VMEM((2, LOCAL_ROWS, D_MODEL), BF16)]
    args = [h, tok]
    if final:
        in_specs.append(pl.BlockSpec((1, D_MODEL), lambda i, *_: (0, 0)))
        scratch.append(pltpu.VMEM((2, MOE_TM, D_MODEL), F32))
        args.append(final_gain)
    return pl.pallas_call(
        _combine_final_kernel if final else _combine_kernel,
        grid_spec=pltpu.PrefetchScalarGridSpec(
            num_scalar_prefetch=5,
            grid=(N_MOE_TILES,),
            in_specs=in_specs + [pl.BlockSpec(memory_space=pl.ANY)],
            out_specs=(pl.BlockSpec(memory_space=pl.ANY) if final
                       else pl.BlockSpec((MOE_TM, D_MODEL), row)),
            scratch_shapes=scratch + [pltpu.SemaphoreType.DMA((2,))] * (2 if final else 1),
        ),
        out_shape=jax.ShapeDtypeStruct((BATCH * SEQ if final else TP, D_MODEL), F32),
        compiler_params=pltpu.CompilerParams(
            dimension_semantics=("arbitrary",), vmem_limit_bytes=VMEM_LIMIT),
        name="moe_combine_final" if final else "moe_combine",
    )(plan["goff"], plan["lens"], plan["loff"], plan["nbig"], plan["nsmall"], *args, ys)


def _row(v):
    return v.reshape(1, -1).astype(F32)


def _pad_lanes(v):
    return jnp.pad(_row(v), ((0, 0), (0, LANES - v.shape[-1])))


def _pair_blockdiag(w):
    w = w.reshape(LRU_WIDTH // LANES, 2, 64, 64)
    z = jnp.zeros_like(w[:, 0])
    top = jnp.concatenate([w[:, 0], z], axis=2)
    bot = jnp.concatenate([z, w[:, 1]], axis=2)
    return jnp.concatenate([top, bot], axis=1).astype(BF16)


def _moe_layer(h, layer, final_gain, valid, ffn_norm, rgw, rgb, rew, reb, wg, wu, wd):
    wr = jnp.zeros((D_MODEL, LANES), F32)
    wr = wr.at[:, 0:MOE_GROUPS].set(rgw[layer])
    wr = wr.at[:, ROUTER_EXPERT_ROW:ROUTER_EXPERT_ROW + MOE_EXPERTS].set(rew[layer])
    br = jnp.zeros((1, LANES), F32)
    br = br.at[0, 0:MOE_GROUPS].set(rgb[layer])
    br = br.at[0, ROUTER_EXPERT_ROW:ROUTER_EXPERT_ROW + MOE_EXPERTS].set(reb[layer])
    gain = _row(ffn_norm[layer])
    lpos, tok, runlen, runoff = _router(h, valid, gain, wr, br)
    plan = _moe_plan(runlen, runoff)
    xs = _dispatch(plan, h, gain, lpos, tok)
    ys = _experts(plan, xs, layer, wg, wu, wd)
    return _combine(plan, h, tok, ys, final_gain)


def kernel(x, meta_tokens, norm_final, mix_norm_even, w_in, ssd_conv_w, ssd_conv_b, ssd_dt_bias, ssd_a_log, ssd_d, ssd_norm, lru_conv_w, lru_conv_b, lru_w_a, lru_b_a, lru_w_x, lru_b_x, lru_lambda, w_out, mix_norm_odd, pool_w, pool_b, pool_scale, ffn_norm, router_group_w, router_group_b, router_expert_w, router_expert_b, expert_w_gate, expert_w_up, expert_w_down):
    meta_chunk = jnp.concatenate([jnp.zeros((PAD, D_MODEL), F32), meta_tokens.astype(F32)], axis=0)
    valid = ((jnp.arange(TP, dtype=jnp.int32) % LP) >= PAD).astype(jnp.int32).reshape(1, TP)
    moe_args = (valid, ffn_norm, router_group_w, router_group_b, router_expert_w, router_expert_b,
                expert_w_gate, expert_w_up, expert_w_down)

    wi = w_in[0]
    w_proj = jnp.concatenate(
        [wi[:, 0:1024], wi[:, 1024:2560], wi[:, 2576:3600], wi[:, 3600:4624], wi[:, 2560:2576],
         jnp.zeros((D_MODEL, LANES - SSD_HEADS), F32)], axis=1).astype(BF16)
    piece_head = jnp.where(jnp.arange(LANES) < HEAD_PIECES * SSD_HEADS, jnp.arange(LANES) % SSD_HEADS, -1)
    expand = (piece_head[:, None] == (jnp.arange(SSD_WIDTH) // SSD_HEAD_DIM)[None, :]).astype(BF16)
    params = [
        _row(mix_norm_even[0]), w_proj,
        ssd_conv_w[0].astype(F32), _row(ssd_conv_b[0]), _pad_lanes(ssd_dt_bias[0]), _pad_lanes(ssd_a_log[0]),
        _row(jnp.repeat(ssd_d[0], SSD_HEAD_DIM)), _row(ssd_norm[0]),
        lru_conv_w[0].astype(F32), _row(lru_conv_b[0]), _pair_blockdiag(lru_w_a[0]), _row(lru_b_a[0]),
        _pair_blockdiag(lru_w_x[0]), _row(lru_b_x[0]), _row(lru_lambda[0]),
        w_out[0].astype(BF16), expand,
    ]
    h = _mixer(x.astype(F32), meta_chunk, params).reshape(TP, D_MODEL)
    h = _moe_layer(h, 0, None, *moe_args)

    h = _pool_mixer(h, _row(mix_norm_odd[0]), pool_w[0].astype(BF16), _row(pool_b[0]), _row(pool_scale[0]))
    out = _moe_layer(h, 1, _row(norm_final), *moe_args)
    return out.reshape(BATCH, SEQ, D_MODEL)
```

```python
import functools
import math

import jax
import jax.numpy as jnp
from jax import lax
from jax.experimental import pallas as pl
from jax.experimental.pallas import tpu as pltpu

F32 = jnp.float32
BF16 = jnp.bfloat16
HIGHEST = lax.Precision.HIGHEST

D_MODEL = 1024
BATCH = 8
SEQ = 2048
N_META = 16
RMS_EPS = 1e-6
CONV_WIDTH = 4
CHUNK = 128
PAD = CHUNK - N_META
LP = PAD + N_META + SEQ
NCHUNK = LP // CHUNK
TP = BATCH * LP

SSD_HEADS = 16
SSD_HEAD_DIM = 64
SSD_WIDTH = 1024
SSD_GROUPS = 2
SSD_STATE = 128
SSD_GROUP_WIDTH = SSD_WIDTH // SSD_GROUPS
SSD_CONV_DIM = SSD_WIDTH + 2 * SSD_GROUPS * SSD_STATE
LRU_WIDTH = 1024
LRU_C = 8.0
LANES = 128
POOL_WINDOWS = (2, 4, 8, 16)
POOL_GROUP_DIM = 256
POOL_MAX_WINDOW = 16
MOE_GROUPS = 4
MOE_PER_GROUP = 4
MOE_EXPERTS = 16
D_EXPERT = 512
ROUTER_EXPERT_ROW = 8

VMEM_LIMIT = 56 * 1024 * 1024


def _dot(a, b, precision=None):
    return jnp.dot(a, b, preferred_element_type=F32, precision=precision)


def _rms(x, gain):
    ms = jnp.mean(x * x, axis=-1, keepdims=True)
    return x * lax.rsqrt(ms + RMS_EPS) * gain


def _silu(x):
    return x * jax.nn.sigmoid(x)


def _softplus(x):
    return jnp.maximum(x, 0.0) + jnp.log1p(jnp.exp(-jnp.abs(x)))


def _row_ids(shape, chunk):
    return lax.broadcasted_iota(jnp.int32, shape, 0) + chunk * CHUNK


MIX_B = 2
N_MIX_ITEMS = (BATCH // MIX_B) * NCHUNK
_Z0, _X0, _G0, _L0, _T0, _PEND = 0, 1024, 2560, 3584, 4608, 4736
HEAD_PIECES = 3
PROJ_PIECE = 256


def _carried(c, state):
    return jnp.where(jnp.full(state.shape, c, jnp.int32) == 0, 0.0, state)


SEG = CHUNK // 8


def _time_ids(shape, axis=0):
    p = lax.broadcasted_iota(jnp.int32, shape, axis)
    return lax.bitwise_and(p, 7) * SEG + lax.shift_right_logical(p, 3)


def _load_interleaved(slab, src):
    cols = []
    for j in range(D_MODEL // LANES):
        slab[j] = src[:, j * LANES:(j + 1) * LANES]
        cols.append(jnp.concatenate([slab[j, pl.ds(k, 8, stride=SEG), :] for k in range(SEG)], axis=0))
    return jnp.concatenate(cols, axis=1)


def _store_time_order(slab, dst, value):
    for j in range(D_MODEL // LANES):
        for k in range(SEG):
            slab[j, pl.ds(k, 8, stride=SEG), :] = value[k * 8:(k + 1) * 8, j * LANES:(j + 1) * LANES]
        dst[:, j * LANES:(j + 1) * LANES] = slab[j]


def _causal_conv(hist, c, x, w_ref, b_ref):
    taps = CONV_WIDTH - 1
    prev = _carried(c, hist[...])
    hist[...] = x[(SEG - taps) * 8:, :]
    first_segment = lax.broadcasted_iota(jnp.int32, (8, x.shape[1]), 0) == 0
    groups = [x[k * 8:(k + 1) * 8, :] for k in range(SEG)]
    wrapped = {i: jnp.where(first_segment, pltpu.roll(prev[(taps - i) * 8:(taps - i + 1) * 8, :], 1, 0),
                            pltpu.roll(groups[SEG - i], 1, 0)) for i in range(1, taps + 1)}
    out = []
    for k in range(SEG):
        acc = b_ref[...] + w_ref[taps:taps + 1, :] * groups[k]
        for j in range(1, taps + 1):
            earlier = groups[k - j] if k >= j else wrapped[j - k]
            acc = acc + w_ref[taps - j:taps - j + 1, :] * earlier
        out.append(acc)
    return jnp.concatenate(out, axis=0)


def _expand_heads(v, exp_ref):
    lane = lax.broadcasted_iota(jnp.int32, v.shape, 1)
    rest = jnp.where(lane < SSD_HEADS, v, 0.0)
    packed = None
    for k in range(HEAD_PIECES):
        piece = rest.astype(BF16).astype(F32)
        rest = rest - piece
        moved = piece if k == 0 else pltpu.roll(piece, k * SSD_HEADS, 1)
        packed = moved if packed is None else packed + moved
    return _dot(packed.astype(BF16), exp_ref[...])


def _ssd_chunk(c, z, xbc, dt_raw, hist, st, cw_ref, cb_ref, dtb_ref, alog_ref, dsk_ref, ng_ref, exp_ref):
    xc = _silu(_causal_conv(hist, c, xbc, cw_ref, cb_ref))
    yield
    xs = xc[:, 0:SSD_WIDTH]
    valid_s = _time_ids((CHUNK, 2 * SSD_STATE)) + c * CHUNK >= PAD
    bm = jnp.where(valid_s, xc[:, SSD_WIDTH:SSD_WIDTH + 2 * SSD_STATE], 0.0)
    cm = jnp.where(valid_s, xc[:, SSD_WIDTH + 2 * SSD_STATE:], 0.0)

    li = _time_ids((CHUNK, CHUNK), 0)
    si = _time_ids((CHUNK, CHUNK), 1)
    causal = si <= li
    dt = _softplus(dt_raw + dtb_ref[...])
    dt = jnp.where(li + c * CHUNK >= PAD, dt, 0.0)
    adt = dt * (-jnp.exp(alog_ref[...]))
    a_cs = _dot(causal.astype(F32), adt, HIGHEST)
    a_last = a_cs[CHUNK - 1:CHUNK, :]
    ea = jnp.exp(a_cs)
    dt_x = _expand_heads(dt, exp_ref)
    w_x = _expand_heads(jnp.exp(a_last - a_cs) * dt, exp_ref)
    ea_x = _expand_heads(ea, exp_ref)
    xdt = xs * dt_x
    a_cs_t = a_cs.T
    yield
    lane = lax.broadcasted_iota(jnp.int32, (CHUNK, LANES), 1)

    ys = []
    for g in range(SSD_GROUPS):
        gsl = slice(g * SSD_GROUP_WIDTH, (g + 1) * SSD_GROUP_WIDTH)
        bg = bm[:, g * SSD_STATE:(g + 1) * SSD_STATE]
        cg16 = cm[:, g * SSD_STATE:(g + 1) * SSD_STATE].astype(BF16)
        cbm = lax.dot_general(cg16, bg.astype(BF16), (((1,), (1,)), ((), ())),
                              preferred_element_type=F32)
        s_in = _carried(c, st[:, gsl])
        y_off = _dot(cg16, s_in.astype(BF16)) * ea_x[:, gsl]
        s_new = _dot(bg.T.astype(BF16), (w_x[:, gsl] * xs[:, gsl]).astype(BF16))
        st[:, gsl] = ea_x[CHUNK - 1:CHUNK, gsl] * s_in + s_new
        for j in range(SSD_GROUP_WIDTH // LANES):
            xp = xdt[:, g * SSD_GROUP_WIDTH + j * LANES:
                     g * SSD_GROUP_WIDTH + (j + 1) * LANES].astype(BF16)
            parts = []
            for hh in range(LANES // SSD_HEAD_DIM):
                hd = g * (SSD_HEADS // SSD_GROUPS) + j * (LANES // SSD_HEAD_DIM) + hh
                seg = a_cs[:, hd:hd + 1] - a_cs_t[hd:hd + 1, :]
                dec = jnp.exp(jnp.where(causal, seg, -1e30))
                parts.append(_dot((cbm * dec).astype(BF16), xp))
            ys.append(jnp.where(lane < SSD_HEAD_DIM, parts[0], parts[1])
                      + y_off[:, j * LANES:(j + 1) * LANES])
            if j % 2:
                yield
    y = jnp.concatenate(ys, axis=1) + dsk_ref[...] * xs
    y = y * _silu(z)
    normed = []
    for g in range(SSD_GROUPS):
        yg = y[:, g * SSD_GROUP_WIDTH:(g + 1) * SSD_GROUP_WIDTH]
        normed.append(yg * lax.rsqrt(jnp.mean(yg * yg, axis=-1, keepdims=True) + RMS_EPS))
    return jnp.concatenate(normed, axis=1) * ng_ref[...]


def _lru_chunk(c, gt, lin, hist, lc, lcw_ref, lcb_ref, wa_ref, ba_ref, wx_ref, bx_ref, lam_ref):
    xb = _causal_conv(hist, c, lin, lcw_ref, lcb_ref)
    yield
    xb16 = xb.astype(BF16)
    nblk = LRU_WIDTH // LANES
    ra = jnp.concatenate(
        [_dot(xb16[:, j * LANES:(j + 1) * LANES], wa_ref[j]) for j in range(nblk)], axis=1)
    ix = jnp.concatenate(
        [_dot(xb16[:, j * LANES:(j + 1) * LANES], wx_ref[j]) for j in range(nblk)], axis=1)
    r = jax.nn.sigmoid(ra + ba_ref[...])
    ig = jax.nn.sigmoid(ix + bx_ref[...])
    log_a = (-LRU_C * _softplus(-lam_ref[...])) * r
    a = jnp.exp(log_a)
    mult = jnp.sqrt(jnp.tanh(-log_a) * (a * a + 1.0))
    grow = _time_ids((CHUNK, LRU_WIDTH)) + c * CHUNK
    mult = jnp.where(grow == PAD, 1.0, mult)
    u = jnp.where(grow >= PAD, mult * (ig * xb), 0.0)
    yield

    local, decay = [], []
    for k in range(SEG):
        a8 = a[k * 8:(k + 1) * 8, :]
        u8 = u[k * 8:(k + 1) * 8, :]
        local.append(u8 if k == 0 else a8 * local[-1] + u8)
        decay.append(a8 if k == 0 else a8 * decay[-1])
        if k == SEG // 2:
            yield
    r8 = lax.broadcasted_iota(jnp.int32, (8, LRU_WIDTH), 0)
    end, span = local[-1], decay[-1]
    for d in (1, 2, 4):
        end_sh = jnp.where(r8 >= d, pltpu.roll(end, d, 0), 0.0)
        span_sh = jnp.where(r8 >= d, pltpu.roll(span, d, 0), 1.0)
        end = span * end_sh + end
        span = span * span_sh
    carry = _carried(c, lc[0:1, :])
    end = end + span * carry
    lc[0:1, :] = end[7:8, :]
    entering = jnp.where(r8 >= 1, pltpu.roll(end, 1, 0), carry)
    hs = [local[k] + decay[k] * entering for k in range(SEG)]
    gelu = 0.5 * gt * (1.0 + jnp.tanh(math.sqrt(2.0 / math.pi) * (gt + 0.044715 * (gt * gt * gt))))
    return jnp.concatenate(hs, axis=0) * gelu


def _mixer_kernel(x_ref, meta_ref, gin_ref, wproj_ref,
                  cw_ref, cb_ref, dtb_ref, alog_ref, dsk_ref, ng_ref,
                  lcw_ref, lcb_ref, wa_ref, ba_ref, wx_ref, bx_ref, lam_ref,
                  wout_ref, exp_ref, o_ref, h_cur, h_next, p_cur, p_next, hist_x, hist_l, st, lc,
                  slab_in, slab_out):
    s = pl.program_id(0)
    p_chunk = lax.rem(jnp.minimum(s, N_MIX_ITEMS - 1), NCHUNK)
    c = lax.rem(jnp.maximum(s - 1, 0), NCHUNK)

    @pl.when(s == 0)
    def _():
        h_next[...] = jnp.zeros_like(h_next)
        p_next[...] = jnp.zeros_like(p_next)
        hist_x[...] = jnp.zeros_like(hist_x)
        hist_l[...] = jnp.zeros_like(hist_l)
        st[...] = jnp.zeros_like(st)
        lc[...] = jnp.zeros_like(lc)

    h_cur[...] = h_next[...]
    p_cur[...] = p_next[...]

    from_meta = jnp.full((CHUNK, D_MODEL), p_chunk, jnp.int32) == 0
    for b in range(MIX_B):
        h_next[b * CHUNK:(b + 1) * CHUNK, :] = jnp.where(
            from_meta, meta_ref[...], _load_interleaved(slab_in, x_ref.at[b]))
    hn = _rms(h_next[...], gin_ref[...]).astype(BF16)
    pieces = [(lo, min(lo + PROJ_PIECE, _PEND)) for lo in range(0, _PEND, PROJ_PIECE)]

    def project_piece():
        if pieces:
            lo, hi = pieces.pop(0)
            p_next[:, lo:hi] = _dot(hn, wproj_ref[:, lo:hi])

    def mix_chunk(b):
        rows = slice(b * CHUNK, (b + 1) * CHUNK)
        y_ssd = yield from _ssd_chunk(c, p_cur[rows, _Z0:_X0], p_cur[rows, _X0:_G0], p_cur[rows, _T0:_PEND],
                                      hist_x.at[b], st.at[b],
                                      cw_ref, cb_ref, dtb_ref, alog_ref, dsk_ref, ng_ref, exp_ref)
        y_lru = yield from _lru_chunk(c, p_cur[rows, _G0:_L0], p_cur[rows, _L0:_T0], hist_l.at[b], lc.at[b],
                                      lcw_ref, lcb_ref, wa_ref, ba_ref, wx_ref, bx_ref, lam_ref)
        return jnp.concatenate([y_ssd, y_lru], axis=1).astype(BF16)

    h = h_cur[...]
    mixed = []
    project_piece()
    for b in range(MIX_B):
        stages = mix_chunk(b)
        while True:
            try:
                next(stages)
            except StopIteration as done:
                mixed.append(done.value)
                break
            project_piece()
    while pieces:
        project_piece()
    out = _dot(jnp.concatenate(mixed, axis=0), wout_ref[...]) + h
    keep = _time_ids((CHUNK, D_MODEL)) + c * CHUNK >= PAD
    for b in range(MIX_B):
        _store_time_order(slab_out, o_ref.at[b], jnp.where(keep, out[b * CHUNK:(b + 1) * CHUNK, :], 0.0))


def _mixer(x, meta_chunk, params):
    full = lambda a: pl.BlockSpec(a.shape, lambda s: (0,) * a.ndim, pipeline_mode=pl.Buffered(1))

    def x_block(s):
        item = jnp.minimum(s, N_MIX_ITEMS - 1)
        return item // NCHUNK, jnp.maximum(item % NCHUNK - 1, 0), 0

    def out_block(s):
        item = jnp.maximum(s - 1, 0)
        return item // NCHUNK, item % NCHUNK, 0

    return pl.pallas_call(
        _mixer_kernel,
        grid=(N_MIX_ITEMS + 1,),
        in_specs=[pl.BlockSpec((MIX_B, CHUNK, D_MODEL), x_block), full(meta_chunk)] + [full(p) for p in params],
        out_specs=pl.BlockSpec((MIX_B, CHUNK, D_MODEL), out_block),
        out_shape=jax.ShapeDtypeStruct((BATCH, LP, D_MODEL), F32),
        scratch_shapes=[
            pltpu.VMEM((MIX_B * CHUNK, D_MODEL), F32),
            pltpu.VMEM((MIX_B * CHUNK, D_MODEL), F32),
            pltpu.VMEM((MIX_B * CHUNK, _PEND), F32),
            pltpu.VMEM((MIX_B * CHUNK, _PEND), F32),
            pltpu.VMEM((MIX_B, 8 * (CONV_WIDTH - 1), SSD_CONV_DIM), F32),
            pltpu.VMEM((MIX_B, 8 * (CONV_WIDTH - 1), LRU_WIDTH), F32),
            pltpu.VMEM((MIX_B, SSD_STATE, SSD_WIDTH), F32),
            pltpu.VMEM((MIX_B, 8, LRU_WIDTH), F32),
            pltpu.VMEM((D_MODEL // LANES, CHUNK, LANES), F32),
            pltpu.VMEM((D_MODEL // LANES, CHUNK, LANES), F32),
        ],
        compiler_params=pltpu.CompilerParams(
            dimension_semantics=("arbitrary",), vmem_limit_bytes=VMEM_LIMIT),
        name="ssd_lru_mixer",
    )(x, meta_chunk, *params)


POOL_TM = 512


def _pool_kernel(h_ref, g_ref, pw_ref, pb_ref, ps_ref, o_ref, buf):
    base = POOL_MAX_WINDOW
    for k in range(POOL_TM // CHUNK):
        chunk = pl.program_id(0) * (POOL_TM // CHUNK) + k
        c = lax.rem(chunk, NCHUNK)

        @pl.when(c == 0)
        def _():
            buf[0:base, :] = jnp.zeros((base, D_MODEL), F32)

        h = h_ref[k * CHUNK:(k + 1) * CHUNK, :]
        hn = _rms(h, g_ref[...])
        buf[base:base + CHUNK, :] = hn
        pos = _row_ids((CHUNK, POOL_GROUP_DIM), c) - PAD
        outs = []
        for g, w in enumerate(POOL_WINDOWS):
            sl = slice(g * POOL_GROUP_DIM, (g + 1) * POOL_GROUP_DIM)
            ws = buf[:, sl]
            s = 1
            while s < w:
                ws = ws + pltpu.roll(ws, s, 0)
                s *= 2
            ws = ws[base:, :]
            count = jnp.clip(pos + 1, 1, w).astype(F32)
            pooled = ws / count - hn[:, sl]
            outs.append(_dot(pooled.astype(BF16), pw_ref[g]))
        buf[0:base, :] = buf[CHUNK:CHUNK + base, :]
        y = (jnp.concatenate(outs, axis=1) + pb_ref[...]) * ps_ref[...]
        o_ref[k * CHUNK:(k + 1) * CHUNK, :] = jnp.where(_row_ids((CHUNK, D_MODEL), c) >= PAD, h + y, 0.0)


def _pool_mixer(h, gain, pw, pb, ps):
    blk = pl.BlockSpec((POOL_TM, D_MODEL), lambda i: (i, 0))
    full = lambda a: pl.BlockSpec(a.shape, lambda i: (0,) * a.ndim)
    return pl.pallas_call(
        _pool_kernel,
        grid=(TP // POOL_TM,),
        in_specs=[blk, full(gain), full(pw), full(pb), full(ps)],
        out_specs=blk,
        out_shape=jax.ShapeDtypeStruct((TP, D_MODEL), F32),
        scratch_shapes=[pltpu.VMEM((CHUNK + POOL_MAX_WINDOW, D_MODEL), F32)],
        compiler_params=pltpu.CompilerParams(
            dimension_semantics=("arbitrary",), vmem_limit_bytes=VMEM_LIMIT),
        name="pool_mixer",
    )(h, gain, pw, pb, ps)


MOE_TM = 512
N_MOE_TILES = TP // MOE_TM
RUN_ALIGN = 16
RUN_SHIFT = 4
RUN_BIG = 64
RUN_BIG_SHIFT = 6
LOCAL_ROWS = 1280
EXPERT_TILE = 512
N_ROUTED = BATCH * (N_META + SEQ)
MAX_ROWS = (2 * N_ROUTED + N_MOE_TILES * MOE_EXPERTS * (RUN_ALIGN - 1)
            + MOE_EXPERTS * (EXPERT_TILE - 1))
NT_MAX = -(-MAX_ROWS // EXPERT_TILE)
NR = NT_MAX * EXPERT_TILE
XS_COLS = D_MODEL + LANES
INT_ROWS = 8
TOK_GATE1, TOK_GATE2, TOK_POS1, TOK_POS2, TOK_EXPERT1 = 0, 3, 6, 7, 8
assert LOCAL_ROWS >= 2 * MOE_TM + MOE_EXPERTS * (RUN_ALIGN - 1) and LOCAL_ROWS % LANES == 0


def _first_argmax(vals):
    best, idx = vals[0], jnp.zeros(vals[0].shape, jnp.int32)
    for k in range(1, len(vals)):
        better = vals[k] > best
        idx = jnp.where(better, k, idx)
        best = jnp.where(better, vals[k], best)
    return idx, best


def _softmax_rows(vals):
    m = functools.reduce(jnp.maximum, vals)
    ex = [jnp.exp(v - m) for v in vals]
    tot = functools.reduce(lambda p, q: p + q, ex)
    return [e / tot for e in ex]


def _bf16_pieces(x):
    hi = x.astype(BF16).astype(F32)
    rest = x - hi
    mid = rest.astype(BF16).astype(F32)
    lo = (rest - mid).astype(BF16).astype(F32)
    return [hi, mid, lo]


def _router_kernel(h_ref, valid_ref, g_ref, wrh_ref, wrl_ref, br_ref, lpos_ref, tok_ref, runlen_ref, runoff_ref):
    hn = _rms(h_ref[...], g_ref[...])
    hn_hi = hn.astype(BF16)
    hn_lo = (hn - hn_hi.astype(F32)).astype(BF16)
    logits = (_dot(hn_hi, wrh_ref[...]) + (_dot(hn_hi, wrl_ref[...]) + _dot(hn_lo, wrh_ref[...]))
              + br_ref[...])
    lt = logits.T
    p_group = _softmax_rows([lt[k:k + 1, :] for k in range(MOE_GROUPS)])
    g_sel, p_g = _first_argmax(p_group)
    fine = []
    for k in range(MOE_PER_GROUP):
        f = lt[ROUTER_EXPERT_ROW + k:ROUTER_EXPERT_ROW + k + 1, :]
        for g in range(1, MOE_GROUPS):
            r0 = ROUTER_EXPERT_ROW + g * MOE_PER_GROUP + k
            f = jnp.where(g_sel == g, lt[r0:r0 + 1, :], f)
        fine.append(f)
    q = _softmax_rows(fine)
    i1, t1 = _first_argmax(q)
    i2, t2 = _first_argmax([jnp.where(i1 == k, -1.0, q[k]) for k in range(MOE_PER_GROUP)])
    tot = t1 + t2
    gate1 = p_g * (t1 / tot)
    gate2 = p_g * (t2 / tot)

    valid = valid_ref[...] > 0
    e1 = jnp.where(valid, g_sel * MOE_PER_GROUP + i1, -1)
    e2 = jnp.where(valid, g_sel * MOE_PER_GROUP + i2, -1)
    erow = lax.broadcasted_iota(jnp.int32, (MOE_EXPERTS, MOE_TM), 0)
    hit1 = erow == e1
    hit2 = erow == e2
    onehot = jnp.where(hit1 | hit2, 1.0, 0.0)
    si = lax.broadcasted_iota(jnp.int32, (MOE_TM, MOE_TM), 0)
    ti = lax.broadcasted_iota(jnp.int32, (MOE_TM, MOE_TM), 1)
    before = jnp.where(si < ti, 1.0, 0.0).astype(BF16)
    seen = _dot(onehot.astype(BF16), before)
    count = jnp.sum(onehot, axis=1, keepdims=True).astype(jnp.int32)
    runlen = lax.shift_left(lax.shift_right_logical(count + (RUN_ALIGN - 1), RUN_SHIFT), RUN_SHIFT)
    runlen_b = jnp.broadcast_to(runlen, (MOE_EXPERTS, LANES))
    ei = lax.broadcasted_iota(jnp.int32, (MOE_EXPERTS, MOE_EXPERTS), 0)
    ej = lax.broadcasted_iota(jnp.int32, (MOE_EXPERTS, MOE_EXPERTS), 1)
    runoff_b = _dot(jnp.where(ej < ei, 1.0, 0.0), runlen_b.astype(F32), HIGHEST)
    runlen_ref[...] = runlen_b
    runoff_ref[...] = runoff_b.astype(jnp.int32)
    place = seen + runoff_b[:, 0:1]
    pos1 = jnp.where(valid, jnp.sum(jnp.where(hit1, place, 0.0), axis=0, keepdims=True), -1.0)
    pos2 = jnp.where(valid, jnp.sum(jnp.where(hit2, place, 0.0), axis=0, keepdims=True), -1.0)
    r8 = lax.broadcasted_iota(jnp.int32, (INT_ROWS, MOE_TM), 0)
    lpos_ref[...] = jnp.where(r8 == 0, pos1.astype(jnp.int32),
                              jnp.where(r8 == 1, pos2.astype(jnp.int32), 0))

    rows = lax.broadcasted_iota(jnp.int32, lt.shape, 0)
    table = jnp.zeros(lt.shape, F32)
    for k, piece in enumerate(_bf16_pieces(gate1)):
        table = jnp.where(rows == TOK_GATE1 + k, piece, table)
    for k, piece in enumerate(_bf16_pieces(gate2)):
        table = jnp.where(rows == TOK_GATE2 + k, piece, table)
    table = jnp.where(rows == TOK_POS1, pos1, jnp.where(rows == TOK_POS2, pos2, table))
    table = jnp.where(rows == TOK_EXPERT1, e1.astype(F32), table)
    tok_ref[...] = table.T


def _router(h, valid, gain, wr, br):
    row = lambda i: (i, 0)
    col = lambda i: (0, i)
    const = lambda i: (0, 0)
    wr_hi = wr.astype(BF16)
    wr_lo = (wr - wr_hi.astype(F32)).astype(BF16)
    return pl.pallas_call(
        _router_kernel,
        grid=(N_MOE_TILES,),
        in_specs=[
            pl.BlockSpec((MOE_TM, D_MODEL), row),
            pl.BlockSpec((1, MOE_TM), col),
            pl.BlockSpec((1, D_MODEL), const),
            pl.BlockSpec((D_MODEL, LANES), const),
            pl.BlockSpec((D_MODEL, LANES), const),
            pl.BlockSpec((1, LANES), const),
        ],
        out_specs=[pl.BlockSpec((INT_ROWS, MOE_TM), col),
                   pl.BlockSpec((MOE_TM, LANES), row),
                   pl.BlockSpec((MOE_EXPERTS, LANES), row),
                   pl.BlockSpec((MOE_EXPERTS, LANES), row)],
        out_shape=[jax.ShapeDtypeStruct((INT_ROWS, TP), jnp.int32),
                   jax.ShapeDtypeStruct((TP, LANES), F32),
                   jax.ShapeDtypeStruct((N_MOE_TILES * MOE_EXPERTS, LANES), jnp.int32),
                   jax.ShapeDtypeStruct((N_MOE_TILES * MOE_EXPERTS, LANES), jnp.int32)],
        compiler_params=pltpu.CompilerParams(
            dimension_semantics=("arbitrary",), vmem_limit_bytes=VMEM_LIMIT),
        name="moe_router",
    )(h, valid, gain, wr_hi, wr_lo, br)


def _moe_plan(runlen, runoff):
    i32 = jnp.int32
    lens = runlen[:, 0].reshape(N_MOE_TILES, MOE_EXPERTS)
    total = jnp.sum(lens, axis=0)
    padded = (total + (EXPERT_TILE - 1)) // EXPERT_TILE * EXPERT_TILE
    ends = jnp.cumsum(padded)
    base = ends - padded
    goff = base[None, :] + jnp.cumsum(lens, axis=0) - lens
    nval = ends[-1] // EXPERT_TILE
    tidx = jnp.minimum(jnp.arange(NT_MAX, dtype=i32), nval - 1)
    texp = jnp.minimum(jnp.sum(tidx[:, None] >= (ends // EXPERT_TILE)[None, :], axis=1), MOE_EXPERTS - 1)
    return dict(
        goff=goff.reshape(-1).astype(i32), lens=lens.reshape(-1).astype(i32),
        loff=runoff[:, 0].astype(i32), gap_start=(base + total).astype(i32),
        gap_len=(padded - total).astype(i32), nval=nval.reshape(1).astype(i32),
        tidx=tidx.astype(i32), texp=texp.astype(i32),
        nbig=jnp.sum(lens // RUN_BIG, axis=1).astype(i32),
        nsmall=jnp.sum(lens % RUN_BIG // RUN_ALIGN, axis=1).astype(i32))


def _aligned(x):
    return pl.multiple_of(x, RUN_ALIGN)


def _for_each_run(step, lens_ref, fn):
    def expert_body(e, carry):
        k = step * MOE_EXPERTS + e
        nbig = lax.shift_right_logical(lens_ref[k], RUN_BIG_SHIFT)
        rest = nbig * RUN_BIG

        def big_body(q, c):
            fn(k, q * RUN_BIG, RUN_BIG)
            return c

        def small_body(q, c):
            fn(k, rest + q * RUN_ALIGN, RUN_ALIGN)
            return c

        lax.fori_loop(0, nbig, big_body, 0)
        lax.fori_loop(0, lax.shift_right_logical(lens_ref[k] - rest, RUN_SHIFT), small_body, 0)
        return carry

    lax.fori_loop(0, MOE_EXPERTS, expert_body, 0)


def _wait_runs(copy, step, nbig_ref, nsmall_ref):
    def big_body(q, c):
        copy(RUN_BIG).wait()
        return c

    def small_body(q, c):
        copy(RUN_ALIGN).wait()
        return c

    lax.fori_loop(0, nbig_ref[step], big_body, 0)
    lax.fori_loop(0, nsmall_ref[step], small_body, 0)


def _dispatch_kernel(goff_ref, lens_ref, loff_ref, gaps_ref, gapl_ref, nval_ref, nbig_ref, nsmall_ref,
                     h_ref, g_ref, lpos_ref, tok_ref, xs_ref, local, zeros, sems):
    i = pl.program_id(0)
    last = pl.num_programs(0) - 1
    slot = lax.rem(i, 2)

    def run_copy(s, src, dst, rows):
        return pltpu.make_async_copy(local.at[s, pl.ds(_aligned(src), rows), :],
                                     xs_ref.at[pl.ds(_aligned(dst), rows), :], sems.at[s])

    def wait_tile(step):
        s = lax.rem(step, 2)
        _wait_runs(lambda rows: run_copy(s, 0, 0, rows), step, nbig_ref, nsmall_ref)

    @pl.when(i >= 2)
    def _():
        wait_tile(i - 2)

    hn = _rms(h_ref[...], g_ref[...]).astype(BF16)
    riota = lax.broadcasted_iota(jnp.int32, (LOCAL_ROWS, MOE_TM), 0)
    sel = jnp.where((riota == lpos_ref[0:1, :]) | (riota == lpos_ref[1:2, :]), 1.0, 0.0).astype(BF16)
    lane = lax.broadcasted_iota(jnp.int32, (MOE_TM, LANES), 1)
    extra = jnp.where((lane < TOK_POS1) | (lane == TOK_EXPERT1), tok_ref[...], 0.0).astype(BF16)
    local[slot] = _dot(sel, jnp.concatenate([hn, extra], axis=1)).astype(BF16)

    _for_each_run(i, lens_ref,
                  lambda k, q, rows: run_copy(slot, loff_ref[k] + q, goff_ref[k] + q, rows).start())

    @pl.when(i == last)
    def _():
        zeros[...] = jnp.zeros_like(zeros)
        zsem = sems.at[2]

        def gap_copy(dst):
            return pltpu.make_async_copy(zeros.at[pl.ds(0, RUN_ALIGN), :],
                                         xs_ref.at[pl.ds(_aligned(dst), RUN_ALIGN), :], zsem)

        def tile_copy(t):
            return pltpu.make_async_copy(
                zeros, xs_ref.at[pl.ds(pl.multiple_of(t * EXPERT_TILE, EXPERT_TILE), EXPERT_TILE), :], zsem)

        def expert_body(e, n):
            g = lax.shift_right_logical(gapl_ref[e], RUN_SHIFT)

            def body(q, c):
                gap_copy(gaps_ref[e] + q * RUN_ALIGN).start()
                return c

            lax.fori_loop(0, g, body, 0)
            return n + g

        ngap = lax.fori_loop(0, MOE_EXPERTS, expert_body, 0)

        def tail_start(t, c):
            tile_copy(t).start()
            return c

        lax.fori_loop(nval_ref[0], NT_MAX, tail_start, 0)

        def gap_wait(k, c):
            gap_copy(0).wait()
            return c

        lax.fori_loop(0, ngap, gap_wait, 0)

        def tail_wait(t, c):
            tile_copy(t).wait()
            return c

        lax.fori_loop(nval_ref[0], NT_MAX, tail_wait, 0)

        @pl.when(i >= 1)
        def _():
            wait_tile(i - 1)

        wait_tile(i)


def _dispatch(plan, h, gain, lpos, tok):
    row = lambda i, *_: (i, 0)
    return pl.pallas_call(
        _dispatch_kernel,
        grid_spec=pltpu.PrefetchScalarGridSpec(
            num_scalar_prefetch=8,
            grid=(N_MOE_TILES,),
            in_specs=[
                pl.BlockSpec((MOE_TM, D_MODEL), row),
                pl.BlockSpec((1, D_MODEL), lambda i, *_: (0, 0)),
                pl.BlockSpec((INT_ROWS, MOE_TM), lambda i, *_: (0, i)),
                pl.BlockSpec((MOE_TM, LANES), row),
            ],
            out_specs=pl.BlockSpec(memory_space=pl.ANY),
            scratch_shapes=[pltpu.VMEM((2, LOCAL_ROWS, XS_COLS), BF16),
                            pltpu.VMEM((EXPERT_TILE, XS_COLS), BF16),
                            pltpu.SemaphoreType.DMA((3,))],
        ),
        out_shape=jax.ShapeDtypeStruct((NR, XS_COLS), BF16),
        compiler_params=pltpu.CompilerParams(
            dimension_semantics=("arbitrary",), vmem_limit_bytes=VMEM_LIMIT),
        name="moe_dispatch",
    )(plan["goff"], plan["lens"], plan["loff"], plan["gap_start"], plan["gap_len"], plan["nval"],
      plan["nbig"], plan["nsmall"], h, gain, lpos, tok)


def _experts_kernel(tidx_ref, texp_ref, nval_ref, xs_ref, wg_ref, wu_ref, wd_ref, ys_ref, wg16, wu16, wd16):
    j = pl.program_id(0)
    active = j < nval_ref[0]
    new_expert = (j == 0) | (texp_ref[j] != texp_ref[jnp.maximum(j - 1, 0)])

    @pl.when(active & new_expert)
    def _():
        wg16[...] = wg_ref[0].astype(BF16)
        wu16[...] = wu_ref[0].astype(BF16)
        wd16[...] = wd_ref[0].astype(BF16)

    @pl.when(active)
    def _():
        x = xs_ref[:, 0:D_MODEL]
        extra = xs_ref[:, D_MODEL:XS_COLS].astype(F32)
        lane = lax.broadcasted_iota(jnp.int32, extra.shape, 1)

        def lanes(lo, hi):
            return jnp.sum(jnp.where((lane >= lo) & (lane < hi), extra, 0.0), axis=1, keepdims=True)

        first = lanes(TOK_EXPERT1, TOK_EXPERT1 + 1) == texp_ref[j].astype(F32)
        gate = jnp.where(first, lanes(TOK_GATE1, TOK_GATE2), lanes(TOK_GATE2, TOK_POS1))
        a = _dot(x, wg16[...])
        b = _dot(x, wu16[...])
        y = _dot((_silu(a) * b).astype(BF16), wd16[...])
        ys_ref[...] = (gate * y).astype(BF16)

    @pl.when(jnp.logical_not(active))
    def _():
        ys_ref[...] = jnp.zeros_like(ys_ref)


def _experts(plan, xs, layer, wg, wu, wd):
    tile = lambda j, tidx, texp, nval: (tidx[j], 0)
    expert = lambda j, tidx, texp, nval: (layer, texp[j], 0, 0)
    return pl.pallas_call(
        _experts_kernel,
        grid_spec=pltpu.PrefetchScalarGridSpec(
            num_scalar_prefetch=3,
            grid=(NT_MAX,),
            in_specs=[
                pl.BlockSpec((EXPERT_TILE, XS_COLS), tile),
                pl.BlockSpec((None, 1, D_MODEL, D_EXPERT), expert),
                pl.BlockSpec((None, 1, D_MODEL, D_EXPERT), expert),
                pl.BlockSpec((None, 1, D_EXPERT, D_MODEL), expert),
            ],
            out_specs=pl.BlockSpec((EXPERT_TILE, D_MODEL), lambda j, *_: (j, 0)),
            scratch_shapes=[pltpu.VMEM((D_MODEL, D_EXPERT), BF16),
                            pltpu.VMEM((D_MODEL, D_EXPERT), BF16),
                            pltpu.VMEM((D_EXPERT, D_MODEL), BF16)],
        ),
        out_shape=jax.ShapeDtypeStruct((NR, D_MODEL), BF16),
        compiler_params=pltpu.CompilerParams(
            dimension_semantics=("arbitrary",), vmem_limit_bytes=VMEM_LIMIT),
        name="moe_experts",
    )(plan["tidx"], plan["texp"], plan["nval"], xs, wg, wu, wd)


def _combined_rows(goff_ref, lens_ref, loff_ref, nbig_ref, nsmall_ref, h_ref, tok_ref, ys_ref, local, sems):
    i = pl.program_id(0)
    slot = lax.rem(i, 2)

    def run_copy(s, src, dst, rows):
        return pltpu.make_async_copy(ys_ref.at[pl.ds(_aligned(src), rows), :],
                                     local.at[s, pl.ds(_aligned(dst), rows), :], sems.at[s])

    def start_tile(step):
        s = lax.rem(step, 2)
        _for_each_run(step, lens_ref,
                      lambda k, q, rows: run_copy(s, goff_ref[k] + q, loff_ref[k] + q, rows).start())

    @pl.when(i == 0)
    def _():
        local[...] = jnp.zeros_like(local)
        start_tile(0)

    @pl.when(i < pl.num_programs(0) - 1)
    def _():
        start_tile(i + 1)

    _wait_runs(lambda rows: run_copy(slot, 0, 0, rows), i, nbig_ref, nsmall_ref)
    tok = tok_ref[...]
    p1 = tok[:, TOK_POS1:TOK_POS1 + 1].astype(jnp.int32)
    p2 = tok[:, TOK_POS2:TOK_POS2 + 1].astype(jnp.int32)
    ciota = lax.broadcasted_iota(jnp.int32, (MOE_TM, LOCAL_ROWS), 1)
    sel = jnp.where((ciota == p1) | (ciota == p2), 1.0, 0.0).astype(BF16)
    return h_ref[...] + _dot(sel, local[slot])


def _combine_kernel(goff_ref, lens_ref, loff_ref, nbig_ref, nsmall_ref, h_ref, tok_ref, ys_ref, o_ref,
                    local, sems):
    o_ref[...] = _combined_rows(goff_ref, lens_ref, loff_ref, nbig_ref, nsmall_ref, h_ref, tok_ref, ys_ref,
                                local, sems)


def _combine_final_kernel(goff_ref, lens_ref, loff_ref, nbig_ref, nsmall_ref, h_ref, tok_ref, g_ref, ys_ref,
                          o_ref, local, stage, sems, out_sems):
    i = pl.program_id(0)
    last = pl.num_programs(0) - 1
    slot = lax.rem(i, 2)
    per_tile = MOE_TM // CHUNK

    def for_each_seq_chunk(step, fn):
        s = lax.rem(step, 2)
        for k in range(per_tile):
            chunk = step * per_tile + k
            c = lax.rem(chunk, NCHUNK)
            dst = (lax.div(chunk, NCHUNK) * (SEQ // CHUNK) + c - 1) * CHUNK

            @pl.when(c > 0)
            def _():
                fn(pltpu.make_async_copy(stage.at[s, pl.ds(k * CHUNK, CHUNK), :],
                                         o_ref.at[pl.ds(pl.multiple_of(dst, CHUNK), CHUNK), :],
                                         out_sems.at[s]))

    @pl.when(i >= 2)
    def _():
        for_each_seq_chunk(i - 2, lambda copy: copy.wait())

    out = _combined_rows(goff_ref, lens_ref, loff_ref, nbig_ref, nsmall_ref, h_ref, tok_ref, ys_ref,
                         local, sems)
    stage[slot] = _rms(out, g_ref[...])
    for_each_seq_chunk(i, lambda copy: copy.start())

    @pl.when(i == last)
    def _():
        @pl.when(i >= 1)
        def _():
            for_each_seq_chunk(i - 1, lambda copy: copy.wait())

        for_each_seq_chunk(i, lambda copy: copy.wait())


def _combine(plan, h, tok, ys, final_gain=None):
    final = final_gain is not None
    row = lambda i, *_: (i, 0)
    in_specs = [pl.BlockSpec((MOE_TM, D_MODEL), row), pl.BlockSpec((MOE_TM, LANES), row)]
    scratch = [pltpu.VMEM((2, LOCAL_ROWS, D_MODEL), BF16)]
    args = [h, tok]
    if final:
        in_specs.append(pl.BlockSpec((1, D_MODEL), lambda i, *_: (0, 0)))
        scratch.append(pltpu.VMEM((2, MOE_TM, D_MODEL), F32))
        args.append(final_gain)
    return pl.pallas_call(
        _combine_final_kernel if final else _combine_kernel,
        grid_spec=pltpu.PrefetchScalarGridSpec(
            num_scalar_prefetch=5,
            grid=(N_MOE_TILES,),
            in_specs=in_specs + [pl.BlockSpec(memory_space=pl.ANY)],
            out_specs=(pl.BlockSpec(memory_space=pl.ANY) if final
                       else pl.BlockSpec((MOE_TM, D_MODEL), row)),
            scratch_shapes=scratch + [pltpu.SemaphoreType.DMA((2,))] * (2 if final else 1),
        ),
        out_shape=jax.ShapeDtypeStruct((BATCH * SEQ if final else TP, D_MODEL), F32),
        compiler_params=pltpu.CompilerParams(
            dimension_semantics=("arbitrary",), vmem_limit_bytes=VMEM_LIMIT),
        name="moe_combine_final" if final else "moe_combine",
    )(plan["goff"], plan["lens"], plan["loff"], plan["nbig"], plan["nsmall"], *args, ys)


def _row(v):
    return v.reshape(1, -1).astype(F32)


def _pad_lanes(v):
    return jnp.pad(_row(v), ((0, 0), (0, LANES - v.shape[-1])))


def _pair_blockdiag(w):
    w = w.reshape(LRU_WIDTH // LANES, 2, 64, 64)
    z = jnp.zeros_like(w[:, 0])
    top = jnp.concatenate([w[:, 0], z], axis=2)
    bot = jnp.concatenate([z, w[:, 1]], axis=2)
    return jnp.concatenate([top, bot], axis=1).astype(BF16)


def _moe_layer(h, layer, final_gain, valid, ffn_norm, rgw, rgb, rew, reb, wg, wu, wd):
    wr = jnp.zeros((D_MODEL, LANES), F32)
    wr = wr.at[:, 0:MOE_GROUPS].set(rgw[layer])
    wr = wr.at[:, ROUTER_EXPERT_ROW:ROUTER_EXPERT_ROW + MOE_EXPERTS].set(rew[layer])
    br = jnp.zeros((1, LANES), F32)
    br = br.at[0, 0:MOE_GROUPS].set(rgb[layer])
    br = br.at[0, ROUTER_EXPERT_ROW:ROUTER_EXPERT_ROW + MOE_EXPERTS].set(reb[layer])
    gain = _row(ffn_norm[layer])
    lpos, tok, runlen, runoff = _router(h, valid, gain, wr, br)
    plan = _moe_plan(runlen, runoff)
    xs = _dispatch(plan, h, gain, lpos, tok)
    ys = _experts(plan, xs, layer, wg, wu, wd)
    return _combine(plan, h, tok, ys, final_gain)


def kernel(x, meta_tokens, norm_final, mix_norm_even, w_in, ssd_conv_w, ssd_conv_b, ssd_dt_bias, ssd_a_log, ssd_d, ssd_norm, lru_conv_w, lru_conv_b, lru_w_a, lru_b_a, lru_w_x, lru_b_x, lru_lambda, w_out, mix_norm_odd, pool_w, pool_b, pool_scale, ffn_norm, router_group_w, router_group_b, router_expert_w, router_expert_b, expert_w_gate, expert_w_up, expert_w_down):
    meta_chunk = jnp.concatenate([jnp.zeros((PAD, D_MODEL), F32), meta_tokens.astype(F32)], axis=0)
    rows = jnp.arange(CHUNK)
    meta_chunk = meta_chunk[(rows % 8) * SEG + rows // 8]
    valid = ((jnp.arange(TP, dtype=jnp.int32) % LP) >= PAD).astype(jnp.int32).reshape(1, TP)
    moe_args = (valid, ffn_norm, router_group_w, router_group_b, router_expert_w, router_expert_b,
                expert_w_gate, expert_w_up, expert_w_down)

    wi = w_in[0]
    w_proj = jnp.concatenate(
        [wi[:, 0:1024], wi[:, 1024:2560], wi[:, 2576:3600], wi[:, 3600:4624], wi[:, 2560:2576],
         jnp.zeros((D_MODEL, LANES - SSD_HEADS), F32)], axis=1).astype(BF16)
    piece_head = jnp.where(jnp.arange(LANES) < HEAD_PIECES * SSD_HEADS, jnp.arange(LANES) % SSD_HEADS, -1)
    expand = (piece_head[:, None] == (jnp.arange(SSD_WIDTH) // SSD_HEAD_DIM)[None, :]).astype(BF16)
    params = [
        _row(mix_norm_even[0]), w_proj,
        ssd_conv_w[0].astype(F32), _row(ssd_conv_b[0]), _pad_lanes(ssd_dt_bias[0]), _pad_lanes(ssd_a_log[0]),
        _row(jnp.repeat(ssd_d[0], SSD_HEAD_DIM)), _row(ssd_norm[0]),
        lru_conv_w[0].astype(F32), _row(lru_conv_b[0]), _pair_blockdiag(lru_w_a[0]), _row(lru_b_a[0]),
        _pair_blockdiag(lru_w_x[0]), _row(lru_b_x[0]), _row(lru_lambda[0]),
        w_out[0].astype(BF16), expand,
    ]
    h = _mixer(x.astype(F32), meta_chunk, params).reshape(TP, D_MODEL)
    h = _moe_layer(h, 0, None, *moe_args)

    h = _pool_mixer(h, _row(mix_norm_odd[0]), pool_w[0].astype(BF16), _row(pool_b[0]), _row(pool_scale[0]))
    out = _moe_layer(h, 1, _row(norm_final), *moe_args)
    return out.reshape(BATCH, SEQ, D_MODEL)
```

```python
import functools
import math

import jax
import jax.numpy as jnp
from jax import lax
from jax.experimental import pallas as pl
from jax.experimental.pallas import tpu as pltpu

F32 = jnp.float32
BF16 = jnp.bfloat16
HIGHEST = lax.Precision.HIGHEST

D_MODEL = 1024
BATCH = 8
SEQ = 2048
N_META = 16
RMS_EPS = 1e-6
CONV_WIDTH = 4
CHUNK = 128
PAD = CHUNK - N_META
LP = PAD + N_META + SEQ
NCHUNK = LP // CHUNK
TP = BATCH * LP

SSD_HEADS = 16
SSD_HEAD_DIM = 64
SSD_WIDTH = 1024
SSD_GROUPS = 2
SSD_STATE = 128
SSD_GROUP_WIDTH = SSD_WIDTH // SSD_GROUPS
SSD_CONV_DIM = SSD_WIDTH + 2 * SSD_GROUPS * SSD_STATE
LRU_WIDTH = 1024
LRU_C = 8.0
LANES = 128
POOL_WINDOWS = (2, 4, 8, 16)
POOL_GROUP_DIM = 256
POOL_MAX_WINDOW = 16
MOE_GROUPS = 4
MOE_PER_GROUP = 4
MOE_EXPERTS = 16
D_EXPERT = 512
ROUTER_EXPERT_ROW = 8

VMEM_LIMIT = 56 * 1024 * 1024


def _dot(a, b, precision=None):
    return jnp.dot(a, b, preferred_element_type=F32, precision=precision)


def _rms(x, gain):
    ms = jnp.mean(x * x, axis=-1, keepdims=True)
    return x * lax.rsqrt(ms + RMS_EPS) * gain


def _silu(x):
    return x * jax.nn.sigmoid(x)


def _softplus(x):
    return jnp.maximum(x, 0.0) + jnp.log1p(jnp.exp(-jnp.abs(x)))


def _row_ids(shape, chunk):
    return lax.broadcasted_iota(jnp.int32, shape, 0) + chunk * CHUNK


MIX_B = 2
N_MIX_ITEMS = (BATCH // MIX_B) * NCHUNK
_Z0, _X0, _G0, _L0, _T0, _PEND = 0, 1024, 2560, 3584, 4608, 4736
HEAD_PIECES = 3
PROJ_PIECE = 256


def _carried(c, state):
    return jnp.where(jnp.full(state.shape, c, jnp.int32) == 0, 0.0, state)


SEG = CHUNK // 8


def _time_ids(shape, axis=0):
    p = lax.broadcasted_iota(jnp.int32, shape, axis)
    return lax.bitwise_and(p, 7) * SEG + lax.shift_right_logical(p, 3)


def _load_interleaved(slab, src):
    cols = []
    for j in range(D_MODEL // LANES):
        slab[j] = src[:, j * LANES:(j + 1) * LANES]
        cols.append(jnp.concatenate([slab[j, pl.ds(k, 8, stride=SEG), :] for k in range(SEG)], axis=0))
    return jnp.concatenate(cols, axis=1)


def _store_time_order(slab, dst, value):
    for j in range(D_MODEL // LANES):
        for k in range(SEG):
            slab[j, pl.ds(k, 8, stride=SEG), :] = value[k * 8:(k + 1) * 8, j * LANES:(j + 1) * LANES]
        dst[:, j * LANES:(j + 1) * LANES] = slab[j]


def _causal_conv(hist, c, x, w_ref, b_ref):
    taps = CONV_WIDTH - 1
    prev = _carried(c, hist[...])
    hist[...] = x[(SEG - taps) * 8:, :]
    first_segment = lax.broadcasted_iota(jnp.int32, (8, x.shape[1]), 0) == 0
    groups = [x[k * 8:(k + 1) * 8, :] for k in range(SEG)]
    wrapped = {i: jnp.where(first_segment, pltpu.roll(prev[(taps - i) * 8:(taps - i + 1) * 8, :], 1, 0),
                            pltpu.roll(groups[SEG - i], 1, 0)) for i in range(1, taps + 1)}
    out = []
    for k in range(SEG):
        acc = b_ref[...] + w_ref[taps:taps + 1, :] * groups[k]
        for j in range(1, taps + 1):
            earlier = groups[k - j] if k >= j else wrapped[j - k]
            acc = acc + w_ref[taps - j:taps - j + 1, :] * earlier
        out.append(acc)
    return jnp.concatenate(out, axis=0)


def _expand_heads(v, exp_ref):
    lane = lax.broadcasted_iota(jnp.int32, v.shape, 1)
    rest = jnp.where(lane < SSD_HEADS, v, 0.0)
    packed = None
    for k in range(HEAD_PIECES):
        piece = rest.astype(BF16).astype(F32)
        rest = rest - piece
        moved = piece if k == 0 else pltpu.roll(piece, k * SSD_HEADS, 1)
        packed = moved if packed is None else packed + moved
    return _dot(packed.astype(BF16), exp_ref[...])


def _ssd_chunk(c, z, xbc, dt_raw, hist, st, cw_ref, cb_ref, dtb_ref, alog_ref, dsk_ref, ng_ref, exp_ref):
    xc = _silu(_causal_conv(hist, c, xbc, cw_ref, cb_ref))
    yield
    xs = xc[:, 0:SSD_WIDTH]
    valid_s = _time_ids((CHUNK, 2 * SSD_STATE)) + c * CHUNK >= PAD
    bm = jnp.where(valid_s, xc[:, SSD_WIDTH:SSD_WIDTH + 2 * SSD_STATE], 0.0)
    cm = jnp.where(valid_s, xc[:, SSD_WIDTH + 2 * SSD_STATE:], 0.0)

    li = _time_ids((CHUNK, CHUNK), 0)
    si = _time_ids((CHUNK, CHUNK), 1)
    causal = si <= li
    dt = _softplus(dt_raw + dtb_ref[...])
    dt = jnp.where(li + c * CHUNK >= PAD, dt, 0.0)
    adt = dt * (-jnp.exp(alog_ref[...]))
    a_cs = _dot(causal.astype(F32), adt, HIGHEST)
    a_last = a_cs[CHUNK - 1:CHUNK, :]
    ea = jnp.exp(a_cs)
    dt_x = _expand_heads(dt, exp_ref)
    w_x = _expand_heads(jnp.exp(a_last - a_cs) * dt, exp_ref)
    ea_x = _expand_heads(ea, exp_ref)
    xdt = xs * dt_x
    a_cs_t = a_cs.T
    yield
    lane = lax.broadcasted_iota(jnp.int32, (CHUNK, LANES), 1)

    ys = []
    for g in range(SSD_GROUPS):
        gsl = slice(g * SSD_GROUP_WIDTH, (g + 1) * SSD_GROUP_WIDTH)
        bg = bm[:, g * SSD_STATE:(g + 1) * SSD_STATE]
        cg16 = cm[:, g * SSD_STATE:(g + 1) * SSD_STATE].astype(BF16)
        cbm = lax.dot_general(cg16, bg.astype(BF16), (((1,), (1,)), ((), ())),
                              preferred_element_type=F32)
        s_in = _carried(c, st[:, gsl])
        y_off = _dot(cg16, s_in.astype(BF16)) * ea_x[:, gsl]
        s_new = _dot(bg.T.astype(BF16), (w_x[:, gsl] * xs[:, gsl]).astype(BF16))
        st[:, gsl] = ea_x[CHUNK - 1:CHUNK, gsl] * s_in + s_new
        for j in range(SSD_GROUP_WIDTH // LANES):
            xp = xdt[:, g * SSD_GROUP_WIDTH + j * LANES:
                     g * SSD_GROUP_WIDTH + (j + 1) * LANES].astype(BF16)
            parts = []
            for hh in range(LANES // SSD_HEAD_DIM):
                hd = g * (SSD_HEADS // SSD_GROUPS) + j * (LANES // SSD_HEAD_DIM) + hh
                seg = a_cs[:, hd:hd + 1] - a_cs_t[hd:hd + 1, :]
                dec = jnp.exp(jnp.where(causal, seg, -1e30))
                parts.append(_dot((cbm * dec).astype(BF16), xp))
            ys.append(jnp.where(lane < SSD_HEAD_DIM, parts[0], parts[1])
                      + y_off[:, j * LANES:(j + 1) * LANES])
            if j % 2:
                yield
    y = jnp.concatenate(ys, axis=1) + dsk_ref[...] * xs
    y = y * _silu(z)
    normed = []
    for g in range(SSD_GROUPS):
        yg = y[:, g * SSD_GROUP_WIDTH:(g + 1) * SSD_GROUP_WIDTH]
        normed.append(yg * lax.rsqrt(jnp.mean(yg * yg, axis=-1, keepdims=True) + RMS_EPS))
    return jnp.concatenate(normed, axis=1) * ng_ref[...]


def _lru_chunk(c, gt, lin, hist, lc, lcw_ref, lcb_ref, wa_ref, ba_ref, wx_ref, bx_ref, lam_ref):
    xb = _causal_conv(hist, c, lin, lcw_ref, lcb_ref)
    yield
    xb16 = xb.astype(BF16)
    nblk = LRU_WIDTH // LANES
    ra = jnp.concatenate(
        [_dot(xb16[:, j * LANES:(j + 1) * LANES], wa_ref[j]) for j in range(nblk)], axis=1)
    ix = jnp.concatenate(
        [_dot(xb16[:, j * LANES:(j + 1) * LANES], wx_ref[j]) for j in range(nblk)], axis=1)
    r = jax.nn.sigmoid(ra + ba_ref[...])
    ig = jax.nn.sigmoid(ix + bx_ref[...])
    log_a = (-LRU_C * _softplus(-lam_ref[...])) * r
    a = jnp.exp(log_a)
    mult = jnp.sqrt(jnp.tanh(-log_a) * (a * a + 1.0))
    grow = _time_ids((CHUNK, LRU_WIDTH)) + c * CHUNK
    mult = jnp.where(grow == PAD, 1.0, mult)
    u = jnp.where(grow >= PAD, mult * (ig * xb), 0.0)
    yield

    local, decay = [], []
    for k in range(SEG):
        a8 = a[k * 8:(k + 1) * 8, :]
        u8 = u[k * 8:(k + 1) * 8, :]
        local.append(u8 if k == 0 else a8 * local[-1] + u8)
        decay.append(a8 if k == 0 else a8 * decay[-1])
        if k == SEG // 2:
            yield
    r8 = lax.broadcasted_iota(jnp.int32, (8, LRU_WIDTH), 0)
    end, span = local[-1], decay[-1]
    for d in (1, 2, 4):
        end_sh = jnp.where(r8 >= d, pltpu.roll(end, d, 0), 0.0)
        span_sh = jnp.where(r8 >= d, pltpu.roll(span, d, 0), 1.0)
        end = span * end_sh + end
        span = span * span_sh
    carry = _carried(c, lc[0:1, :])
    end = end + span * carry
    lc[0:1, :] = end[7:8, :]
    entering = jnp.where(r8 >= 1, pltpu.roll(end, 1, 0), carry)
    hs = [local[k] + decay[k] * entering for k in range(SEG)]
    gelu = 0.5 * gt * (1.0 + jnp.tanh(math.sqrt(2.0 / math.pi) * (gt + 0.044715 * (gt * gt * gt))))
    return jnp.concatenate(hs, axis=0) * gelu


def _mixer_kernel(x_ref, meta_ref, gin_ref, wproj_ref,
                  cw_ref, cb_ref, dtb_ref, alog_ref, dsk_ref, ng_ref,
                  lcw_ref, lcb_ref, wa_ref, ba_ref, wx_ref, bx_ref, lam_ref,
                  wout_ref, exp_ref, o_ref, h_cur, h_next, p_cur, p_next, hist_x, hist_l, st, lc,
                  slab_in, slab_out):
    s = pl.program_id(0)
    p_chunk = lax.rem(jnp.minimum(s, N_MIX_ITEMS - 1), NCHUNK)
    c = lax.rem(jnp.maximum(s - 1, 0), NCHUNK)

    @pl.when(s == 0)
    def _():
        h_next[...] = jnp.zeros_like(h_next)
        p_next[...] = jnp.zeros_like(p_next)
        hist_x[...] = jnp.zeros_like(hist_x)
        hist_l[...] = jnp.zeros_like(hist_l)
        st[...] = jnp.zeros_like(st)
        lc[...] = jnp.zeros_like(lc)

    h_cur[...] = h_next[...]
    p_cur[...] = p_next[...]

    from_meta = jnp.full((CHUNK, D_MODEL), p_chunk, jnp.int32) == 0
    for b in range(MIX_B):
        h_next[b * CHUNK:(b + 1) * CHUNK, :] = jnp.where(
            from_meta, meta_ref[...], _load_interleaved(slab_in, x_ref.at[b]))
    hn = _rms(h_next[...], gin_ref[...]).astype(BF16)
    pieces = [(lo, min(lo + PROJ_PIECE, _PEND)) for lo in range(0, _PEND, PROJ_PIECE)]

    def project_piece():
        if pieces:
            lo, hi = pieces.pop(0)
            p_next[:, lo:hi] = _dot(hn, wproj_ref[:, lo:hi])

    def mix_chunk(b):
        rows = slice(b * CHUNK, (b + 1) * CHUNK)
        y_ssd = yield from _ssd_chunk(c, p_cur[rows, _Z0:_X0], p_cur[rows, _X0:_G0], p_cur[rows, _T0:_PEND],
                                      hist_x.at[b], st.at[b],
                                      cw_ref, cb_ref, dtb_ref, alog_ref, dsk_ref, ng_ref, exp_ref)
        y_lru = yield from _lru_chunk(c, p_cur[rows, _G0:_L0], p_cur[rows, _L0:_T0], hist_l.at[b], lc.at[b],
                                      lcw_ref, lcb_ref, wa_ref, ba_ref, wx_ref, bx_ref, lam_ref)
        return jnp.concatenate([y_ssd, y_lru], axis=1).astype(BF16)

    h = h_cur[...]
    mixed = []
    project_piece()
    for b in range(MIX_B):
        stages = mix_chunk(b)
        while True:
            try:
                next(stages)
            except StopIteration as done:
                mixed.append(done.value)
                break
            project_piece()
    while pieces:
        project_piece()
    out = _dot(jnp.concatenate(mixed, axis=0), wout_ref[...]) + h
    keep = _time_ids((CHUNK, D_MODEL)) + c * CHUNK >= PAD
    for b in range(MIX_B):
        _store_time_order(slab_out, o_ref.at[b], jnp.where(keep, out[b * CHUNK:(b + 1) * CHUNK, :], 0.0))


def _mixer(x, meta_chunk, params):
    full = lambda a: pl.BlockSpec(a.shape, lambda s: (0,) * a.ndim, pipeline_mode=pl.Buffered(1))

    def x_block(s):
        item = jnp.minimum(s, N_MIX_ITEMS - 1)
        return item // NCHUNK, jnp.maximum(item % NCHUNK - 1, 0), 0

    def out_block(s):
        item = jnp.maximum(s - 1, 0)
        return item // NCHUNK, item % NCHUNK, 0

    return pl.pallas_call(
        _mixer_kernel,
        grid=(N_MIX_ITEMS + 1,),
        in_specs=[pl.BlockSpec((MIX_B, CHUNK, D_MODEL), x_block), full(meta_chunk)] + [full(p) for p in params],
        out_specs=pl.BlockSpec((MIX_B, CHUNK, D_MODEL), out_block),
        out_shape=jax.ShapeDtypeStruct((BATCH, LP, D_MODEL), F32),
        scratch_shapes=[
            pltpu.VMEM((MIX_B * CHUNK, D_MODEL), F32),
            pltpu.VMEM((MIX_B * CHUNK, D_MODEL), F32),
            pltpu.VMEM((MIX_B * CHUNK, _PEND), F32),
            pltpu.VMEM((MIX_B * CHUNK, _PEND), F32),
            pltpu.VMEM((MIX_B, 8 * (CONV_WIDTH - 1), SSD_CONV_DIM), F32),
            pltpu.VMEM((MIX_B, 8 * (CONV_WIDTH - 1), LRU_WIDTH), F32),
            pltpu.VMEM((MIX_B, SSD_STATE, SSD_WIDTH), F32),
            pltpu.VMEM((MIX_B, 8, LRU_WIDTH), F32),
            pltpu.VMEM((D_MODEL // LANES, CHUNK, LANES), F32),
            pltpu.VMEM((D_MODEL // LANES, CHUNK, LANES), F32),
        ],
        compiler_params=pltpu.CompilerParams(
            dimension_semantics=("arbitrary",), vmem_limit_bytes=VMEM_LIMIT),
        name="ssd_lru_mixer",
    )(x, meta_chunk, *params)


def _pool_rows(step, h_ref, g_ref, pw_ref, pb_ref, ps_ref, o_ref, buf):
    base = POOL_MAX_WINDOW
    for k in range(h_ref.shape[0] // CHUNK):
        chunk = step * (h_ref.shape[0] // CHUNK) + k
        c = lax.rem(chunk, NCHUNK)

        @pl.when(c == 0)
        def _():
            buf[0:base, :] = jnp.zeros((base, D_MODEL), F32)

        h = h_ref[k * CHUNK:(k + 1) * CHUNK, :]
        hn = _rms(h, g_ref[...])
        buf[base:base + CHUNK, :] = hn
        pos = _row_ids((CHUNK, POOL_GROUP_DIM), c) - PAD
        outs = []
        for g, w in enumerate(POOL_WINDOWS):
            sl = slice(g * POOL_GROUP_DIM, (g + 1) * POOL_GROUP_DIM)
            ws = buf[:, sl]
            s = 1
            while s < w:
                ws = ws + pltpu.roll(ws, s, 0)
                s *= 2
            ws = ws[base:, :]
            count = jnp.clip(pos + 1, 1, w).astype(F32)
            pooled = ws / count - hn[:, sl]
            outs.append(_dot(pooled.astype(BF16), pw_ref[g]))
        buf[0:base, :] = buf[CHUNK:CHUNK + base, :]
        y = (jnp.concatenate(outs, axis=1) + pb_ref[...]) * ps_ref[...]
        o_ref[k * CHUNK:(k + 1) * CHUNK, :] = jnp.where(_row_ids((CHUNK, D_MODEL), c) >= PAD, h + y, 0.0)


MOE_TM = 512
N_MOE_TILES = TP // MOE_TM
RUN_ALIGN = 16
RUN_SHIFT = 4
RUN_BIG = 64
RUN_BIG_SHIFT = 6
LOCAL_ROWS = 1280
EXPERT_TILE = 512
N_ROUTED = BATCH * (N_META + SEQ)
MAX_ROWS = (2 * N_ROUTED + N_MOE_TILES * MOE_EXPERTS * (RUN_ALIGN - 1)
            + MOE_EXPERTS * (EXPERT_TILE - 1))
NT_MAX = -(-MAX_ROWS // EXPERT_TILE)
NR = NT_MAX * EXPERT_TILE
XS_COLS = D_MODEL + LANES
INT_ROWS = 8
TOK_GATE1, TOK_GATE2, TOK_POS1, TOK_POS2, TOK_EXPERT1 = 0, 3, 6, 7, 8
assert LOCAL_ROWS >= 2 * MOE_TM + MOE_EXPERTS * (RUN_ALIGN - 1) and LOCAL_ROWS % LANES == 0


def _first_argmax(vals):
    best, idx = vals[0], jnp.zeros(vals[0].shape, jnp.int32)
    for k in range(1, len(vals)):
        better = vals[k] > best
        idx = jnp.where(better, k, idx)
        best = jnp.where(better, vals[k], best)
    return idx, best


def _softmax_rows(vals):
    m = functools.reduce(jnp.maximum, vals)
    ex = [jnp.exp(v - m) for v in vals]
    tot = functools.reduce(lambda p, q: p + q, ex)
    return [e / tot for e in ex]


def _bf16_pieces(x):
    hi = x.astype(BF16).astype(F32)
    rest = x - hi
    mid = rest.astype(BF16).astype(F32)
    lo = (rest - mid).astype(BF16).astype(F32)
    return [hi, mid, lo]


def _router_kernel(h_ref, valid_ref, g_ref, wrh_ref, wrl_ref, br_ref, lpos_ref, tok_ref, runlen_ref, runoff_ref):
    hn = _rms(h_ref[...], g_ref[...])
    hn_hi = hn.astype(BF16)
    hn_lo = (hn - hn_hi.astype(F32)).astype(BF16)
    logits = (_dot(hn_hi, wrh_ref[...]) + (_dot(hn_hi, wrl_ref[...]) + _dot(hn_lo, wrh_ref[...]))
              + br_ref[...])
    lt = logits.T
    p_group = _softmax_rows([lt[k:k + 1, :] for k in range(MOE_GROUPS)])
    g_sel, p_g = _first_argmax(p_group)
    fine = []
    for k in range(MOE_PER_GROUP):
        f = lt[ROUTER_EXPERT_ROW + k:ROUTER_EXPERT_ROW + k + 1, :]
        for g in range(1, MOE_GROUPS):
            r0 = ROUTER_EXPERT_ROW + g * MOE_PER_GROUP + k
            f = jnp.where(g_sel == g, lt[r0:r0 + 1, :], f)
        fine.append(f)
    q = _softmax_rows(fine)
    i1, t1 = _first_argmax(q)
    i2, t2 = _first_argmax([jnp.where(i1 == k, -1.0, q[k]) for k in range(MOE_PER_GROUP)])
    tot = t1 + t2
    gate1 = p_g * (t1 / tot)
    gate2 = p_g * (t2 / tot)

    valid = valid_ref[...] > 0
    e1 = jnp.where(valid, g_sel * MOE_PER_GROUP + i1, -1)
    e2 = jnp.where(valid, g_sel * MOE_PER_GROUP + i2, -1)
    erow = lax.broadcasted_iota(jnp.int32, (MOE_EXPERTS, MOE_TM), 0)
    hit1 = erow == e1
    hit2 = erow == e2
    onehot = jnp.where(hit1 | hit2, 1.0, 0.0)
    si = lax.broadcasted_iota(jnp.int32, (MOE_TM, MOE_TM), 0)
    ti = lax.broadcasted_iota(jnp.int32, (MOE_TM, MOE_TM), 1)
    before = jnp.where(si < ti, 1.0, 0.0).astype(BF16)
    seen = _dot(onehot.astype(BF16), before)
    count = jnp.sum(onehot, axis=1, keepdims=True).astype(jnp.int32)
    runlen = lax.shift_left(lax.shift_right_logical(count + (RUN_ALIGN - 1), RUN_SHIFT), RUN_SHIFT)
    runlen_b = jnp.broadcast_to(runlen, (MOE_EXPERTS, LANES))
    ei = lax.broadcasted_iota(jnp.int32, (MOE_EXPERTS, MOE_EXPERTS), 0)
    ej = lax.broadcasted_iota(jnp.int32, (MOE_EXPERTS, MOE_EXPERTS), 1)
    runoff_b = _dot(jnp.where(ej < ei, 1.0, 0.0), runlen_b.astype(F32), HIGHEST)
    runlen_ref[...] = runlen_b
    runoff_ref[...] = runoff_b.astype(jnp.int32)
    place = seen + runoff_b[:, 0:1]
    pos1 = jnp.where(valid, jnp.sum(jnp.where(hit1, place, 0.0), axis=0, keepdims=True), -1.0)
    pos2 = jnp.where(valid, jnp.sum(jnp.where(hit2, place, 0.0), axis=0, keepdims=True), -1.0)
    r8 = lax.broadcasted_iota(jnp.int32, (INT_ROWS, MOE_TM), 0)
    lpos_ref[...] = jnp.where(r8 == 0, pos1.astype(jnp.int32),
                              jnp.where(r8 == 1, pos2.astype(jnp.int32), 0))

    rows = lax.broadcasted_iota(jnp.int32, lt.shape, 0)
    table = jnp.zeros(lt.shape, F32)
    for k, piece in enumerate(_bf16_pieces(gate1)):
        table = jnp.where(rows == TOK_GATE1 + k, piece, table)
    for k, piece in enumerate(_bf16_pieces(gate2)):
        table = jnp.where(rows == TOK_GATE2 + k, piece, table)
    table = jnp.where(rows == TOK_POS1, pos1, jnp.where(rows == TOK_POS2, pos2, table))
    table = jnp.where(rows == TOK_EXPERT1, e1.astype(F32), table)
    tok_ref[...] = table.T


def _router(h, valid, gain, wr, br):
    row = lambda i: (i, 0)
    col = lambda i: (0, i)
    const = lambda i: (0, 0)
    wr_hi = wr.astype(BF16)
    wr_lo = (wr - wr_hi.astype(F32)).astype(BF16)
    return pl.pallas_call(
        _router_kernel,
        grid=(N_MOE_TILES,),
        in_specs=[
            pl.BlockSpec((MOE_TM, D_MODEL), row),
            pl.BlockSpec((1, MOE_TM), col),
            pl.BlockSpec((1, D_MODEL), const),
            pl.BlockSpec((D_MODEL, LANES), const),
            pl.BlockSpec((D_MODEL, LANES), const),
            pl.BlockSpec((1, LANES), const),
        ],
        out_specs=[pl.BlockSpec((INT_ROWS, MOE_TM), col),
                   pl.BlockSpec((MOE_TM, LANES), row),
                   pl.BlockSpec((MOE_EXPERTS, LANES), row),
                   pl.BlockSpec((MOE_EXPERTS, LANES), row)],
        out_shape=[jax.ShapeDtypeStruct((INT_ROWS, TP), jnp.int32),
                   jax.ShapeDtypeStruct((TP, LANES), F32),
                   jax.ShapeDtypeStruct((N_MOE_TILES * MOE_EXPERTS, LANES), jnp.int32),
                   jax.ShapeDtypeStruct((N_MOE_TILES * MOE_EXPERTS, LANES), jnp.int32)],
        compiler_params=pltpu.CompilerParams(
            dimension_semantics=("arbitrary",), vmem_limit_bytes=VMEM_LIMIT),
        name="moe_router",
    )(h, valid, gain, wr_hi, wr_lo, br)


def _moe_plan(runlen, runoff):
    i32 = jnp.int32
    lens = runlen[:, 0].reshape(N_MOE_TILES, MOE_EXPERTS)
    total = jnp.sum(lens, axis=0)
    padded = (total + (EXPERT_TILE - 1)) // EXPERT_TILE * EXPERT_TILE
    ends = jnp.cumsum(padded)
    base = ends - padded
    goff = base[None, :] + jnp.cumsum(lens, axis=0) - lens
    nval = ends[-1] // EXPERT_TILE
    tidx = jnp.minimum(jnp.arange(NT_MAX, dtype=i32), nval - 1)
    texp = jnp.minimum(jnp.sum(tidx[:, None] >= (ends // EXPERT_TILE)[None, :], axis=1), MOE_EXPERTS - 1)
    return dict(
        goff=goff.reshape(-1).astype(i32), lens=lens.reshape(-1).astype(i32),
        loff=runoff[:, 0].astype(i32), gap_start=(base + total).astype(i32),
        gap_len=(padded - total).astype(i32), nval=nval.reshape(1).astype(i32),
        tidx=tidx.astype(i32), texp=texp.astype(i32),
        nbig=jnp.sum(lens // RUN_BIG, axis=1).astype(i32),
        nsmall=jnp.sum(lens % RUN_BIG // RUN_ALIGN, axis=1).astype(i32))


def _aligned(x):
    return pl.multiple_of(x, RUN_ALIGN)


def _for_each_run(step, lens_ref, fn):
    def expert_body(e, carry):
        k = step * MOE_EXPERTS + e
        nbig = lax.shift_right_logical(lens_ref[k], RUN_BIG_SHIFT)
        rest = nbig * RUN_BIG

        def big_body(q, c):
            fn(k, q * RUN_BIG, RUN_BIG)
            return c

        def small_body(q, c):
            fn(k, rest + q * RUN_ALIGN, RUN_ALIGN)
            return c

        lax.fori_loop(0, nbig, big_body, 0)
        lax.fori_loop(0, lax.shift_right_logical(lens_ref[k] - rest, RUN_SHIFT), small_body, 0)
        return carry

    lax.fori_loop(0, MOE_EXPERTS, expert_body, 0)


def _wait_runs(copy, step, nbig_ref, nsmall_ref):
    def big_body(q, c):
        copy(RUN_BIG).wait()
        return c

    def small_body(q, c):
        copy(RUN_ALIGN).wait()
        return c

    lax.fori_loop(0, nbig_ref[step], big_body, 0)
    lax.fori_loop(0, nsmall_ref[step], small_body, 0)


def _dispatch_kernel(goff_ref, lens_ref, loff_ref, gaps_ref, gapl_ref, nval_ref, nbig_ref, nsmall_ref,
                     h_ref, g_ref, lpos_ref, tok_ref, xs_ref, local, zeros, sems):
    i = pl.program_id(0)
    last = pl.num_programs(0) - 1
    slot = lax.rem(i, 2)

    def run_copy(s, src, dst, rows):
        return pltpu.make_async_copy(local.at[s, pl.ds(_aligned(src), rows), :],
                                     xs_ref.at[pl.ds(_aligned(dst), rows), :], sems.at[s])

    def wait_tile(step):
        s = lax.rem(step, 2)
        _wait_runs(lambda rows: run_copy(s, 0, 0, rows), step, nbig_ref, nsmall_ref)

    @pl.when(i >= 2)
    def _():
        wait_tile(i - 2)

    hn = _rms(h_ref[...], g_ref[...]).astype(BF16)
    riota = lax.broadcasted_iota(jnp.int32, (LOCAL_ROWS, MOE_TM), 0)
    sel = jnp.where((riota == lpos_ref[0:1, :]) | (riota == lpos_ref[1:2, :]), 1.0, 0.0).astype(BF16)
    lane = lax.broadcasted_iota(jnp.int32, (MOE_TM, LANES), 1)
    extra = jnp.where((lane < TOK_POS1) | (lane == TOK_EXPERT1), tok_ref[...], 0.0).astype(BF16)
    local[slot] = _dot(sel, jnp.concatenate([hn, extra], axis=1)).astype(BF16)

    _for_each_run(i, lens_ref,
                  lambda k, q, rows: run_copy(slot, loff_ref[k] + q, goff_ref[k] + q, rows).start())

    @pl.when(i == last)
    def _():
        zeros[...] = jnp.zeros_like(zeros)
        zsem = sems.at[2]

        def gap_copy(dst):
            return pltpu.make_async_copy(zeros.at[pl.ds(0, RUN_ALIGN), :],
                                         xs_ref.at[pl.ds(_aligned(dst), RUN_ALIGN), :], zsem)

        def tile_copy(t):
            return pltpu.make_async_copy(
                zeros, xs_ref.at[pl.ds(pl.multiple_of(t * EXPERT_TILE, EXPERT_TILE), EXPERT_TILE), :], zsem)

        def expert_body(e, n):
            g = lax.shift_right_logical(gapl_ref[e], RUN_SHIFT)

            def body(q, c):
                gap_copy(gaps_ref[e] + q * RUN_ALIGN).start()
                return c

            lax.fori_loop(0, g, body, 0)
            return n + g

        ngap = lax.fori_loop(0, MOE_EXPERTS, expert_body, 0)

        def tail_start(t, c):
            tile_copy(t).start()
            return c

        lax.fori_loop(nval_ref[0], NT_MAX, tail_start, 0)

        def gap_wait(k, c):
            gap_copy(0).wait()
            return c

        lax.fori_loop(0, ngap, gap_wait, 0)

        def tail_wait(t, c):
            tile_copy(t).wait()
            return c

        lax.fori_loop(nval_ref[0], NT_MAX, tail_wait, 0)

        @pl.when(i >= 1)
        def _():
            wait_tile(i - 1)

        wait_tile(i)


def _dispatch(plan, h, gain, lpos, tok):
    row = lambda i, *_: (i, 0)
    return pl.pallas_call(
        _dispatch_kernel,
        grid_spec=pltpu.PrefetchScalarGridSpec(
            num_scalar_prefetch=8,
            grid=(N_MOE_TILES,),
            in_specs=[
                pl.BlockSpec((MOE_TM, D_MODEL), row),
                pl.BlockSpec((1, D_MODEL), lambda i, *_: (0, 0)),
                pl.BlockSpec((INT_ROWS, MOE_TM), lambda i, *_: (0, i)),
                pl.BlockSpec((MOE_TM, LANES), row),
            ],
            out_specs=pl.BlockSpec(memory_space=pl.ANY),
            scratch_shapes=[pltpu.VMEM((2, LOCAL_ROWS, XS_COLS), BF16),
                            pltpu.VMEM((EXPERT_TILE, XS_COLS), BF16),
                            pltpu.SemaphoreType.DMA((3,))],
        ),
        out_shape=jax.ShapeDtypeStruct((NR, XS_COLS), BF16),
        compiler_params=pltpu.CompilerParams(
            dimension_semantics=("arbitrary",), vmem_limit_bytes=VMEM_LIMIT),
        name="moe_dispatch",
    )(plan["goff"], plan["lens"], plan["loff"], plan["gap_start"], plan["gap_len"], plan["nval"],
      plan["nbig"], plan["nsmall"], h, gain, lpos, tok)


def _experts_kernel(tidx_ref, texp_ref, nval_ref, xs_ref, wg_ref, wu_ref, wd_ref, ys_ref, wg16, wu16, wd16):
    j = pl.program_id(0)
    active = j < nval_ref[0]
    new_expert = (j == 0) | (texp_ref[j] != texp_ref[jnp.maximum(j - 1, 0)])

    @pl.when(active & new_expert)
    def _():
        wg16[...] = wg_ref[0].astype(BF16)
        wu16[...] = wu_ref[0].astype(BF16)
        wd16[...] = wd_ref[0].astype(BF16)

    @pl.when(active)
    def _():
        x = xs_ref[:, 0:D_MODEL]
        extra = xs_ref[:, D_MODEL:XS_COLS].astype(F32)
        lane = lax.broadcasted_iota(jnp.int32, extra.shape, 1)

        def lanes(lo, hi):
            return jnp.sum(jnp.where((lane >= lo) & (lane < hi), extra, 0.0), axis=1, keepdims=True)

        first = lanes(TOK_EXPERT1, TOK_EXPERT1 + 1) == texp_ref[j].astype(F32)
        gate = jnp.where(first, lanes(TOK_GATE1, TOK_GATE2), lanes(TOK_GATE2, TOK_POS1))
        a = _dot(x, wg16[...])
        b = _dot(x, wu16[...])
        y = _dot((_silu(a) * b).astype(BF16), wd16[...])
        ys_ref[...] = (gate * y).astype(BF16)

    @pl.when(jnp.logical_not(active))
    def _():
        ys_ref[...] = jnp.zeros_like(ys_ref)


def _experts(plan, xs, layer, wg, wu, wd):
    tile = lambda j, tidx, texp, nval: (tidx[j], 0)
    expert = lambda j, tidx, texp, nval: (layer, texp[j], 0, 0)
    return pl.pallas_call(
        _experts_kernel,
        grid_spec=pltpu.PrefetchScalarGridSpec(
            num_scalar_prefetch=3,
            grid=(NT_MAX,),
            in_specs=[
                pl.BlockSpec((EXPERT_TILE, XS_COLS), tile),
                pl.BlockSpec((None, 1, D_MODEL, D_EXPERT), expert),
                pl.BlockSpec((None, 1, D_MODEL, D_EXPERT), expert),
                pl.BlockSpec((None, 1, D_EXPERT, D_MODEL), expert),
            ],
            out_specs=pl.BlockSpec((EXPERT_TILE, D_MODEL), lambda j, *_: (j, 0)),
            scratch_shapes=[pltpu.VMEM((D_MODEL, D_EXPERT), BF16),
                            pltpu.VMEM((D_MODEL, D_EXPERT), BF16),
                            pltpu.VMEM((D_EXPERT, D_MODEL), BF16)],
        ),
        out_shape=jax.ShapeDtypeStruct((NR, D_MODEL), BF16),
        compiler_params=pltpu.CompilerParams(
            dimension_semantics=("arbitrary",), vmem_limit_bytes=VMEM_LIMIT),
        name="moe_experts",
    )(plan["tidx"], plan["texp"], plan["nval"], xs, wg, wu, wd)


def _combined_rows(goff_ref, lens_ref, loff_ref, nbig_ref, nsmall_ref, h_ref, tok_ref, ys_ref, local, sems):
    i = pl.program_id(0)
    slot = lax.rem(i, 2)

    def run_copy(s, src, dst, rows):
        return pltpu.make_async_copy(ys_ref.at[pl.ds(_aligned(src), rows), :],
                                     local.at[s, pl.ds(_aligned(dst), rows), :], sems.at[s])

    def start_tile(step):
        s = lax.rem(step, 2)
        _for_each_run(step, lens_ref,
                      lambda k, q, rows: run_copy(s, goff_ref[k] + q, loff_ref[k] + q, rows).start())

    @pl.when(i == 0)
    def _():
        local[...] = jnp.zeros_like(local)
        start_tile(0)

    @pl.when(i < pl.num_programs(0) - 1)
    def _():
        start_tile(i + 1)

    _wait_runs(lambda rows: run_copy(slot, 0, 0, rows), i, nbig_ref, nsmall_ref)
    tok = tok_ref[...]
    p1 = tok[:, TOK_POS1:TOK_POS1 + 1].astype(jnp.int32)
    p2 = tok[:, TOK_POS2:TOK_POS2 + 1].astype(jnp.int32)
    ciota = lax.broadcasted_iota(jnp.int32, (MOE_TM, LOCAL_ROWS), 1)
    sel = jnp.where((ciota == p1) | (ciota == p2), 1.0, 0.0).astype(BF16)
    return h_ref[...] + _dot(sel, local[slot])


def _combine_pool_kernel(goff_ref, lens_ref, loff_ref, nbig_ref, nsmall_ref, h_ref, tok_ref,
                         g_ref, pw_ref, pb_ref, ps_ref, ys_ref, o_ref, local, mixed, buf, sems):
    mixed[...] = _combined_rows(goff_ref, lens_ref, loff_ref, nbig_ref, nsmall_ref, h_ref, tok_ref, ys_ref,
                                local, sems)
    _pool_rows(pl.program_id(0), mixed, g_ref, pw_ref, pb_ref, ps_ref, o_ref, buf)


def _combine_final_kernel(goff_ref, lens_ref, loff_ref, nbig_ref, nsmall_ref, h_ref, tok_ref, g_ref, ys_ref,
                          o_ref, local, stage, sems, out_sems):
    i = pl.program_id(0)
    last = pl.num_programs(0) - 1
    slot = lax.rem(i, 2)
    per_tile = MOE_TM // CHUNK

    def for_each_seq_chunk(step, fn):
        s = lax.rem(step, 2)
        for k in range(per_tile):
            chunk = step * per_tile + k
            c = lax.rem(chunk, NCHUNK)
            dst = (lax.div(chunk, NCHUNK) * (SEQ // CHUNK) + c - 1) * CHUNK

            @pl.when(c > 0)
            def _():
                fn(pltpu.make_async_copy(stage.at[s, pl.ds(k * CHUNK, CHUNK), :],
                                         o_ref.at[pl.ds(pl.multiple_of(dst, CHUNK), CHUNK), :],
                                         out_sems.at[s]))

    @pl.when(i >= 2)
    def _():
        for_each_seq_chunk(i - 2, lambda copy: copy.wait())

    out = _combined_rows(goff_ref, lens_ref, loff_ref, nbig_ref, nsmall_ref, h_ref, tok_ref, ys_ref,
                         local, sems)
    stage[slot] = _rms(out, g_ref[...])
    for_each_seq_chunk(i, lambda copy: copy.start())

    @pl.when(i == last)
    def _():
        @pl.when(i >= 1)
        def _():
            for_each_seq_chunk(i - 1, lambda copy: copy.wait())

        for_each_seq_chunk(i, lambda copy: copy.wait())


def _combine(plan, h, tok, ys, pool=None, final_gain=None):
    final = final_gain is not None
    row = lambda i, *_: (i, 0)
    whole = lambda a: pl.BlockSpec(a.shape, lambda i, *_: (0,) * a.ndim)
    extra = [final_gain] if final else list(pool)
    scratch = [pltpu.VMEM((2, LOCAL_ROWS, D_MODEL), BF16)]
    if final:
        scratch.append(pltpu.VMEM((2, MOE_TM, D_MODEL), F32))
    else:
        scratch += [pltpu.VMEM((MOE_TM, D_MODEL), F32), pltpu.VMEM((CHUNK + POOL_MAX_WINDOW, D_MODEL), F32)]
    return pl.pallas_call(
        _combine_final_kernel if final else _combine_pool_kernel,
        grid_spec=pltpu.PrefetchScalarGridSpec(
            num_scalar_prefetch=5,
            grid=(N_MOE_TILES,),
            in_specs=([pl.BlockSpec((MOE_TM, D_MODEL), row), pl.BlockSpec((MOE_TM, LANES), row)]
                      + [whole(a) for a in extra] + [pl.BlockSpec(memory_space=pl.ANY)]),
            out_specs=(pl.BlockSpec(memory_space=pl.ANY) if final
                       else pl.BlockSpec((MOE_TM, D_MODEL), row)),
            scratch_shapes=scratch + [pltpu.SemaphoreType.DMA((2,))] * (2 if final else 1),
        ),
        out_shape=jax.ShapeDtypeStruct((BATCH * SEQ if final else TP, D_MODEL), F32),
        compiler_params=pltpu.CompilerParams(
            dimension_semantics=("arbitrary",), vmem_limit_bytes=VMEM_LIMIT),
        name="moe_combine_final" if final else "moe_combine_pool",
    )(plan["goff"], plan["lens"], plan["loff"], plan["nbig"], plan["nsmall"], h, tok, *extra, ys)


def _row(v):
    return v.reshape(1, -1).astype(F32)


def _pad_lanes(v):
    return jnp.pad(_row(v), ((0, 0), (0, LANES - v.shape[-1])))


def _pair_blockdiag(w):
    w = w.reshape(LRU_WIDTH // LANES, 2, 64, 64)
    z = jnp.zeros_like(w[:, 0])
    top = jnp.concatenate([w[:, 0], z], axis=2)
    bot = jnp.concatenate([z, w[:, 1]], axis=2)
    return jnp.concatenate([top, bot], axis=1).astype(BF16)


def _moe_layer(h, layer, pool, final_gain, valid, ffn_norm, rgw, rgb, rew, reb, wg, wu, wd):
    wr = jnp.zeros((D_MODEL, LANES), F32)
    wr = wr.at[:, 0:MOE_GROUPS].set(rgw[layer])
    wr = wr.at[:, ROUTER_EXPERT_ROW:ROUTER_EXPERT_ROW + MOE_EXPERTS].set(rew[layer])
    br = jnp.zeros((1, LANES), F32)
    br = br.at[0, 0:MOE_GROUPS].set(rgb[layer])
    br = br.at[0, ROUTER_EXPERT_ROW:ROUTER_EXPERT_ROW + MOE_EXPERTS].set(reb[layer])
    gain = _row(ffn_norm[layer])
    lpos, tok, runlen, runoff = _router(h, valid, gain, wr, br)
    plan = _moe_plan(runlen, runoff)
    xs = _dispatch(plan, h, gain, lpos, tok)
    ys = _experts(plan, xs, layer, wg, wu, wd)
    return _combine(plan, h, tok, ys, pool, final_gain)


def kernel(x, meta_tokens, norm_final, mix_norm_even, w_in, ssd_conv_w, ssd_conv_b, ssd_dt_bias, ssd_a_log, ssd_d, ssd_norm, lru_conv_w, lru_conv_b, lru_w_a, lru_b_a, lru_w_x, lru_b_x, lru_lambda, w_out, mix_norm_odd, pool_w, pool_b, pool_scale, ffn_norm, router_group_w, router_group_b, router_expert_w, router_expert_b, expert_w_gate, expert_w_up, expert_w_down):
    meta_chunk = jnp.concatenate([jnp.zeros((PAD, D_MODEL), F32), meta_tokens.astype(F32)], axis=0)
    rows = jnp.arange(CHUNK)
    meta_chunk = meta_chunk[(rows % 8) * SEG + rows // 8]
    valid = ((jnp.arange(TP, dtype=jnp.int32) % LP) >= PAD).astype(jnp.int32).reshape(1, TP)
    moe_args = (valid, ffn_norm, router_group_w, router_group_b, router_expert_w, router_expert_b,
                expert_w_gate, expert_w_up, expert_w_down)

    wi = w_in[0]
    w_proj = jnp.concatenate(
        [wi[:, 0:1024], wi[:, 1024:2560], wi[:, 2576:3600], wi[:, 3600:4624], wi[:, 2560:2576],
         jnp.zeros((D_MODEL, LANES - SSD_HEADS), F32)], axis=1).astype(BF16)
    piece_head = jnp.where(jnp.arange(LANES) < HEAD_PIECES * SSD_HEADS, jnp.arange(LANES) % SSD_HEADS, -1)
    expand = (piece_head[:, None] == (jnp.arange(SSD_WIDTH) // SSD_HEAD_DIM)[None, :]).astype(BF16)
    params = [
        _row(mix_norm_even[0]), w_proj,
        ssd_conv_w[0].astype(F32), _row(ssd_conv_b[0]), _pad_lanes(ssd_dt_bias[0]), _pad_lanes(ssd_a_log[0]),
        _row(jnp.repeat(ssd_d[0], SSD_HEAD_DIM)), _row(ssd_norm[0]),
        lru_conv_w[0].astype(F32), _row(lru_conv_b[0]), _pair_blockdiag(lru_w_a[0]), _row(lru_b_a[0]),
        _pair_blockdiag(lru_w_x[0]), _row(lru_b_x[0]), _row(lru_lambda[0]),
        w_out[0].astype(BF16), expand,
    ]
    h = _mixer(x.astype(F32), meta_chunk, params).reshape(TP, D_MODEL)
    pool = (_row(mix_norm_odd[0]), pool_w[0].astype(BF16), _row(pool_b[0]), _row(pool_scale[0]))
    h = _moe_layer(h, 0, pool, None, *moe_args)
    out = _moe_layer(h, 1, None, _row(norm_final), *moe_args)
    return out.reshape(BATCH, SEQ, D_MODEL)
```

```python
import functools
import math

import jax
import jax.numpy as jnp
from jax import lax
from jax.experimental import pallas as pl
from jax.experimental.pallas import tpu as pltpu

F32 = jnp.float32
BF16 = jnp.bfloat16
HIGHEST = lax.Precision.HIGHEST

D_MODEL = 1024
BATCH = 8
SEQ = 2048
N_META = 16
RMS_EPS = 1e-6
CONV_WIDTH = 4
CHUNK = 128
PAD = CHUNK - N_META
LP = PAD + N_META + SEQ
NCHUNK = LP // CHUNK
TP = BATCH * LP

SSD_HEADS = 16
SSD_HEAD_DIM = 64
SSD_WIDTH = 1024
SSD_GROUPS = 2
SSD_STATE = 128
SSD_GROUP_WIDTH = SSD_WIDTH // SSD_GROUPS
SSD_CONV_DIM = SSD_WIDTH + 2 * SSD_GROUPS * SSD_STATE
LRU_WIDTH = 1024
LRU_C = 8.0
LANES = 128
POOL_WINDOWS = (2, 4, 8, 16)
POOL_GROUP_DIM = 256
POOL_MAX_WINDOW = 16
MOE_GROUPS = 4
MOE_PER_GROUP = 4
MOE_EXPERTS = 16
D_EXPERT = 512
ROUTER_EXPERT_ROW = 8

VMEM_LIMIT = 56 * 1024 * 1024


def _dot(a, b, precision=None):
    return jnp.dot(a, b, preferred_element_type=F32, precision=precision)


def _rms(x, gain):
    ms = jnp.mean(x * x, axis=-1, keepdims=True)
    return x * lax.rsqrt(ms + RMS_EPS) * gain


def _silu(x):
    return x * jax.nn.sigmoid(x)


def _softplus(x):
    return jnp.maximum(x, 0.0) + jnp.log1p(jnp.exp(-jnp.abs(x)))


def _row_ids(shape, chunk):
    return lax.broadcasted_iota(jnp.int32, shape, 0) + chunk * CHUNK


MIX_B = 2
N_MIX_ITEMS = (BATCH // MIX_B) * NCHUNK
_Z0, _X0, _G0, _L0, _T0, _PEND = 0, 1024, 2560, 3584, 4608, 4736
HEAD_PIECES = 3
PROJ_PIECE = 256


def _carried(c, state):
    return jnp.where(jnp.full(state.shape, c, jnp.int32) == 0, 0.0, state)


SEG = CHUNK // 8


def _time_ids(shape, axis=0):
    p = lax.broadcasted_iota(jnp.int32, shape, axis)
    return lax.bitwise_and(p, 7) * SEG + lax.shift_right_logical(p, 3)


def _load_interleaved(slab, src):
    cols = []
    for j in range(D_MODEL // LANES):
        slab[j] = src[:, j * LANES:(j + 1) * LANES]
        cols.append(jnp.concatenate([slab[j, pl.ds(k, 8, stride=SEG), :] for k in range(SEG)], axis=0))
    return jnp.concatenate(cols, axis=1)


def _store_time_order(slab, dst, value):
    for j in range(D_MODEL // LANES):
        for k in range(SEG):
            slab[j, pl.ds(k, 8, stride=SEG), :] = value[k * 8:(k + 1) * 8, j * LANES:(j + 1) * LANES]
        dst[:, j * LANES:(j + 1) * LANES] = slab[j]


def _causal_conv(hist, c, x, w_ref, b_ref):
    taps = CONV_WIDTH - 1
    prev = _carried(c, hist[...])
    hist[...] = x[(SEG - taps) * 8:, :]
    first_segment = lax.broadcasted_iota(jnp.int32, (8, x.shape[1]), 0) == 0
    groups = [x[k * 8:(k + 1) * 8, :] for k in range(SEG)]
    wrapped = {i: jnp.where(first_segment, pltpu.roll(prev[(taps - i) * 8:(taps - i + 1) * 8, :], 1, 0),
                            pltpu.roll(groups[SEG - i], 1, 0)) for i in range(1, taps + 1)}
    out = []
    for k in range(SEG):
        acc = b_ref[...] + w_ref[taps:taps + 1, :] * groups[k]
        for j in range(1, taps + 1):
            earlier = groups[k - j] if k >= j else wrapped[j - k]
            acc = acc + w_ref[taps - j:taps - j + 1, :] * earlier
        out.append(acc)
    return jnp.concatenate(out, axis=0)


def _expand_heads(v, exp_ref):
    lane = lax.broadcasted_iota(jnp.int32, v.shape, 1)
    rest = jnp.where(lane < SSD_HEADS, v, 0.0)
    packed = None
    for k in range(HEAD_PIECES):
        piece = rest.astype(BF16).astype(F32)
        rest = rest - piece
        moved = piece if k == 0 else pltpu.roll(piece, k * SSD_HEADS, 1)
        packed = moved if packed is None else packed + moved
    return _dot(packed.astype(BF16), exp_ref[...])


def _ssd_chunk(c, z, xbc, dt_raw, hist, st, cw_ref, cb_ref, dtb_ref, alog_ref, dsk_ref, ng_ref, exp_ref):
    xc = _silu(_causal_conv(hist, c, xbc, cw_ref, cb_ref))
    yield
    xs = xc[:, 0:SSD_WIDTH]
    valid_s = _time_ids((CHUNK, 2 * SSD_STATE)) + c * CHUNK >= PAD
    bm = jnp.where(valid_s, xc[:, SSD_WIDTH:SSD_WIDTH + 2 * SSD_STATE], 0.0)
    cm = jnp.where(valid_s, xc[:, SSD_WIDTH + 2 * SSD_STATE:], 0.0)

    li = _time_ids((CHUNK, CHUNK), 0)
    si = _time_ids((CHUNK, CHUNK), 1)
    causal = si <= li
    dt = _softplus(dt_raw + dtb_ref[...])
    dt = jnp.where(li + c * CHUNK >= PAD, dt, 0.0)
    adt = dt * (-jnp.exp(alog_ref[...]))
    a_cs = _dot(causal.astype(F32), adt, HIGHEST)
    a_last = a_cs[CHUNK - 1:CHUNK, :]
    ea = jnp.exp(a_cs)
    dt_x = _expand_heads(dt, exp_ref)
    w_x = _expand_heads(jnp.exp(a_last - a_cs) * dt, exp_ref)
    ea_x = _expand_heads(ea, exp_ref)
    xdt = xs * dt_x
    a_cs_t = a_cs.T
    yield
    lane = lax.broadcasted_iota(jnp.int32, (CHUNK, LANES), 1)

    ys = []
    for g in range(SSD_GROUPS):
        gsl = slice(g * SSD_GROUP_WIDTH, (g + 1) * SSD_GROUP_WIDTH)
        bg = bm[:, g * SSD_STATE:(g + 1) * SSD_STATE]
        cg16 = cm[:, g * SSD_STATE:(g + 1) * SSD_STATE].astype(BF16)
        cbm = lax.dot_general(cg16, bg.astype(BF16), (((1,), (1,)), ((), ())),
                              preferred_element_type=F32)
        s_in = _carried(c, st[:, gsl])
        y_off = _dot(cg16, s_in.astype(BF16)) * ea_x[:, gsl]
        s_new = _dot(bg.T.astype(BF16), (w_x[:, gsl] * xs[:, gsl]).astype(BF16))
        st[:, gsl] = ea_x[CHUNK - 1:CHUNK, gsl] * s_in + s_new
        for j in range(SSD_GROUP_WIDTH // LANES):
            xp = xdt[:, g * SSD_GROUP_WIDTH + j * LANES:
                     g * SSD_GROUP_WIDTH + (j + 1) * LANES].astype(BF16)
            weights, values = [], []
            for hh in range(LANES // SSD_HEAD_DIM):
                hd = g * (SSD_HEADS // SSD_GROUPS) + j * (LANES // SSD_HEAD_DIM) + hh
                seg = a_cs[:, hd:hd + 1] - a_cs_t[hd:hd + 1, :]
                dec = jnp.exp(jnp.where(causal, seg, -1e30))
                weights.append((cbm * dec).astype(BF16))
                own = (lane >= hh * SSD_HEAD_DIM) & (lane < (hh + 1) * SSD_HEAD_DIM)
                values.append(jnp.where(own, xp, jnp.zeros_like(xp)))
            y_diag = _dot(jnp.concatenate(weights, axis=1), jnp.concatenate(values, axis=0))
            ys.append(y_diag + y_off[:, j * LANES:(j + 1) * LANES])
            if j % 2:
                yield
    y = jnp.concatenate(ys, axis=1) + dsk_ref[...] * xs
    y = y * _silu(z)
    normed = []
    for g in range(SSD_GROUPS):
        yg = y[:, g * SSD_GROUP_WIDTH:(g + 1) * SSD_GROUP_WIDTH]
        normed.append(yg * lax.rsqrt(jnp.mean(yg * yg, axis=-1, keepdims=True) + RMS_EPS))
    return jnp.concatenate(normed, axis=1) * ng_ref[...]


def _lru_chunk(c, gt, lin, hist, lc, lcw_ref, lcb_ref, wax_ref, ba_ref, bx_ref, lam_ref):
    xb = _causal_conv(hist, c, lin, lcw_ref, lcb_ref)
    yield
    xb16 = xb.astype(BF16)
    both = [_dot(xb16[:, j * LANES:(j + 1) * LANES], wax_ref[j]) for j in range(LRU_WIDTH // LANES)]
    r = jax.nn.sigmoid(jnp.concatenate([p[:, 0:LANES] for p in both], axis=1) + ba_ref[...])
    ig = jax.nn.sigmoid(jnp.concatenate([p[:, LANES:2 * LANES] for p in both], axis=1) + bx_ref[...])
    log_a = (-LRU_C * _softplus(-lam_ref[...])) * r
    a = jnp.exp(log_a)
    mult = jnp.sqrt(jnp.tanh(-log_a) * (a * a + 1.0))
    grow = _time_ids((CHUNK, LRU_WIDTH)) + c * CHUNK
    mult = jnp.where(grow == PAD, 1.0, mult)
    u = jnp.where(grow >= PAD, mult * (ig * xb), 0.0)
    yield

    local, decay = [], []
    for k in range(SEG):
        a8 = a[k * 8:(k + 1) * 8, :]
        u8 = u[k * 8:(k + 1) * 8, :]
        local.append(u8 if k == 0 else a8 * local[-1] + u8)
        decay.append(a8 if k == 0 else a8 * decay[-1])
        if k == SEG // 2:
            yield
    r8 = lax.broadcasted_iota(jnp.int32, (8, LRU_WIDTH), 0)
    end, span = local[-1], decay[-1]
    for d in (1, 2, 4):
        end_sh = jnp.where(r8 >= d, pltpu.roll(end, d, 0), 0.0)
        span_sh = jnp.where(r8 >= d, pltpu.roll(span, d, 0), 1.0)
        end = span * end_sh + end
        span = span * span_sh
    carry = _carried(c, lc[0:1, :])
    end = end + span * carry
    lc[0:1, :] = end[7:8, :]
    entering = jnp.where(r8 >= 1, pltpu.roll(end, 1, 0), carry)
    hs = [local[k] + decay[k] * entering for k in range(SEG)]
    gelu = 0.5 * gt * (1.0 + jnp.tanh(math.sqrt(2.0 / math.pi) * (gt + 0.044715 * (gt * gt * gt))))
    return jnp.concatenate(hs, axis=0) * gelu


def _mixer_kernel(x_ref, meta_ref, gin_ref, wproj_ref,
                  cw_ref, cb_ref, dtb_ref, alog_ref, dsk_ref, ng_ref,
                  lcw_ref, lcb_ref, wax_ref, ba_ref, bx_ref, lam_ref,
                  wout_ref, exp_ref, o_ref, h_cur, h_next, p_cur, p_next, hist_x, hist_l, st, lc,
                  slab_in, slab_out):
    s = pl.program_id(0)
    p_chunk = lax.rem(jnp.minimum(s, N_MIX_ITEMS - 1), NCHUNK)
    c = lax.rem(jnp.maximum(s - 1, 0), NCHUNK)

    @pl.when(s == 0)
    def _():
        h_next[...] = jnp.zeros_like(h_next)
        p_next[...] = jnp.zeros_like(p_next)
        hist_x[...] = jnp.zeros_like(hist_x)
        hist_l[...] = jnp.zeros_like(hist_l)
        st[...] = jnp.zeros_like(st)
        lc[...] = jnp.zeros_like(lc)

    h_cur[...] = h_next[...]
    p_cur[...] = p_next[...]

    from_meta = jnp.full((CHUNK, D_MODEL), p_chunk, jnp.int32) == 0
    for b in range(MIX_B):
        h_next[b * CHUNK:(b + 1) * CHUNK, :] = jnp.where(
            from_meta, meta_ref[...], _load_interleaved(slab_in, x_ref.at[b]))
    hn = _rms(h_next[...], gin_ref[...]).astype(BF16)
    pieces = [(lo, min(lo + PROJ_PIECE, _PEND)) for lo in range(0, _PEND, PROJ_PIECE)]

    def project_piece():
        if pieces:
            lo, hi = pieces.pop(0)
            p_next[:, lo:hi] = _dot(hn, wproj_ref[:, lo:hi])

    def mix_chunk(b):
        rows = slice(b * CHUNK, (b + 1) * CHUNK)
        y_ssd = yield from _ssd_chunk(c, p_cur[rows, _Z0:_X0], p_cur[rows, _X0:_G0], p_cur[rows, _T0:_PEND],
                                      hist_x.at[b], st.at[b],
                                      cw_ref, cb_ref, dtb_ref, alog_ref, dsk_ref, ng_ref, exp_ref)
        y_lru = yield from _lru_chunk(c, p_cur[rows, _G0:_L0], p_cur[rows, _L0:_T0], hist_l.at[b], lc.at[b],
                                      lcw_ref, lcb_ref, wax_ref, ba_ref, bx_ref, lam_ref)
        return jnp.concatenate([y_ssd, y_lru], axis=1).astype(BF16)

    h = h_cur[...]
    mixed = []
    project_piece()
    for b in range(MIX_B):
        stages = mix_chunk(b)
        while True:
            try:
                next(stages)
            except StopIteration as done:
                mixed.append(done.value)
                break
            project_piece()
    while pieces:
        project_piece()
    out = _dot(jnp.concatenate(mixed, axis=0), wout_ref[...]) + h
    keep = _time_ids((CHUNK, D_MODEL)) + c * CHUNK >= PAD
    for b in range(MIX_B):
        _store_time_order(slab_out, o_ref.at[b], jnp.where(keep, out[b * CHUNK:(b + 1) * CHUNK, :], 0.0))


def _mixer(x, meta_chunk, params):
    full = lambda a: pl.BlockSpec(a.shape, lambda s: (0,) * a.ndim, pipeline_mode=pl.Buffered(1))

    def x_block(s):
        item = jnp.minimum(s, N_MIX_ITEMS - 1)
        return item // NCHUNK, jnp.maximum(item % NCHUNK - 1, 0), 0

    def out_block(s):
        item = jnp.maximum(s - 1, 0)
        return item // NCHUNK, item % NCHUNK, 0

    return pl.pallas_call(
        _mixer_kernel,
        grid=(N_MIX_ITEMS + 1,),
        in_specs=[pl.BlockSpec((MIX_B, CHUNK, D_MODEL), x_block), full(meta_chunk)] + [full(p) for p in params],
        out_specs=pl.BlockSpec((MIX_B, CHUNK, D_MODEL), out_block),
        out_shape=jax.ShapeDtypeStruct((BATCH, LP, D_MODEL), F32),
        scratch_shapes=[
            pltpu.VMEM((MIX_B * CHUNK, D_MODEL), F32),
            pltpu.VMEM((MIX_B * CHUNK, D_MODEL), F32),
            pltpu.VMEM((MIX_B * CHUNK, _PEND), F32),
            pltpu.VMEM((MIX_B * CHUNK, _PEND), F32),
            pltpu.VMEM((MIX_B, 8 * (CONV_WIDTH - 1), SSD_CONV_DIM), F32),
            pltpu.VMEM((MIX_B, 8 * (CONV_WIDTH - 1), LRU_WIDTH), F32),
            pltpu.VMEM((MIX_B, SSD_STATE, SSD_WIDTH), F32),
            pltpu.VMEM((MIX_B, 8, LRU_WIDTH), F32),
            pltpu.VMEM((D_MODEL // LANES, CHUNK, LANES), F32),
            pltpu.VMEM((D_MODEL // LANES, CHUNK, LANES), F32),
        ],
        compiler_params=pltpu.CompilerParams(
            dimension_semantics=("arbitrary",), vmem_limit_bytes=VMEM_LIMIT),
        name="ssd_lru_mixer",
    )(x, meta_chunk, *params)


def _pool_rows(step, h_ref, g_ref, pw_ref, pb_ref, ps_ref, o_ref, buf):
    base = POOL_MAX_WINDOW
    for k in range(h_ref.shape[0] // CHUNK):
        chunk = step * (h_ref.shape[0] // CHUNK) + k
        c = lax.rem(chunk, NCHUNK)

        @pl.when(c == 0)
        def _():
            buf[0:base, :] = jnp.zeros((base, D_MODEL), F32)

        h = h_ref[k * CHUNK:(k + 1) * CHUNK, :]
        hn = _rms(h, g_ref[...])
        buf[base:base + CHUNK, :] = hn
        pos = _row_ids((CHUNK, POOL_GROUP_DIM), c) - PAD
        outs = []
        for g, w in enumerate(POOL_WINDOWS):
            sl = slice(g * POOL_GROUP_DIM, (g + 1) * POOL_GROUP_DIM)
            ws = buf[:, sl]
            s = 1
            while s < w:
                ws = ws + pltpu.roll(ws, s, 0)
                s *= 2
            ws = ws[base:, :]
            count = jnp.clip(pos + 1, 1, w).astype(F32)
            pooled = ws / count - hn[:, sl]
            outs.append(_dot(pooled.astype(BF16), pw_ref[g]))
        buf[0:base, :] = buf[CHUNK:CHUNK + base, :]
        y = (jnp.concatenate(outs, axis=1) + pb_ref[...]) * ps_ref[...]
        o_ref[k * CHUNK:(k + 1) * CHUNK, :] = jnp.where(_row_ids((CHUNK, D_MODEL), c) >= PAD, h + y, 0.0)


MOE_TM = 512
N_MOE_TILES = TP // MOE_TM
RUN_ALIGN = 16
RUN_SHIFT = 4
RUN_BIG = 64
RUN_BIG_SHIFT = 6
LOCAL_ROWS = 1280
EXPERT_TILE = 512
N_ROUTED = BATCH * (N_META + SEQ)
MAX_ROWS = (2 * N_ROUTED + N_MOE_TILES * MOE_EXPERTS * (RUN_ALIGN - 1)
            + MOE_EXPERTS * (EXPERT_TILE - 1))
NT_MAX = -(-MAX_ROWS // EXPERT_TILE)
NR = NT_MAX * EXPERT_TILE
XS_COLS = D_MODEL + LANES
INT_ROWS = 8
TOK_GATE1, TOK_GATE2, TOK_POS1, TOK_POS2, TOK_EXPERT1 = 0, 3, 6, 7, 8
assert LOCAL_ROWS >= 2 * MOE_TM + MOE_EXPERTS * (RUN_ALIGN - 1) and LOCAL_ROWS % LANES == 0


def _first_argmax(vals):
    best, idx = vals[0], jnp.zeros(vals[0].shape, jnp.int32)
    for k in range(1, len(vals)):
        better = vals[k] > best
        idx = jnp.where(better, k, idx)
        best = jnp.where(better, vals[k], best)
    return idx, best


def _softmax_rows(vals):
    m = functools.reduce(jnp.maximum, vals)
    ex = [jnp.exp(v - m) for v in vals]
    tot = functools.reduce(lambda p, q: p + q, ex)
    return [e / tot for e in ex]


def _bf16_pieces(x):
    hi = x.astype(BF16).astype(F32)
    rest = x - hi
    mid = rest.astype(BF16).astype(F32)
    lo = (rest - mid).astype(BF16).astype(F32)
    return [hi, mid, lo]


def _router_kernel(h_ref, valid_ref, g_ref, wr_ref, br_ref, lpos_ref, tok_ref, runlen_ref, runoff_ref, before):
    @pl.when(pl.program_id(0) == 0)
    def _():
        si = lax.broadcasted_iota(jnp.int32, (MOE_TM, MOE_TM), 0)
        ti = lax.broadcasted_iota(jnp.int32, (MOE_TM, MOE_TM), 1)
        before[...] = jnp.where(si < ti, 1.0, 0.0).astype(BF16)

    hn = _rms(h_ref[...], g_ref[...])
    hn_hi = hn.astype(BF16)
    hn_lo = (hn - hn_hi.astype(F32)).astype(BF16)
    by_head = _dot(hn_hi, wr_ref[...])
    logits = (by_head[:, 0:LANES] + (by_head[:, LANES:2 * LANES] + _dot(hn_lo, wr_ref[:, 0:LANES]))
              + br_ref[...])
    lt = logits.T
    p_group = _softmax_rows([lt[k:k + 1, :] for k in range(MOE_GROUPS)])
    g_sel, p_g = _first_argmax(p_group)
    fine = []
    for k in range(MOE_PER_GROUP):
        f = lt[ROUTER_EXPERT_ROW + k:ROUTER_EXPERT_ROW + k + 1, :]
        for g in range(1, MOE_GROUPS):
            r0 = ROUTER_EXPERT_ROW + g * MOE_PER_GROUP + k
            f = jnp.where(g_sel == g, lt[r0:r0 + 1, :], f)
        fine.append(f)
    q = _softmax_rows(fine)
    i1, t1 = _first_argmax(q)
    i2, t2 = _first_argmax([jnp.where(i1 == k, -1.0, q[k]) for k in range(MOE_PER_GROUP)])
    tot = t1 + t2
    gate1 = p_g * (t1 / tot)
    gate2 = p_g * (t2 / tot)

    valid = valid_ref[...] > 0
    e1 = jnp.where(valid, g_sel * MOE_PER_GROUP + i1, -1)
    e2 = jnp.where(valid, g_sel * MOE_PER_GROUP + i2, -1)
    erow = lax.broadcasted_iota(jnp.int32, (MOE_EXPERTS, MOE_TM), 0)
    hit1 = erow == e1
    hit2 = erow == e2
    onehot = jnp.where(hit1 | hit2, 1.0, 0.0)
    seen = _dot(onehot.astype(BF16), before[...])
    count = jnp.sum(onehot, axis=1, keepdims=True).astype(jnp.int32)
    runlen = lax.shift_left(lax.shift_right_logical(count + (RUN_ALIGN - 1), RUN_SHIFT), RUN_SHIFT)
    runlen_b = jnp.broadcast_to(runlen, (MOE_EXPERTS, LANES))
    ei = lax.broadcasted_iota(jnp.int32, (MOE_EXPERTS, MOE_EXPERTS), 0)
    ej = lax.broadcasted_iota(jnp.int32, (MOE_EXPERTS, MOE_EXPERTS), 1)
    runoff_b = _dot(jnp.where(ej < ei, 1.0, 0.0), runlen_b.astype(F32), HIGHEST)
    runlen_ref[...] = runlen_b
    runoff_ref[...] = runoff_b.astype(jnp.int32)
    place = seen + runoff_b[:, 0:1]
    pos1 = jnp.where(valid, jnp.sum(jnp.where(hit1, place, 0.0), axis=0, keepdims=True), -1.0)
    pos2 = jnp.where(valid, jnp.sum(jnp.where(hit2, place, 0.0), axis=0, keepdims=True), -1.0)
    r8 = lax.broadcasted_iota(jnp.int32, (INT_ROWS, MOE_TM), 0)
    lpos_ref[...] = jnp.where(r8 == 0, pos1.astype(jnp.int32),
                              jnp.where(r8 == 1, pos2.astype(jnp.int32), 0))

    rows = lax.broadcasted_iota(jnp.int32, lt.shape, 0)
    table = jnp.zeros(lt.shape, F32)
    for k, piece in enumerate(_bf16_pieces(gate1)):
        table = jnp.where(rows == TOK_GATE1 + k, piece, table)
    for k, piece in enumerate(_bf16_pieces(gate2)):
        table = jnp.where(rows == TOK_GATE2 + k, piece, table)
    table = jnp.where(rows == TOK_POS1, pos1, jnp.where(rows == TOK_POS2, pos2, table))
    table = jnp.where(rows == TOK_EXPERT1, e1.astype(F32), table)
    tok_ref[...] = table.T


def _router(h, valid, gain, wr, br):
    row = lambda i: (i, 0)
    col = lambda i: (0, i)
    const = lambda i: (0, 0)
    wr_hi = wr.astype(BF16)
    wr_lo = (wr - wr_hi.astype(F32)).astype(BF16)
    return pl.pallas_call(
        _router_kernel,
        grid=(N_MOE_TILES,),
        in_specs=[
            pl.BlockSpec((MOE_TM, D_MODEL), row),
            pl.BlockSpec((1, MOE_TM), col),
            pl.BlockSpec((1, D_MODEL), const),
            pl.BlockSpec((D_MODEL, 2 * LANES), const),
            pl.BlockSpec((1, LANES), const),
        ],
        out_specs=[pl.BlockSpec((INT_ROWS, MOE_TM), col),
                   pl.BlockSpec((MOE_TM, LANES), row),
                   pl.BlockSpec((MOE_EXPERTS, LANES), row),
                   pl.BlockSpec((MOE_EXPERTS, LANES), row)],
        out_shape=[jax.ShapeDtypeStruct((INT_ROWS, TP), jnp.int32),
                   jax.ShapeDtypeStruct((TP, LANES), F32),
                   jax.ShapeDtypeStruct((N_MOE_TILES * MOE_EXPERTS, LANES), jnp.int32),
                   jax.ShapeDtypeStruct((N_MOE_TILES * MOE_EXPERTS, LANES), jnp.int32)],
        scratch_shapes=[pltpu.VMEM((MOE_TM, MOE_TM), BF16)],
        compiler_params=pltpu.CompilerParams(
            dimension_semantics=("arbitrary",), vmem_limit_bytes=VMEM_LIMIT),
        name="moe_router",
    )(h, valid, gain, jnp.concatenate([wr_hi, wr_lo], axis=1), br)


def _moe_plan(runlen, runoff):
    i32 = jnp.int32
    lens = runlen[:, 0].reshape(N_MOE_TILES, MOE_EXPERTS)
    total = jnp.sum(lens, axis=0)
    padded = (total + (EXPERT_TILE - 1)) // EXPERT_TILE * EXPERT_TILE
    ends = jnp.cumsum(padded)
    base = ends - padded
    goff = base[None, :] + jnp.cumsum(lens, axis=0) - lens
    nval = ends[-1] // EXPERT_TILE
    tidx = jnp.minimum(jnp.arange(NT_MAX, dtype=i32), nval - 1)
    texp = jnp.minimum(jnp.sum(tidx[:, None] >= (ends // EXPERT_TILE)[None, :], axis=1), MOE_EXPERTS - 1)
    return dict(
        goff=goff.reshape(-1).astype(i32), lens=lens.reshape(-1).astype(i32),
        loff=runoff[:, 0].astype(i32), gap_start=(base + total).astype(i32),
        gap_len=(padded - total).astype(i32), nval=nval.reshape(1).astype(i32),
        tidx=tidx.astype(i32), texp=texp.astype(i32),
        nbig=jnp.sum(lens // RUN_BIG, axis=1).astype(i32),
        nsmall=jnp.sum(lens % RUN_BIG // RUN_ALIGN, axis=1).astype(i32))


def _aligned(x):
    return pl.multiple_of(x, RUN_ALIGN)


def _for_each_run(step, lens_ref, fn):
    def expert_body(e, carry):
        k = step * MOE_EXPERTS + e
        nbig = lax.shift_right_logical(lens_ref[k], RUN_BIG_SHIFT)
        rest = nbig * RUN_BIG

        def big_body(q, c):
            fn(k, q * RUN_BIG, RUN_BIG)
            return c

        def small_body(q, c):
            fn(k, rest + q * RUN_ALIGN, RUN_ALIGN)
            return c

        lax.fori_loop(0, nbig, big_body, 0)
        lax.fori_loop(0, lax.shift_right_logical(lens_ref[k] - rest, RUN_SHIFT), small_body, 0)
        return carry

    lax.fori_loop(0, MOE_EXPERTS, expert_body, 0)


def _wait_runs(copy, step, nbig_ref, nsmall_ref):
    def big_body(q, c):
        copy(RUN_BIG).wait()
        return c

    def small_body(q, c):
        copy(RUN_ALIGN).wait()
        return c

    lax.fori_loop(0, nbig_ref[step], big_body, 0)
    lax.fori_loop(0, nsmall_ref[step], small_body, 0)


def _dispatch_kernel(goff_ref, lens_ref, loff_ref, gaps_ref, gapl_ref, nval_ref, nbig_ref, nsmall_ref,
                     h_ref, g_ref, lpos_ref, tok_ref, xs_ref, local, zeros, sems):
    i = pl.program_id(0)
    last = pl.num_programs(0) - 1
    slot = lax.rem(i, 2)

    def run_copy(s, src, dst, rows):
        return pltpu.make_async_copy(local.at[s, pl.ds(_aligned(src), rows), :],
                                     xs_ref.at[pl.ds(_aligned(dst), rows), :], sems.at[s])

    def wait_tile(step):
        s = lax.rem(step, 2)
        _wait_runs(lambda rows: run_copy(s, 0, 0, rows), step, nbig_ref, nsmall_ref)

    @pl.when(i >= 2)
    def _():
        wait_tile(i - 2)

    hn = _rms(h_ref[...], g_ref[...]).astype(BF16)
    riota = lax.broadcasted_iota(jnp.int32, (LOCAL_ROWS, MOE_TM), 0)
    sel = jnp.where((riota == lpos_ref[0:1, :]) | (riota == lpos_ref[1:2, :]), 1.0, 0.0).astype(BF16)
    lane = lax.broadcasted_iota(jnp.int32, (MOE_TM, LANES), 1)
    extra = jnp.where((lane < TOK_POS1) | (lane == TOK_EXPERT1), tok_ref[...], 0.0).astype(BF16)
    local[slot] = _dot(sel, jnp.concatenate([hn, extra], axis=1)).astype(BF16)

    _for_each_run(i, lens_ref,
                  lambda k, q, rows: run_copy(slot, loff_ref[k] + q, goff_ref[k] + q, rows).start())

    @pl.when(i == last)
    def _():
        zeros[...] = jnp.zeros_like(zeros)
        zsem = sems.at[2]

        def gap_copy(dst):
            return pltpu.make_async_copy(zeros.at[pl.ds(0, RUN_ALIGN), :],
                                         xs_ref.at[pl.ds(_aligned(dst), RUN_ALIGN), :], zsem)

        def tile_copy(t):
            return pltpu.make_async_copy(
                zeros, xs_ref.at[pl.ds(pl.multiple_of(t * EXPERT_TILE, EXPERT_TILE), EXPERT_TILE), :], zsem)

        def expert_body(e, n):
            g = lax.shift_right_logical(gapl_ref[e], RUN_SHIFT)

            def body(q, c):
                gap_copy(gaps_ref[e] + q * RUN_ALIGN).start()
                return c

            lax.fori_loop(0, g, body, 0)
            return n + g

        ngap = lax.fori_loop(0, MOE_EXPERTS, expert_body, 0)

        def tail_start(t, c):
            tile_copy(t).start()
            return c

        lax.fori_loop(nval_ref[0], NT_MAX, tail_start, 0)

        def gap_wait(k, c):
            gap_copy(0).wait()
            return c

        lax.fori_loop(0, ngap, gap_wait, 0)

        def tail_wait(t, c):
            tile_copy(t).wait()
            return c

        lax.fori_loop(nval_ref[0], NT_MAX, tail_wait, 0)

        @pl.when(i >= 1)
        def _():
            wait_tile(i - 1)

        wait_tile(i)


def _dispatch(plan, h, gain, lpos, tok):
    row = lambda i, *_: (i, 0)
    return pl.pallas_call(
        _dispatch_kernel,
        grid_spec=pltpu.PrefetchScalarGridSpec(
            num_scalar_prefetch=8,
            grid=(N_MOE_TILES,),
            in_specs=[
                pl.BlockSpec((MOE_TM, D_MODEL), row),
                pl.BlockSpec((1, D_MODEL), lambda i, *_: (0, 0)),
                pl.BlockSpec((INT_ROWS, MOE_TM), lambda i, *_: (0, i)),
                pl.BlockSpec((MOE_TM, LANES), row),
            ],
            out_specs=pl.BlockSpec(memory_space=pl.ANY),
            scratch_shapes=[pltpu.VMEM((2, LOCAL_ROWS, XS_COLS), BF16),
                            pltpu.VMEM((EXPERT_TILE, XS_COLS), BF16),
                            pltpu.SemaphoreType.DMA((3,))],
        ),
        out_shape=jax.ShapeDtypeStruct((NR, XS_COLS), BF16),
        compiler_params=pltpu.CompilerParams(
            dimension_semantics=("arbitrary",), vmem_limit_bytes=VMEM_LIMIT),
        name="moe_dispatch",
    )(plan["goff"], plan["lens"], plan["loff"], plan["gap_start"], plan["gap_len"], plan["nval"],
      plan["nbig"], plan["nsmall"], h, gain, lpos, tok)


def _experts_kernel(tidx_ref, texp_ref, nval_ref, xs_ref, wg_ref, wu_ref, wd_ref, ys_ref, wg16, wu16, wd16):
    j = pl.program_id(0)
    active = j < nval_ref[0]
    new_expert = (j == 0) | (texp_ref[j] != texp_ref[jnp.maximum(j - 1, 0)])

    @pl.when(active & new_expert)
    def _():
        wg16[...] = wg_ref[0].astype(BF16)
        wu16[...] = wu_ref[0].astype(BF16)
        wd16[...] = wd_ref[0].astype(BF16)

    @pl.when(active)
    def _():
        x = xs_ref[:, 0:D_MODEL]
        extra = xs_ref[:, D_MODEL:XS_COLS].astype(F32)
        lane = lax.broadcasted_iota(jnp.int32, extra.shape, 1)

        def lanes(lo, hi):
            return jnp.sum(jnp.where((lane >= lo) & (lane < hi), extra, 0.0), axis=1, keepdims=True)

        first = lanes(TOK_EXPERT1, TOK_EXPERT1 + 1) == texp_ref[j].astype(F32)
        gate = jnp.where(first, lanes(TOK_GATE1, TOK_GATE2), lanes(TOK_GATE2, TOK_POS1))
        a = _dot(x, wg16[...])
        b = _dot(x, wu16[...])
        y = _dot((_silu(a) * b).astype(BF16), wd16[...])
        ys_ref[...] = (gate * y).astype(BF16)

    @pl.when(jnp.logical_not(active))
    def _():
        ys_ref[...] = jnp.zeros_like(ys_ref)


def _experts(plan, xs, layer, wg, wu, wd):
    tile = lambda j, tidx, texp, nval: (tidx[j], 0)
    expert = lambda j, tidx, texp, nval: (layer, texp[j], 0, 0)
    return pl.pallas_call(
        _experts_kernel,
        grid_spec=pltpu.PrefetchScalarGridSpec(
            num_scalar_prefetch=3,
            grid=(NT_MAX,),
            in_specs=[
                pl.BlockSpec((EXPERT_TILE, XS_COLS), tile),
                pl.BlockSpec((None, 1, D_MODEL, D_EXPERT), expert),
                pl.BlockSpec((None, 1, D_MODEL, D_EXPERT), expert),
                pl.BlockSpec((None, 1, D_EXPERT, D_MODEL), expert),
            ],
            out_specs=pl.BlockSpec((EXPERT_TILE, D_MODEL), lambda j, *_: (j, 0)),
            scratch_shapes=[pltpu.VMEM((D_MODEL, D_EXPERT), BF16),
                            pltpu.VMEM((D_MODEL, D_EXPERT), BF16),
                            pltpu.VMEM((D_EXPERT, D_MODEL), BF16)],
        ),
        out_shape=jax.ShapeDtypeStruct((NR, D_MODEL), BF16),
        compiler_params=pltpu.CompilerParams(
            dimension_semantics=("arbitrary",), vmem_limit_bytes=VMEM_LIMIT),
        name="moe_experts",
    )(plan["tidx"], plan["texp"], plan["nval"], xs, wg, wu, wd)


def _combined_rows(goff_ref, lens_ref, loff_ref, nbig_ref, nsmall_ref, h_ref, tok_ref, ys_ref, local, sems):
    i = pl.program_id(0)
    slot = lax.rem(i, 2)

    def run_copy(s, src, dst, rows):
        return pltpu.make_async_copy(ys_ref.at[pl.ds(_aligned(src), rows), :],
                                     local.at[s, pl.ds(_aligned(dst), rows), :], sems.at[s])

    def start_tile(step):
        s = lax.rem(step, 2)
        _for_each_run(step, lens_ref,
                      lambda k, q, rows: run_copy(s, goff_ref[k] + q, loff_ref[k] + q, rows).start())

    @pl.when(i == 0)
    def _():
        local[...] = jnp.zeros_like(local)
        start_tile(0)

    @pl.when(i < pl.num_programs(0) - 1)
    def _():
        start_tile(i + 1)

    _wait_runs(lambda rows: run_copy(slot, 0, 0, rows), i, nbig_ref, nsmall_ref)
    tok = tok_ref[...]
    p1 = tok[:, TOK_POS1:TOK_POS1 + 1].astype(jnp.int32)
    p2 = tok[:, TOK_POS2:TOK_POS2 + 1].astype(jnp.int32)
    ciota = lax.broadcasted_iota(jnp.int32, (MOE_TM, LOCAL_ROWS), 1)
    sel = jnp.where((ciota == p1) | (ciota == p2), 1.0, 0.0).astype(BF16)
    return h_ref[...] + _dot(sel, local[slot])


def _combine_pool_kernel(goff_ref, lens_ref, loff_ref, nbig_ref, nsmall_ref, h_ref, tok_ref,
                         g_ref, pw_ref, pb_ref, ps_ref, ys_ref, o_ref, local, mixed, buf, sems):
    mixed[...] = _combined_rows(goff_ref, lens_ref, loff_ref, nbig_ref, nsmall_ref, h_ref, tok_ref, ys_ref,
                                local, sems)
    _pool_rows(pl.program_id(0), mixed, g_ref, pw_ref, pb_ref, ps_ref, o_ref, buf)


def _combine_final_kernel(goff_ref, lens_ref, loff_ref, nbig_ref, nsmall_ref, h_ref, tok_ref, g_ref, ys_ref,
                          o_ref, local, stage, sems, out_sems):
    i = pl.program_id(0)
    last = pl.num_programs(0) - 1
    slot = lax.rem(i, 2)
    per_tile = MOE_TM // CHUNK

    def for_each_seq_chunk(step, fn):
        s = lax.rem(step, 2)
        for k in range(per_tile):
            chunk = step * per_tile + k
            c = lax.rem(chunk, NCHUNK)
            dst = (lax.div(chunk, NCHUNK) * (SEQ // CHUNK) + c - 1) * CHUNK

            @pl.when(c > 0)
            def _():
                fn(pltpu.make_async_copy(stage.at[s, pl.ds(k * CHUNK, CHUNK), :],
                                         o_ref.at[pl.ds(pl.multiple_of(dst, CHUNK), CHUNK), :],
                                         out_sems.at[s]))

    @pl.when(i >= 2)
    def _():
        for_each_seq_chunk(i - 2, lambda copy: copy.wait())

    out = _combined_rows(goff_ref, lens_ref, loff_ref, nbig_ref, nsmall_ref, h_ref, tok_ref, ys_ref,
                         local, sems)
    stage[slot] = _rms(out, g_ref[...])
    for_each_seq_chunk(i, lambda copy: copy.start())

    @pl.when(i == last)
    def _():
        @pl.when(i >= 1)
        def _():
            for_each_seq_chunk(i - 1, lambda copy: copy.wait())

        for_each_seq_chunk(i, lambda copy: copy.wait())


def _combine(plan, h, tok, ys, pool=None, final_gain=None):
    final = final_gain is not None
    row = lambda i, *_: (i, 0)
    whole = lambda a: pl.BlockSpec(a.shape, lambda i, *_: (0,) * a.ndim)
    extra = [final_gain] if final else list(pool)
    scratch = [pltpu.VMEM((2, LOCAL_ROWS, D_MODEL), BF16)]
    if final:
        scratch.append(pltpu.VMEM((2, MOE_TM, D_MODEL), F32))
    else:
        scratch += [pltpu.VMEM((MOE_TM, D_MODEL), F32), pltpu.VMEM((CHUNK + POOL_MAX_WINDOW, D_MODEL), F32)]
    return pl.pallas_call(
        _combine_final_kernel if final else _combine_pool_kernel,
        grid_spec=pltpu.PrefetchScalarGridSpec(
            num_scalar_prefetch=5,
            grid=(N_MOE_TILES,),
            in_specs=([pl.BlockSpec((MOE_TM, D_MODEL), row), pl.BlockSpec((MOE_TM, LANES), row)]
                      + [whole(a) for a in extra] + [pl.BlockSpec(memory_space=pl.ANY)]),
            out_specs=(pl.BlockSpec(memory_space=pl.ANY) if final
                       else pl.BlockSpec((MOE_TM, D_MODEL), row)),
            scratch_shapes=scratch + [pltpu.SemaphoreType.DMA((2,))] * (2 if final else 1),
        ),
        out_shape=jax.ShapeDtypeStruct((BATCH * SEQ if final else TP, D_MODEL), F32),
        compiler_params=pltpu.CompilerParams(
            dimension_semantics=("arbitrary",), vmem_limit_bytes=VMEM_LIMIT),
        name="moe_combine_final" if final else "moe_combine_pool",
    )(plan["goff"], plan["lens"], plan["loff"], plan["nbig"], plan["nsmall"], h, tok, *extra, ys)


def _row(v):
    return v.reshape(1, -1).astype(F32)


def _pad_lanes(v):
    return jnp.pad(_row(v), ((0, 0), (0, LANES - v.shape[-1])))


def _pair_blockdiag(w):
    w = w.reshape(LRU_WIDTH // LANES, 2, 64, 64)
    z = jnp.zeros_like(w[:, 0])
    top = jnp.concatenate([w[:, 0], z], axis=2)
    bot = jnp.concatenate([z, w[:, 1]], axis=2)
    return jnp.concatenate([top, bot], axis=1).astype(BF16)


def _moe_layer(h, layer, pool, final_gain, valid, ffn_norm, rgw, rgb, rew, reb, wg, wu, wd):
    wr = jnp.zeros((D_MODEL, LANES), F32)
    wr = wr.at[:, 0:MOE_GROUPS].set(rgw[layer])
    wr = wr.at[:, ROUTER_EXPERT_ROW:ROUTER_EXPERT_ROW + MOE_EXPERTS].set(rew[layer])
    br = jnp.zeros((1, LANES), F32)
    br = br.at[0, 0:MOE_GROUPS].set(rgb[layer])
    br = br.at[0, ROUTER_EXPERT_ROW:ROUTER_EXPERT_ROW + MOE_EXPERTS].set(reb[layer])
    gain = _row(ffn_norm[layer])
    lpos, tok, runlen, runoff = _router(h, valid, gain, wr, br)
    plan = _moe_plan(runlen, runoff)
    xs = _dispatch(plan, h, gain, lpos, tok)
    ys = _experts(plan, xs, layer, wg, wu, wd)
    return _combine(plan, h, tok, ys, pool, final_gain)


def kernel(x, meta_tokens, norm_final, mix_norm_even, w_in, ssd_conv_w, ssd_conv_b, ssd_dt_bias, ssd_a_log, ssd_d, ssd_norm, lru_conv_w, lru_conv_b, lru_w_a, lru_b_a, lru_w_x, lru_b_x, lru_lambda, w_out, mix_norm_odd, pool_w, pool_b, pool_scale, ffn_norm, router_group_w, router_group_b, router_expert_w, router_expert_b, expert_w_gate, expert_w_up, expert_w_down):
    meta_chunk = jnp.concatenate([jnp.zeros((PAD, D_MODEL), F32), meta_tokens.astype(F32)], axis=0)
    rows = jnp.arange(CHUNK)
    meta_chunk = meta_chunk[(rows % 8) * SEG + rows // 8]
    valid = ((jnp.arange(TP, dtype=jnp.int32) % LP) >= PAD).astype(jnp.int32).reshape(1, TP)
    moe_args = (valid, ffn_norm, router_group_w, router_group_b, router_expert_w, router_expert_b,
                expert_w_gate, expert_w_up, expert_w_down)

    wi = w_in[0]
    w_proj = jnp.concatenate(
        [wi[:, 0:1024], wi[:, 1024:2560], wi[:, 2576:3600], wi[:, 3600:4624], wi[:, 2560:2576],
         jnp.zeros((D_MODEL, LANES - SSD_HEADS), F32)], axis=1).astype(BF16)
    piece_head = jnp.where(jnp.arange(LANES) < HEAD_PIECES * SSD_HEADS, jnp.arange(LANES) % SSD_HEADS, -1)
    expand = (piece_head[:, None] == (jnp.arange(SSD_WIDTH) // SSD_HEAD_DIM)[None, :]).astype(BF16)
    params = [
        _row(mix_norm_even[0]), w_proj,
        ssd_conv_w[0].astype(F32), _row(ssd_conv_b[0]), _pad_lanes(ssd_dt_bias[0]), _pad_lanes(ssd_a_log[0]),
        _row(jnp.repeat(ssd_d[0], SSD_HEAD_DIM)), _row(ssd_norm[0]),
        lru_conv_w[0].astype(F32), _row(lru_conv_b[0]),
        jnp.concatenate([_pair_blockdiag(lru_w_a[0]), _pair_blockdiag(lru_w_x[0])], axis=2),
        _row(lru_b_a[0]), _row(lru_b_x[0]), _row(lru_lambda[0]),
        w_out[0].astype(BF16), expand,
    ]
    h = _mixer(x.astype(F32), meta_chunk, params).reshape(TP, D_MODEL)
    pool = (_row(mix_norm_odd[0]), pool_w[0].astype(BF16), _row(pool_b[0]), _row(pool_scale[0]))
    h = _moe_layer(h, 0, pool, None, *moe_args)
    out = _moe_layer(h, 1, None, _row(norm_final), *moe_args)
    return out.reshape(BATCH, SEQ, D_MODEL)
```

```python
import functools
import math

import jax
import jax.numpy as jnp
from jax import lax
from jax.experimental import pallas as pl
from jax.experimental.pallas import tpu as pltpu

F32 = jnp.float32
BF16 = jnp.bfloat16
HIGHEST = lax.Precision.HIGHEST

D_MODEL = 1024
BATCH = 8
SEQ = 2048
N_META = 16
RMS_EPS = 1e-6
CONV_WIDTH = 4
CHUNK = 128
PAD = CHUNK - N_META
LP = PAD + N_META + SEQ
NCHUNK = LP // CHUNK
TP = BATCH * LP

SSD_HEADS = 16
SSD_HEAD_DIM = 64
SSD_WIDTH = 1024
SSD_GROUPS = 2
SSD_STATE = 128
SSD_GROUP_WIDTH = SSD_WIDTH // SSD_GROUPS
SSD_CONV_DIM = SSD_WIDTH + 2 * SSD_GROUPS * SSD_STATE
LRU_WIDTH = 1024
LRU_C = 8.0
LANES = 128
POOL_WINDOWS = (2, 4, 8, 16)
POOL_GROUP_DIM = 256
POOL_MAX_WINDOW = 16
MOE_GROUPS = 4
MOE_PER_GROUP = 4
MOE_EXPERTS = 16
D_EXPERT = 512
ROUTER_EXPERT_ROW = 8

VMEM_LIMIT = 56 * 1024 * 1024


def _dot(a, b, precision=None):
    return jnp.dot(a, b, preferred_element_type=F32, precision=precision)


def _rms(x, gain):
    ms = jnp.mean(x * x, axis=-1, keepdims=True)
    return x * lax.rsqrt(ms + RMS_EPS) * gain


def _silu(x):
    return x * jax.nn.sigmoid(x)


def _softplus(x):
    return jnp.maximum(x, 0.0) + jnp.log1p(jnp.exp(-jnp.abs(x)))


def _row_ids(shape, chunk):
    return lax.broadcasted_iota(jnp.int32, shape, 0) + chunk * CHUNK


MIX_B = 2
N_MIX_ITEMS = (BATCH // MIX_B) * NCHUNK
_Z0, _X0, _G0, _L0, _T0, _PEND = 0, 1024, 2560, 3584, 4608, 4736
HEAD_PIECES = 3
PROJ_PIECE = 256


def _carried(c, state):
    return jnp.where(jnp.full(state.shape, c, jnp.int32) == 0, 0.0, state)


SEG = CHUNK // 8


def _time_ids(shape, axis=0):
    p = lax.broadcasted_iota(jnp.int32, shape, axis)
    return lax.bitwise_and(p, 7) * SEG + lax.shift_right_logical(p, 3)


def _load_interleaved(slab, src):
    cols = []
    for j in range(D_MODEL // LANES):
        slab[j] = src[:, j * LANES:(j + 1) * LANES]
        cols.append(jnp.concatenate([slab[j, pl.ds(k, 8, stride=SEG), :] for k in range(SEG)], axis=0))
    return jnp.concatenate(cols, axis=1)


def _store_time_order(slab, dst, value):
    for j in range(D_MODEL // LANES):
        for k in range(SEG):
            slab[j, pl.ds(k, 8, stride=SEG), :] = value[k * 8:(k + 1) * 8, j * LANES:(j + 1) * LANES]
        dst[:, j * LANES:(j + 1) * LANES] = slab[j]


def _causal_conv(hist, c, x, w_ref, b_ref):
    taps = CONV_WIDTH - 1
    prev = _carried(c, hist[...])
    hist[...] = x[(SEG - taps) * 8:, :]
    first_segment = lax.broadcasted_iota(jnp.int32, (8, x.shape[1]), 0) == 0
    groups = [x[k * 8:(k + 1) * 8, :] for k in range(SEG)]
    wrapped = {i: jnp.where(first_segment, pltpu.roll(prev[(taps - i) * 8:(taps - i + 1) * 8, :], 1, 0),
                            pltpu.roll(groups[SEG - i], 1, 0)) for i in range(1, taps + 1)}
    out = []
    for k in range(SEG):
        acc = b_ref[...] + w_ref[taps:taps + 1, :] * groups[k]
        for j in range(1, taps + 1):
            earlier = groups[k - j] if k >= j else wrapped[j - k]
            acc = acc + w_ref[taps - j:taps - j + 1, :] * earlier
        out.append(acc)
    return jnp.concatenate(out, axis=0)


def _head_pieces(v):
    lane = lax.broadcasted_iota(jnp.int32, v.shape, 1)
    rest = jnp.where(lane < SSD_HEADS, v, 0.0)
    packed = None
    for k in range(HEAD_PIECES):
        piece = rest.astype(BF16).astype(F32)
        rest = rest - piece
        moved = piece if k == 0 else pltpu.roll(piece, k * SSD_HEADS, 1)
        packed = moved if packed is None else packed + moved
    return packed.astype(BF16)


def _expand_heads(v, exp_ref):
    return _dot(_head_pieces(v), exp_ref[...])


def _cumsum_over_time(order, v):
    sums = _dot(order, _head_pieces(v))
    total = sums
    for k in range(1, HEAD_PIECES):
        total = total + pltpu.roll(sums, LANES - k * SSD_HEADS, 1)
    lane = lax.broadcasted_iota(jnp.int32, v.shape, 1)
    return jnp.where(lane < SSD_HEADS, total, 0.0)


def _ssd_chunk(c, z, xbc, dt_raw, hist, st, cw_ref, cb_ref, dtb_ref, alog_ref, dsk_ref, ng_ref, exp_ref):
    xc = _silu(_causal_conv(hist, c, xbc, cw_ref, cb_ref))
    yield
    xs = xc[:, 0:SSD_WIDTH]
    valid_s = _time_ids((CHUNK, 2 * SSD_STATE)) + c * CHUNK >= PAD
    bm = jnp.where(valid_s, xc[:, SSD_WIDTH:SSD_WIDTH + 2 * SSD_STATE], 0.0)
    cm = jnp.where(valid_s, xc[:, SSD_WIDTH + 2 * SSD_STATE:], 0.0)

    li = _time_ids((CHUNK, CHUNK), 0)
    si = _time_ids((CHUNK, CHUNK), 1)
    causal = si <= li
    dt = _softplus(dt_raw + dtb_ref[...])
    dt = jnp.where(li + c * CHUNK >= PAD, dt, 0.0)
    adt = dt * (-jnp.exp(alog_ref[...]))
    a_cs = _cumsum_over_time(jnp.where(causal, 1.0, 0.0).astype(BF16), adt)
    a_last = a_cs[CHUNK - 1:CHUNK, :]
    ea = jnp.exp(a_cs)
    dt_x = _expand_heads(dt, exp_ref)
    w_x = _expand_heads(jnp.exp(a_last - a_cs) * dt, exp_ref)
    ea_x = _expand_heads(ea, exp_ref)
    xdt = xs * dt_x
    a_cs_t = a_cs.T
    yield
    lane = lax.broadcasted_iota(jnp.int32, (CHUNK, LANES), 1)

    ys = []
    for g in range(SSD_GROUPS):
        gsl = slice(g * SSD_GROUP_WIDTH, (g + 1) * SSD_GROUP_WIDTH)
        bg = bm[:, g * SSD_STATE:(g + 1) * SSD_STATE]
        cg16 = cm[:, g * SSD_STATE:(g + 1) * SSD_STATE].astype(BF16)
        cbm = lax.dot_general(cg16, bg.astype(BF16), (((1,), (1,)), ((), ())),
                              preferred_element_type=F32)
        s_in = _carried(c, st[:, gsl])
        y_off = _dot(cg16, s_in.astype(BF16)) * ea_x[:, gsl]
        s_new = _dot(bg.T.astype(BF16), (w_x[:, gsl] * xs[:, gsl]).astype(BF16))
        st[:, gsl] = ea_x[CHUNK - 1:CHUNK, gsl] * s_in + s_new
        for j in range(SSD_GROUP_WIDTH // LANES):
            xp = xdt[:, g * SSD_GROUP_WIDTH + j * LANES:
                     g * SSD_GROUP_WIDTH + (j + 1) * LANES].astype(BF16)
            weights, values = [], []
            for hh in range(LANES // SSD_HEAD_DIM):
                hd = g * (SSD_HEADS // SSD_GROUPS) + j * (LANES // SSD_HEAD_DIM) + hh
                seg = a_cs[:, hd:hd + 1] - a_cs_t[hd:hd + 1, :]
                dec = jnp.exp(jnp.where(causal, seg, -1e30))
                weights.append((cbm * dec).astype(BF16))
                own = (lane >= hh * SSD_HEAD_DIM) & (lane < (hh + 1) * SSD_HEAD_DIM)
                values.append(jnp.where(own, xp, jnp.zeros_like(xp)))
            y_diag = _dot(jnp.concatenate(weights, axis=1), jnp.concatenate(values, axis=0))
            ys.append(y_diag + y_off[:, j * LANES:(j + 1) * LANES])
            if j % 2:
                yield
    y = jnp.concatenate(ys, axis=1) + dsk_ref[...] * xs
    y = y * _silu(z)
    normed = []
    for g in range(SSD_GROUPS):
        yg = y[:, g * SSD_GROUP_WIDTH:(g + 1) * SSD_GROUP_WIDTH]
        normed.append(yg * lax.rsqrt(jnp.mean(yg * yg, axis=-1, keepdims=True) + RMS_EPS))
    return jnp.concatenate(normed, axis=1) * ng_ref[...]


def _lru_chunk(c, gt, lin, hist, lc, lcw_ref, lcb_ref, wax_ref, ba_ref, bx_ref, lam_ref):
    xb = _causal_conv(hist, c, lin, lcw_ref, lcb_ref)
    yield
    xb16 = xb.astype(BF16)
    both = [_dot(xb16[:, j * LANES:(j + 1) * LANES], wax_ref[j]) for j in range(LRU_WIDTH // LANES)]
    r = jax.nn.sigmoid(jnp.concatenate([p[:, 0:LANES] for p in both], axis=1) + ba_ref[...])
    ig = jax.nn.sigmoid(jnp.concatenate([p[:, LANES:2 * LANES] for p in both], axis=1) + bx_ref[...])
    log_a = (-LRU_C * _softplus(-lam_ref[...])) * r
    a = jnp.exp(log_a)
    mult = jnp.sqrt(jnp.tanh(-log_a) * (a * a + 1.0))
    grow = _time_ids((CHUNK, LRU_WIDTH)) + c * CHUNK
    mult = jnp.where(grow == PAD, 1.0, mult)
    u = jnp.where(grow >= PAD, mult * (ig * xb), 0.0)
    yield

    local, decay = [], []
    for k in range(SEG):
        a8 = a[k * 8:(k + 1) * 8, :]
        u8 = u[k * 8:(k + 1) * 8, :]
        local.append(u8 if k == 0 else a8 * local[-1] + u8)
        decay.append(a8 if k == 0 else a8 * decay[-1])
        if k == SEG // 2:
            yield
    r8 = lax.broadcasted_iota(jnp.int32, (8, LRU_WIDTH), 0)
    end, span = local[-1], decay[-1]
    for d in (1, 2, 4):
        end_sh = jnp.where(r8 >= d, pltpu.roll(end, d, 0), 0.0)
        span_sh = jnp.where(r8 >= d, pltpu.roll(span, d, 0), 1.0)
        end = span * end_sh + end
        span = span * span_sh
    carry = _carried(c, lc[0:1, :])
    end = end + span * carry
    lc[0:1, :] = end[7:8, :]
    entering = jnp.where(r8 >= 1, pltpu.roll(end, 1, 0), carry)
    hs = [local[k] + decay[k] * entering for k in range(SEG)]
    gelu = 0.5 * gt * (1.0 + jnp.tanh(math.sqrt(2.0 / math.pi) * (gt + 0.044715 * (gt * gt * gt))))
    return jnp.concatenate(hs, axis=0) * gelu


def _mixer_kernel(x_ref, meta_ref, gin_ref, wproj_ref,
                  cw_ref, cb_ref, dtb_ref, alog_ref, dsk_ref, ng_ref,
                  lcw_ref, lcb_ref, wax_ref, ba_ref, bx_ref, lam_ref,
                  wout_ref, exp_ref, o_ref, h_cur, h_next, p_cur, p_next, hist_x, hist_l, st, lc,
                  slab_in, slab_out):
    s = pl.program_id(0)
    p_chunk = lax.rem(jnp.minimum(s, N_MIX_ITEMS - 1), NCHUNK)
    c = lax.rem(jnp.maximum(s - 1, 0), NCHUNK)

    @pl.when(s == 0)
    def _():
        h_next[...] = jnp.zeros_like(h_next)
        p_next[...] = jnp.zeros_like(p_next)
        hist_x[...] = jnp.zeros_like(hist_x)
        hist_l[...] = jnp.zeros_like(hist_l)
        st[...] = jnp.zeros_like(st)
        lc[...] = jnp.zeros_like(lc)

    h_cur[...] = h_next[...]
    p_cur[...] = p_next[...]

    from_meta = jnp.full((CHUNK, D_MODEL), p_chunk, jnp.int32) == 0
    for b in range(MIX_B):
        h_next[b * CHUNK:(b + 1) * CHUNK, :] = jnp.where(
            from_meta, meta_ref[...], _load_interleaved(slab_in, x_ref.at[b]))
    hn = _rms(h_next[...], gin_ref[...]).astype(BF16)
    pieces = [(lo, min(lo + PROJ_PIECE, _PEND)) for lo in range(0, _PEND, PROJ_PIECE)]

    def project_piece():
        if pieces:
            lo, hi = pieces.pop(0)
            p_next[:, lo:hi] = _dot(hn, wproj_ref[:, lo:hi])

    def rows(b):
        return slice(b * CHUNK, (b + 1) * CHUNK)

    ssd = [_ssd_chunk(c, p_cur[rows(b), _Z0:_X0], p_cur[rows(b), _X0:_G0], p_cur[rows(b), _T0:_PEND],
                      hist_x.at[b], st.at[b], cw_ref, cb_ref, dtb_ref, alog_ref, dsk_ref, ng_ref, exp_ref)
           for b in range(MIX_B)]
    lru = [_lru_chunk(c, p_cur[rows(b), _G0:_L0], p_cur[rows(b), _L0:_T0], hist_l.at[b], lc.at[b],
                      lcw_ref, lcb_ref, wax_ref, ba_ref, bx_ref, lam_ref)
           for b in range(MIX_B)]

    def finish(stages):
        while True:
            try:
                next(stages)
            except StopIteration as done:
                return done.value
            project_piece()

    h = h_cur[...]
    project_piece()
    mixed = []
    for b in range(MIX_B):
        y_ssd = finish(ssd[b])
        y_lru = finish(lru[b])
        mixed.append(jnp.concatenate([y_ssd, y_lru], axis=1).astype(BF16))
    while pieces:
        project_piece()
    out = _dot(jnp.concatenate(mixed, axis=0), wout_ref[...]) + h
    keep = _time_ids((CHUNK, D_MODEL)) + c * CHUNK >= PAD
    for b in range(MIX_B):
        _store_time_order(slab_out, o_ref.at[b], jnp.where(keep, out[b * CHUNK:(b + 1) * CHUNK, :], 0.0))


def _mixer(x, meta_chunk, params):
    full = lambda a: pl.BlockSpec(a.shape, lambda s: (0,) * a.ndim, pipeline_mode=pl.Buffered(1))

    def x_block(s):
        item = jnp.minimum(s, N_MIX_ITEMS - 1)
        return item // NCHUNK, jnp.maximum(item % NCHUNK - 1, 0), 0

    def out_block(s):
        item = jnp.maximum(s - 1, 0)
        return item // NCHUNK, item % NCHUNK, 0

    return pl.pallas_call(
        _mixer_kernel,
        grid=(N_MIX_ITEMS + 1,),
        in_specs=[pl.BlockSpec((MIX_B, CHUNK, D_MODEL), x_block), full(meta_chunk)] + [full(p) for p in params],
        out_specs=pl.BlockSpec((MIX_B, CHUNK, D_MODEL), out_block),
        out_shape=jax.ShapeDtypeStruct((BATCH, LP, D_MODEL), F32),
        scratch_shapes=[
            pltpu.VMEM((MIX_B * CHUNK, D_MODEL), F32),
            pltpu.VMEM((MIX_B * CHUNK, D_MODEL), F32),
            pltpu.VMEM((MIX_B * CHUNK, _PEND), F32),
            pltpu.VMEM((MIX_B * CHUNK, _PEND), F32),
            pltpu.VMEM((MIX_B, 8 * (CONV_WIDTH - 1), SSD_CONV_DIM), F32),
            pltpu.VMEM((MIX_B, 8 * (CONV_WIDTH - 1), LRU_WIDTH), F32),
            pltpu.VMEM((MIX_B, SSD_STATE, SSD_WIDTH), F32),
            pltpu.VMEM((MIX_B, 8, LRU_WIDTH), F32),
            pltpu.VMEM((D_MODEL // LANES, CHUNK, LANES), F32),
            pltpu.VMEM((D_MODEL // LANES, CHUNK, LANES), F32),
        ],
        compiler_params=pltpu.CompilerParams(
            dimension_semantics=("arbitrary",), vmem_limit_bytes=VMEM_LIMIT),
        name="ssd_lru_mixer",
    )(x, meta_chunk, *params)


def _pool_rows(step, h_ref, g_ref, pw_ref, pb_ref, ps_ref, o_ref, buf):
    base = POOL_MAX_WINDOW
    for k in range(h_ref.shape[0] // CHUNK):
        chunk = step * (h_ref.shape[0] // CHUNK) + k
        c = lax.rem(chunk, NCHUNK)

        @pl.when(c == 0)
        def _():
            buf[0:base, :] = jnp.zeros((base, D_MODEL), F32)

        h = h_ref[k * CHUNK:(k + 1) * CHUNK, :]
        hn = _rms(h, g_ref[...])
        buf[base:base + CHUNK, :] = hn
        pos = _row_ids((CHUNK, POOL_GROUP_DIM), c) - PAD
        outs = []
        for g, w in enumerate(POOL_WINDOWS):
            sl = slice(g * POOL_GROUP_DIM, (g + 1) * POOL_GROUP_DIM)
            ws = buf[:, sl]
            s = 1
            while s < w:
                ws = ws + pltpu.roll(ws, s, 0)
                s *= 2
            ws = ws[base:, :]
            count = jnp.clip(pos + 1, 1, w).astype(F32)
            pooled = ws / count - hn[:, sl]
            outs.append(_dot(pooled.astype(BF16), pw_ref[g]))
        buf[0:base, :] = buf[CHUNK:CHUNK + base, :]
        y = (jnp.concatenate(outs, axis=1) + pb_ref[...]) * ps_ref[...]
        o_ref[k * CHUNK:(k + 1) * CHUNK, :] = jnp.where(_row_ids((CHUNK, D_MODEL), c) >= PAD, h + y, 0.0)


MOE_TM = 512
N_MOE_TILES = TP // MOE_TM
DISPATCH_TILES = 2
RUN_ALIGN = 16
RUN_SHIFT = 4
RUN_BIG = 64
RUN_BIG_SHIFT = 6
LOCAL_ROWS = 1280
EXPERT_TILE = 512
N_ROUTED = BATCH * (N_META + SEQ)
MAX_ROWS = (2 * N_ROUTED + N_MOE_TILES * MOE_EXPERTS * (RUN_ALIGN - 1)
            + MOE_EXPERTS * (EXPERT_TILE - 1))
NT_MAX = -(-MAX_ROWS // EXPERT_TILE)
NR = NT_MAX * EXPERT_TILE
XS_COLS = D_MODEL + LANES
INT_ROWS = 8
TOK_GATE1, TOK_GATE2, TOK_POS1, TOK_POS2, TOK_EXPERT1 = 0, 3, 6, 7, 8
assert LOCAL_ROWS >= 2 * MOE_TM + MOE_EXPERTS * (RUN_ALIGN - 1) and LOCAL_ROWS % LANES == 0


def _first_argmax(vals):
    best, idx = vals[0], jnp.zeros(vals[0].shape, jnp.int32)
    for k in range(1, len(vals)):
        better = vals[k] > best
        idx = jnp.where(better, k, idx)
        best = jnp.where(better, vals[k], best)
    return idx, best


def _softmax_rows(vals):
    m = functools.reduce(jnp.maximum, vals)
    ex = [jnp.exp(v - m) for v in vals]
    tot = functools.reduce(lambda p, q: p + q, ex)
    return [e / tot for e in ex]


def _bf16_pieces(x):
    hi = x.astype(BF16).astype(F32)
    rest = x - hi
    mid = rest.astype(BF16).astype(F32)
    lo = (rest - mid).astype(BF16).astype(F32)
    return [hi, mid, lo]


def _router_kernel(h_ref, valid_ref, g_ref, wr_ref, br_ref, lpos_ref, tok_ref, runlen_ref, runoff_ref, before):
    @pl.when(pl.program_id(0) == 0)
    def _():
        si = lax.broadcasted_iota(jnp.int32, (MOE_TM, MOE_TM), 0)
        ti = lax.broadcasted_iota(jnp.int32, (MOE_TM, MOE_TM), 1)
        before[...] = jnp.where(si < ti, 1.0, 0.0).astype(BF16)

    hn = _rms(h_ref[...], g_ref[...])
    hn_hi = hn.astype(BF16)
    hn_lo = (hn - hn_hi.astype(F32)).astype(BF16)
    by_head = _dot(hn_hi, wr_ref[...])
    logits = (by_head[:, 0:LANES] + (by_head[:, LANES:2 * LANES] + _dot(hn_lo, wr_ref[:, 0:LANES]))
              + br_ref[...])
    lt = logits.T
    p_group = _softmax_rows([lt[k:k + 1, :] for k in range(MOE_GROUPS)])
    g_sel, p_g = _first_argmax(p_group)
    fine = []
    for k in range(MOE_PER_GROUP):
        f = lt[ROUTER_EXPERT_ROW + k:ROUTER_EXPERT_ROW + k + 1, :]
        for g in range(1, MOE_GROUPS):
            r0 = ROUTER_EXPERT_ROW + g * MOE_PER_GROUP + k
            f = jnp.where(g_sel == g, lt[r0:r0 + 1, :], f)
        fine.append(f)
    q = _softmax_rows(fine)
    i1, t1 = _first_argmax(q)
    i2, t2 = _first_argmax([jnp.where(i1 == k, -1.0, q[k]) for k in range(MOE_PER_GROUP)])
    tot = t1 + t2
    gate1 = p_g * (t1 / tot)
    gate2 = p_g * (t2 / tot)

    valid = valid_ref[...] > 0
    e1 = jnp.where(valid, g_sel * MOE_PER_GROUP + i1, -1)
    e2 = jnp.where(valid, g_sel * MOE_PER_GROUP + i2, -1)
    erow = lax.broadcasted_iota(jnp.int32, (MOE_EXPERTS, MOE_TM), 0)
    hit1 = erow == e1
    hit2 = erow == e2
    onehot = jnp.where(hit1 | hit2, 1.0, 0.0)
    seen = _dot(onehot.astype(BF16), before[...])
    count = jnp.sum(onehot, axis=1, keepdims=True).astype(jnp.int32)
    runlen = lax.shift_left(lax.shift_right_logical(count + (RUN_ALIGN - 1), RUN_SHIFT), RUN_SHIFT)
    runlen_b = jnp.broadcast_to(runlen, (MOE_EXPERTS, LANES))
    ei = lax.broadcasted_iota(jnp.int32, (MOE_EXPERTS, MOE_EXPERTS), 0)
    ej = lax.broadcasted_iota(jnp.int32, (MOE_EXPERTS, MOE_EXPERTS), 1)
    runoff_b = _dot(jnp.where(ej < ei, 1.0, 0.0), runlen_b.astype(F32), HIGHEST)
    runlen_ref[...] = runlen_b
    runoff_ref[...] = runoff_b.astype(jnp.int32)
    place = seen + runoff_b[:, 0:1]
    pos1 = jnp.where(valid, jnp.sum(jnp.where(hit1, place, 0.0), axis=0, keepdims=True), -1.0)
    pos2 = jnp.where(valid, jnp.sum(jnp.where(hit2, place, 0.0), axis=0, keepdims=True), -1.0)
    r8 = lax.broadcasted_iota(jnp.int32, (INT_ROWS, MOE_TM), 0)
    lpos_ref[...] = jnp.where(r8 == 0, pos1.astype(jnp.int32),
                              jnp.where(r8 == 1, pos2.astype(jnp.int32), 0))

    rows = lax.broadcasted_iota(jnp.int32, lt.shape, 0)
    table = jnp.zeros(lt.shape, F32)
    for k, piece in enumerate(_bf16_pieces(gate1)):
        table = jnp.where(rows == TOK_GATE1 + k, piece, table)
    for k, piece in enumerate(_bf16_pieces(gate2)):
        table = jnp.where(rows == TOK_GATE2 + k, piece, table)
    table = jnp.where(rows == TOK_POS1, pos1, jnp.where(rows == TOK_POS2, pos2, table))
    table = jnp.where(rows == TOK_EXPERT1, e1.astype(F32), table)
    tok_ref[...] = table.T


def _router(h, valid, gain, wr, br):
    row = lambda i: (i, 0)
    col = lambda i: (0, i)
    const = lambda i: (0, 0)
    wr_hi = wr.astype(BF16)
    wr_lo = (wr - wr_hi.astype(F32)).astype(BF16)
    return pl.pallas_call(
        _router_kernel,
        grid=(N_MOE_TILES,),
        in_specs=[
            pl.BlockSpec((MOE_TM, D_MODEL), row),
            pl.BlockSpec((1, MOE_TM), col),
            pl.BlockSpec((1, D_MODEL), const),
            pl.BlockSpec((D_MODEL, 2 * LANES), const),
            pl.BlockSpec((1, LANES), const),
        ],
        out_specs=[pl.BlockSpec((INT_ROWS, MOE_TM), col),
                   pl.BlockSpec((MOE_TM, LANES), row),
                   pl.BlockSpec((MOE_EXPERTS, LANES), row),
                   pl.BlockSpec((MOE_EXPERTS, LANES), row)],
        out_shape=[jax.ShapeDtypeStruct((INT_ROWS, TP), jnp.int32),
                   jax.ShapeDtypeStruct((TP, LANES), F32),
                   jax.ShapeDtypeStruct((N_MOE_TILES * MOE_EXPERTS, LANES), jnp.int32),
                   jax.ShapeDtypeStruct((N_MOE_TILES * MOE_EXPERTS, LANES), jnp.int32)],
        scratch_shapes=[pltpu.VMEM((MOE_TM, MOE_TM), BF16)],
        compiler_params=pltpu.CompilerParams(
            dimension_semantics=("arbitrary",), vmem_limit_bytes=VMEM_LIMIT),
        name="moe_router",
    )(h, valid, gain, jnp.concatenate([wr_hi, wr_lo], axis=1), br)


def _moe_plan(runlen, runoff):
    i32 = jnp.int32
    lens = runlen[:, 0].reshape(N_MOE_TILES, MOE_EXPERTS)
    total = jnp.sum(lens, axis=0)
    padded = (total + (EXPERT_TILE - 1)) // EXPERT_TILE * EXPERT_TILE
    ends = jnp.cumsum(padded)
    base = ends - padded
    goff = base[None, :] + jnp.cumsum(lens, axis=0) - lens
    nval = ends[-1] // EXPERT_TILE
    tidx = jnp.minimum(jnp.arange(NT_MAX, dtype=i32), nval - 1)
    texp = jnp.minimum(jnp.sum(tidx[:, None] >= (ends // EXPERT_TILE)[None, :], axis=1), MOE_EXPERTS - 1)
    return dict(
        goff=goff.reshape(-1).astype(i32), lens=lens.reshape(-1).astype(i32),
        loff=runoff[:, 0].astype(i32), gap_start=(base + total).astype(i32),
        gap_len=(padded - total).astype(i32), nval=nval.reshape(1).astype(i32),
        tidx=tidx.astype(i32), texp=texp.astype(i32),
        nbig=jnp.sum(lens // RUN_BIG, axis=1).astype(i32),
        nsmall=jnp.sum(lens % RUN_BIG // RUN_ALIGN, axis=1).astype(i32))


def _aligned(x):
    return pl.multiple_of(x, RUN_ALIGN)


def _for_each_run(step, lens_ref, fn):
    def expert_body(e, carry):
        k = step * MOE_EXPERTS + e
        nbig = lax.shift_right_logical(lens_ref[k], RUN_BIG_SHIFT)
        rest = nbig * RUN_BIG

        def big_body(q, c):
            fn(k, q * RUN_BIG, RUN_BIG)
            return c

        def small_body(q, c):
            fn(k, rest + q * RUN_ALIGN, RUN_ALIGN)
            return c

        lax.fori_loop(0, nbig, big_body, 0)
        lax.fori_loop(0, lax.shift_right_logical(lens_ref[k] - rest, RUN_SHIFT), small_body, 0)
        return carry

    lax.fori_loop(0, MOE_EXPERTS, expert_body, 0)


def _wait_runs(copy, step, nbig_ref, nsmall_ref):
    def big_body(q, c):
        copy(RUN_BIG).wait()
        return c

    def small_body(q, c):
        copy(RUN_ALIGN).wait()
        return c

    lax.fori_loop(0, nbig_ref[step], big_body, 0)
    lax.fori_loop(0, nsmall_ref[step], small_body, 0)


def _dispatch_kernel(goff_ref, lens_ref, loff_ref, gaps_ref, gapl_ref, nval_ref, nbig_ref, nsmall_ref,
                     h_ref, g_ref, lpos_ref, tok_ref, xs_ref, local, zeros, sems):
    i = pl.program_id(0)
    last = pl.num_programs(0) - 1
    nslots = 2 * DISPATCH_TILES

    def run_copy(s, src, dst, rows):
        return pltpu.make_async_copy(local.at[s, pl.ds(_aligned(src), rows), :],
                                     xs_ref.at[pl.ds(_aligned(dst), rows), :], sems.at[s])

    def wait_tile(tile):
        s = lax.rem(tile, nslots)
        _wait_runs(lambda rows: run_copy(s, 0, 0, rows), tile, nbig_ref, nsmall_ref)

    @pl.when(i >= 2)
    def _():
        for t in range(DISPATCH_TILES):
            wait_tile((i - 2) * DISPATCH_TILES + t)

    riota = lax.broadcasted_iota(jnp.int32, (LOCAL_ROWS, MOE_TM), 0)
    lane = lax.broadcasted_iota(jnp.int32, (MOE_TM, LANES), 1)
    for t in range(DISPATCH_TILES):
        rows_t = slice(t * MOE_TM, (t + 1) * MOE_TM)
        slot = lax.rem(i, 2) * DISPATCH_TILES + t
        hn = _rms(h_ref[rows_t, :], g_ref[...]).astype(BF16)
        sel = jnp.where((riota == lpos_ref[0:1, rows_t]) | (riota == lpos_ref[1:2, rows_t]), 1.0, 0.0).astype(BF16)
        extra = jnp.where((lane < TOK_POS1) | (lane == TOK_EXPERT1), tok_ref[rows_t, :], 0.0).astype(BF16)
        local[slot] = _dot(sel, jnp.concatenate([hn, extra], axis=1)).astype(BF16)

    for t in range(DISPATCH_TILES):
        slot = lax.rem(i, 2) * DISPATCH_TILES + t
        _for_each_run(i * DISPATCH_TILES + t, lens_ref,
                      lambda k, q, rows, slot=slot: run_copy(slot, loff_ref[k] + q, goff_ref[k] + q, rows).start())

    @pl.when(i == last)
    def _():
        zeros[...] = jnp.zeros_like(zeros)
        zsem = sems.at[nslots]

        def gap_copy(dst):
            return pltpu.make_async_copy(zeros.at[pl.ds(0, RUN_ALIGN), :],
                                         xs_ref.at[pl.ds(_aligned(dst), RUN_ALIGN), :], zsem)

        def tile_copy(t):
            return pltpu.make_async_copy(
                zeros, xs_ref.at[pl.ds(pl.multiple_of(t * EXPERT_TILE, EXPERT_TILE), EXPERT_TILE), :], zsem)

        def expert_body(e, n):
            g = lax.shift_right_logical(gapl_ref[e], RUN_SHIFT)

            def body(q, c):
                gap_copy(gaps_ref[e] + q * RUN_ALIGN).start()
                return c

            lax.fori_loop(0, g, body, 0)
            return n + g

        ngap = lax.fori_loop(0, MOE_EXPERTS, expert_body, 0)

        def tail_start(t, c):
            tile_copy(t).start()
            return c

        lax.fori_loop(nval_ref[0], NT_MAX, tail_start, 0)

        def gap_wait(k, c):
            gap_copy(0).wait()
            return c

        lax.fori_loop(0, ngap, gap_wait, 0)

        def tail_wait(t, c):
            tile_copy(t).wait()
            return c

        lax.fori_loop(nval_ref[0], NT_MAX, tail_wait, 0)

        @pl.when(i >= 1)
        def _():
            for t in range(DISPATCH_TILES):
                wait_tile((i - 1) * DISPATCH_TILES + t)

        for t in range(DISPATCH_TILES):
            wait_tile(i * DISPATCH_TILES + t)


def _dispatch(plan, h, gain, lpos, tok):
    row = lambda i, *_: (i, 0)
    rows = DISPATCH_TILES * MOE_TM
    return pl.pallas_call(
        _dispatch_kernel,
        grid_spec=pltpu.PrefetchScalarGridSpec(
            num_scalar_prefetch=8,
            grid=(N_MOE_TILES // DISPATCH_TILES,),
            in_specs=[
                pl.BlockSpec((rows, D_MODEL), row),
                pl.BlockSpec((1, D_MODEL), lambda i, *_: (0, 0)),
                pl.BlockSpec((INT_ROWS, rows), lambda i, *_: (0, i)),
                pl.BlockSpec((rows, LANES), row),
            ],
            out_specs=pl.BlockSpec(memory_space=pl.ANY),
            scratch_shapes=[pltpu.VMEM((2 * DISPATCH_TILES, LOCAL_ROWS, XS_COLS), BF16),
                            pltpu.VMEM((EXPERT_TILE, XS_COLS), BF16),
                            pltpu.SemaphoreType.DMA((2 * DISPATCH_TILES + 1,))],
        ),
        out_shape=jax.ShapeDtypeStruct((NR, XS_COLS), BF16),
        compiler_params=pltpu.CompilerParams(
            dimension_semantics=("arbitrary",), vmem_limit_bytes=VMEM_LIMIT),
        name="moe_dispatch",
    )(plan["goff"], plan["lens"], plan["loff"], plan["gap_start"], plan["gap_len"], plan["nval"],
      plan["nbig"], plan["nsmall"], h, gain, lpos, tok)


def _experts_kernel(tidx_ref, texp_ref, nval_ref, xs_ref, wg_ref, wu_ref, wd_ref, ys_ref, wg16, wu16, wd16):
    j = pl.program_id(0)
    active = j < nval_ref[0]
    new_expert = (j == 0) | (texp_ref[j] != texp_ref[jnp.maximum(j - 1, 0)])

    @pl.when(active & new_expert)
    def _():
        wg16[...] = wg_ref[0].astype(BF16)
        wu16[...] = wu_ref[0].astype(BF16)
        wd16[...] = wd_ref[0].astype(BF16)

    @pl.when(active)
    def _():
        x = xs_ref[:, 0:D_MODEL]
        extra = xs_ref[:, D_MODEL:XS_COLS].astype(F32)
        lane = lax.broadcasted_iota(jnp.int32, extra.shape, 1)

        def lanes(lo, hi):
            return jnp.sum(jnp.where((lane >= lo) & (lane < hi), extra, 0.0), axis=1, keepdims=True)

        first = lanes(TOK_EXPERT1, TOK_EXPERT1 + 1) == texp_ref[j].astype(F32)
        gate = jnp.where(first, lanes(TOK_GATE1, TOK_GATE2), lanes(TOK_GATE2, TOK_POS1))
        a = _dot(x, wg16[...])
        b = _dot(x, wu16[...])
        y = _dot((_silu(a) * b).astype(BF16), wd16[...])
        ys_ref[...] = (gate * y).astype(BF16)

    @pl.when(jnp.logical_not(active))
    def _():
        ys_ref[...] = jnp.zeros_like(ys_ref)


def _experts(plan, xs, layer, wg, wu, wd):
    tile = lambda j, tidx, texp, nval: (tidx[j], 0)
    expert = lambda j, tidx, texp, nval: (layer, texp[j], 0, 0)
    return pl.pallas_call(
        _experts_kernel,
        grid_spec=pltpu.PrefetchScalarGridSpec(
            num_scalar_prefetch=3,
            grid=(NT_MAX,),
            in_specs=[
                pl.BlockSpec((EXPERT_TILE, XS_COLS), tile),
                pl.BlockSpec((None, 1, D_MODEL, D_EXPERT), expert),
                pl.BlockSpec((None, 1, D_MODEL, D_EXPERT), expert),
                pl.BlockSpec((None, 1, D_EXPERT, D_MODEL), expert),
            ],
            out_specs=pl.BlockSpec((EXPERT_TILE, D_MODEL), lambda j, *_: (j, 0)),
            scratch_shapes=[pltpu.VMEM((D_MODEL, D_EXPERT), BF16),
                            pltpu.VMEM((D_MODEL, D_EXPERT), BF16),
                            pltpu.VMEM((D_EXPERT, D_MODEL), BF16)],
        ),
        out_shape=jax.ShapeDtypeStruct((NR, D_MODEL), BF16),
        compiler_params=pltpu.CompilerParams(
            dimension_semantics=("arbitrary",), vmem_limit_bytes=VMEM_LIMIT),
        name="moe_experts",
    )(plan["tidx"], plan["texp"], plan["nval"], xs, wg, wu, wd)


def _combined_rows(goff_ref, lens_ref, loff_ref, nbig_ref, nsmall_ref, h_ref, tok_ref, ys_ref, local, sems):
    i = pl.program_id(0)
    slot = lax.rem(i, 2)

    def run_copy(s, src, dst, rows):
        return pltpu.make_async_copy(ys_ref.at[pl.ds(_aligned(src), rows), :],
                                     local.at[s, pl.ds(_aligned(dst), rows), :], sems.at[s])

    def start_tile(step):
        s = lax.rem(step, 2)
        _for_each_run(step, lens_ref,
                      lambda k, q, rows: run_copy(s, goff_ref[k] + q, loff_ref[k] + q, rows).start())

    @pl.when(i == 0)
    def _():
        local[...] = jnp.zeros_like(local)
        start_tile(0)

    @pl.when(i < pl.num_programs(0) - 1)
    def _():
        start_tile(i + 1)

    _wait_runs(lambda rows: run_copy(slot, 0, 0, rows), i, nbig_ref, nsmall_ref)
    tok = tok_ref[...]
    p1 = tok[:, TOK_POS1:TOK_POS1 + 1].astype(jnp.int32)
    p2 = tok[:, TOK_POS2:TOK_POS2 + 1].astype(jnp.int32)
    ciota = lax.broadcasted_iota(jnp.int32, (MOE_TM, LOCAL_ROWS), 1)
    sel = jnp.where((ciota == p1) | (ciota == p2), 1.0, 0.0).astype(BF16)
    return h_ref[...] + _dot(sel, local[slot])


def _combine_pool_kernel(goff_ref, lens_ref, loff_ref, nbig_ref, nsmall_ref, h_ref, tok_ref,
                         g_ref, pw_ref, pb_ref, ps_ref, ys_ref, o_ref, local, mixed, buf, sems):
    mixed[...] = _combined_rows(goff_ref, lens_ref, loff_ref, nbig_ref, nsmall_ref, h_ref, tok_ref, ys_ref,
                                local, sems)
    _pool_rows(pl.program_id(0), mixed, g_ref, pw_ref, pb_ref, ps_ref, o_ref, buf)


def _combine_final_kernel(goff_ref, lens_ref, loff_ref, nbig_ref, nsmall_ref, h_ref, tok_ref, g_ref, ys_ref,
                          o_ref, local, stage, sems, out_sems):
    i = pl.program_id(0)
    last = pl.num_programs(0) - 1
    slot = lax.rem(i, 2)
    per_tile = MOE_TM // CHUNK

    def for_each_seq_chunk(step, fn):
        s = lax.rem(step, 2)
        for k in range(per_tile):
            chunk = step * per_tile + k
            c = lax.rem(chunk, NCHUNK)
            dst = (lax.div(chunk, NCHUNK) * (SEQ // CHUNK) + c - 1) * CHUNK

            @pl.when(c > 0)
            def _():
                fn(pltpu.make_async_copy(stage.at[s, pl.ds(k * CHUNK, CHUNK), :],
                                         o_ref.at[pl.ds(pl.multiple_of(dst, CHUNK), CHUNK), :],
                                         out_sems.at[s]))

    @pl.when(i >= 2)
    def _():
        for_each_seq_chunk(i - 2, lambda copy: copy.wait())

    out = _combined_rows(goff_ref, lens_ref, loff_ref, nbig_ref, nsmall_ref, h_ref, tok_ref, ys_ref,
                         local, sems)
    stage[slot] = _rms(out, g_ref[...])
    for_each_seq_chunk(i, lambda copy: copy.start())

    @pl.when(i == last)
    def _():
        @pl.when(i >= 1)
        def _():
            for_each_seq_chunk(i - 1, lambda copy: copy.wait())

        for_each_seq_chunk(i, lambda copy: copy.wait())


def _combine(plan, h, tok, ys, pool=None, final_gain=None):
    final = final_gain is not None
    row = lambda i, *_: (i, 0)
    whole = lambda a: pl.BlockSpec(a.shape, lambda i, *_: (0,) * a.ndim)
    extra = [final_gain] if final else list(pool)
    scratch = [pltpu.VMEM((2, LOCAL_ROWS, D_MODEL), BF16)]
    if final:
        scratch.append(pltpu.VMEM((2, MOE_TM, D_MODEL), F32))
    else:
        scratch += [pltpu.VMEM((MOE_TM, D_MODEL), F32), pltpu.VMEM((CHUNK + POOL_MAX_WINDOW, D_MODEL), F32)]
    return pl.pallas_call(
        _combine_final_kernel if final else _combine_pool_kernel,
        grid_spec=pltpu.PrefetchScalarGridSpec(
            num_scalar_prefetch=5,
            grid=(N_MOE_TILES,),
            in_specs=([pl.BlockSpec((MOE_TM, D_MODEL), row), pl.BlockSpec((MOE_TM, LANES), row)]
                      + [whole(a) for a in extra] + [pl.BlockSpec(memory_space=pl.ANY)]),
            out_specs=(pl.BlockSpec(memory_space=pl.ANY) if final
                       else pl.BlockSpec((MOE_TM, D_MODEL), row)),
            scratch_shapes=scratch + [pltpu.SemaphoreType.DMA((2,))] * (2 if final else 1),
        ),
        out_shape=jax.ShapeDtypeStruct((BATCH * SEQ if final else TP, D_MODEL), F32),
        compiler_params=pltpu.CompilerParams(
            dimension_semantics=("arbitrary",), vmem_limit_bytes=VMEM_LIMIT),
        name="moe_combine_final" if final else "moe_combine_pool",
    )(plan["goff"], plan["lens"], plan["loff"], plan["nbig"], plan["nsmall"], h, tok, *extra, ys)


def _row(v):
    return v.reshape(1, -1).astype(F32)


def _pad_lanes(v):
    return jnp.pad(_row(v), ((0, 0), (0, LANES - v.shape[-1])))


def _pair_blockdiag(w):
    w = w.reshape(LRU_WIDTH // LANES, 2, 64, 64)
    z = jnp.zeros_like(w[:, 0])
    top = jnp.concatenate([w[:, 0], z], axis=2)
    bot = jnp.concatenate([z, w[:, 1]], axis=2)
    return jnp.concatenate([top, bot], axis=1).astype(BF16)


def _moe_layer(h, layer, pool, final_gain, valid, ffn_norm, rgw, rgb, rew, reb, wg, wu, wd):
    wr = jnp.zeros((D_MODEL, LANES), F32)
    wr = wr.at[:, 0:MOE_GROUPS].set(rgw[layer])
    wr = wr.at[:, ROUTER_EXPERT_ROW:ROUTER_EXPERT_ROW + MOE_EXPERTS].set(rew[layer])
    br = jnp.zeros((1, LANES), F32)
    br = br.at[0, 0:MOE_GROUPS].set(rgb[layer])
    br = br.at[0, ROUTER_EXPERT_ROW:ROUTER_EXPERT_ROW + MOE_EXPERTS].set(reb[layer])
    gain = _row(ffn_norm[layer])
    lpos, tok, runlen, runoff = _router(h, valid, gain, wr, br)
    plan = _moe_plan(runlen, runoff)
    xs = _dispatch(plan, h, gain, lpos, tok)
    ys = _experts(plan, xs, layer, wg, wu, wd)
    return _combine(plan, h, tok, ys, pool, final_gain)


def kernel(x, meta_tokens, norm_final, mix_norm_even, w_in, ssd_conv_w, ssd_conv_b, ssd_dt_bias, ssd_a_log, ssd_d, ssd_norm, lru_conv_w, lru_conv_b, lru_w_a, lru_b_a, lru_w_x, lru_b_x, lru_lambda, w_out, mix_norm_odd, pool_w, pool_b, pool_scale, ffn_norm, router_group_w, router_group_b, router_expert_w, router_expert_b, expert_w_gate, expert_w_up, expert_w_down):
    meta_chunk = jnp.concatenate([jnp.zeros((PAD, D_MODEL), F32), meta_tokens.astype(F32)], axis=0)
    rows = jnp.arange(CHUNK)
    meta_chunk = meta_chunk[(rows % 8) * SEG + rows // 8]
    valid = ((jnp.arange(TP, dtype=jnp.int32) % LP) >= PAD).astype(jnp.int32).reshape(1, TP)
    moe_args = (valid, ffn_norm, router_group_w, router_group_b, router_expert_w, router_expert_b,
                expert_w_gate, expert_w_up, expert_w_down)

    wi = w_in[0]
    w_proj = jnp.concatenate(
        [wi[:, 0:1024], wi[:, 1024:2560], wi[:, 2576:3600], wi[:, 3600:4624], wi[:, 2560:2576],
         jnp.zeros((D_MODEL, LANES - SSD_HEADS), F32)], axis=1).astype(BF16)
    piece_head = jnp.where(jnp.arange(LANES) < HEAD_PIECES * SSD_HEADS, jnp.arange(LANES) % SSD_HEADS, -1)
    expand = (piece_head[:, None] == (jnp.arange(SSD_WIDTH) // SSD_HEAD_DIM)[None, :]).astype(BF16)
    params = [
        _row(mix_norm_even[0]), w_proj,
        ssd_conv_w[0].astype(F32), _row(ssd_conv_b[0]), _pad_lanes(ssd_dt_bias[0]), _pad_lanes(ssd_a_log[0]),
        _row(jnp.repeat(ssd_d[0], SSD_HEAD_DIM)), _row(ssd_norm[0]),
        lru_conv_w[0].astype(F32), _row(lru_conv_b[0]),
        jnp.concatenate([_pair_blockdiag(lru_w_a[0]), _pair_blockdiag(lru_w_x[0])], axis=2),
        _row(lru_b_a[0]), _row(lru_b_x[0]), _row(lru_lambda[0]),
        w_out[0].astype(BF16), expand,
    ]
    h = _mixer(x.astype(F32), meta_chunk, params).reshape(TP, D_MODEL)
    pool = (_row(mix_norm_odd[0]), pool_w[0].astype(BF16), _row(pool_b[0]), _row(pool_scale[0]))
    h = _moe_layer(h, 0, pool, None, *moe_args)
    out = _moe_layer(h, 1, None, _row(norm_final), *moe_args)
    return out.reshape(BATCH, SEQ, D_MODEL)
```

```python
import functools
import math

import jax
import jax.numpy as jnp
from jax import lax
from jax.experimental import pallas as pl
from jax.experimental.pallas import tpu as pltpu

F32 = jnp.float32
BF16 = jnp.bfloat16
HIGHEST = lax.Precision.HIGHEST

D_MODEL = 1024
BATCH = 8
SEQ = 2048
N_META = 16
RMS_EPS = 1e-6
CONV_WIDTH = 4
CHUNK = 128
PAD = CHUNK - N_META
LP = PAD + N_META + SEQ
NCHUNK = LP // CHUNK
TP = BATCH * LP

SSD_HEADS = 16
SSD_HEAD_DIM = 64
SSD_WIDTH = 1024
SSD_GROUPS = 2
SSD_STATE = 128
SSD_GROUP_WIDTH = SSD_WIDTH // SSD_GROUPS
SSD_CONV_DIM = SSD_WIDTH + 2 * SSD_GROUPS * SSD_STATE
LRU_WIDTH = 1024
LRU_C = 8.0
LANES = 128
POOL_WINDOWS = (2, 4, 8, 16)
POOL_GROUP_DIM = 256
POOL_MAX_WINDOW = 16
MOE_GROUPS = 4
MOE_PER_GROUP = 4
MOE_EXPERTS = 16
D_EXPERT = 512
ROUTER_EXPERT_ROW = 8

VMEM_LIMIT = 56 * 1024 * 1024


def _dot(a, b, precision=None):
    return jnp.dot(a, b, preferred_element_type=F32, precision=precision)


def _rms(x, gain):
    ms = jnp.mean(x * x, axis=-1, keepdims=True)
    return x * lax.rsqrt(ms + RMS_EPS) * gain


def _silu(x):
    return x * jax.nn.sigmoid(x)


def _softplus(x):
    return jnp.maximum(x, 0.0) + jnp.log1p(jnp.exp(-jnp.abs(x)))


def _row_ids(shape, chunk):
    return lax.broadcasted_iota(jnp.int32, shape, 0) + chunk * CHUNK


MIX_B = 2
N_MIX_ITEMS = (BATCH // MIX_B) * NCHUNK
_Z0, _X0, _G0, _L0, _T0, _PEND = 0, 1024, 2560, 3584, 4608, 4736
HEAD_PIECES = 3
PROJ_PIECE = 256


def _carried(c, state):
    return jnp.where(jnp.full(state.shape, c, jnp.int32) == 0, 0.0, state)


SEG = CHUNK // 8


def _time_ids(shape, axis=0):
    p = lax.broadcasted_iota(jnp.int32, shape, axis)
    return lax.bitwise_and(p, 7) * SEG + lax.shift_right_logical(p, 3)


def _load_interleaved(slab, src):
    cols = []
    for j in range(D_MODEL // LANES):
        slab[j] = src[:, j * LANES:(j + 1) * LANES]
        cols.append(jnp.concatenate([slab[j, pl.ds(k, 8, stride=SEG), :] for k in range(SEG)], axis=0))
    return jnp.concatenate(cols, axis=1)


def _store_time_order(slab, dst, value):
    for j in range(D_MODEL // LANES):
        for k in range(SEG):
            slab[j, pl.ds(k, 8, stride=SEG), :] = value[k * 8:(k + 1) * 8, j * LANES:(j + 1) * LANES]
        dst[:, j * LANES:(j + 1) * LANES] = slab[j]


def _causal_conv(hist, c, x, w_ref, b_ref):
    taps = CONV_WIDTH - 1
    prev = _carried(c, hist[...])
    hist[...] = x[(SEG - taps) * 8:, :]
    first_segment = lax.broadcasted_iota(jnp.int32, (8, x.shape[1]), 0) == 0
    groups = [x[k * 8:(k + 1) * 8, :] for k in range(SEG)]
    wrapped = {i: jnp.where(first_segment, pltpu.roll(prev[(taps - i) * 8:(taps - i + 1) * 8, :], 1, 0),
                            pltpu.roll(groups[SEG - i], 1, 0)) for i in range(1, taps + 1)}
    out = []
    for k in range(SEG):
        acc = b_ref[...] + w_ref[taps:taps + 1, :] * groups[k]
        for j in range(1, taps + 1):
            earlier = groups[k - j] if k >= j else wrapped[j - k]
            acc = acc + w_ref[taps - j:taps - j + 1, :] * earlier
        out.append(acc)
    return jnp.concatenate(out, axis=0)


def _head_pieces(v):
    lane = lax.broadcasted_iota(jnp.int32, v.shape, 1)
    rest = jnp.where(lane < SSD_HEADS, v, 0.0)
    packed = None
    for k in range(HEAD_PIECES):
        piece = rest.astype(BF16).astype(F32)
        rest = rest - piece
        moved = piece if k == 0 else pltpu.roll(piece, k * SSD_HEADS, 1)
        packed = moved if packed is None else packed + moved
    return packed.astype(BF16)


def _expand_heads(v, exp_ref):
    return _dot(_head_pieces(v), exp_ref[...])


def _cumsum_over_time(order, v):
    sums = _dot(order, _head_pieces(v))
    total = sums
    for k in range(1, HEAD_PIECES):
        total = total + pltpu.roll(sums, LANES - k * SSD_HEADS, 1)
    lane = lax.broadcasted_iota(jnp.int32, v.shape, 1)
    return jnp.where(lane < SSD_HEADS, total, 0.0)


def _ssd_chunk(c, z, xbc, dt_raw, hist, st, cw_ref, cb_ref, dtb_ref, alog_ref, dsk_ref, ng_ref, exp_ref):
    xc = _silu(_causal_conv(hist, c, xbc, cw_ref, cb_ref))
    yield
    xs = xc[:, 0:SSD_WIDTH]
    valid_s = _time_ids((CHUNK, 2 * SSD_STATE)) + c * CHUNK >= PAD
    bm = jnp.where(valid_s, xc[:, SSD_WIDTH:SSD_WIDTH + 2 * SSD_STATE], 0.0)
    cm = jnp.where(valid_s, xc[:, SSD_WIDTH + 2 * SSD_STATE:], 0.0)

    li = _time_ids((CHUNK, CHUNK), 0)
    si = _time_ids((CHUNK, CHUNK), 1)
    causal = si <= li
    dt = _softplus(dt_raw + dtb_ref[...])
    dt = jnp.where(li + c * CHUNK >= PAD, dt, 0.0)
    adt = dt * (-jnp.exp(alog_ref[...]))
    a_cs = _cumsum_over_time(jnp.where(causal, 1.0, 0.0).astype(BF16), adt)
    a_last = a_cs[CHUNK - 1:CHUNK, :]
    ea = jnp.exp(a_cs)
    dt_x = _expand_heads(dt, exp_ref)
    w_x = _expand_heads(jnp.exp(a_last - a_cs) * dt, exp_ref)
    ea_x = _expand_heads(ea, exp_ref)
    xdt = xs * dt_x
    a_cs_t = a_cs.T
    yield
    lane = lax.broadcasted_iota(jnp.int32, (CHUNK, LANES), 1)

    ys = []
    for g in range(SSD_GROUPS):
        gsl = slice(g * SSD_GROUP_WIDTH, (g + 1) * SSD_GROUP_WIDTH)
        bg = bm[:, g * SSD_STATE:(g + 1) * SSD_STATE]
        cg16 = cm[:, g * SSD_STATE:(g + 1) * SSD_STATE].astype(BF16)
        cbm = lax.dot_general(cg16, bg.astype(BF16), (((1,), (1,)), ((), ())),
                              preferred_element_type=F32)
        s_in = _carried(c, st[:, gsl])
        y_off = _dot(cg16, s_in.astype(BF16)) * ea_x[:, gsl]
        s_new = _dot(bg.T.astype(BF16), (w_x[:, gsl] * xs[:, gsl]).astype(BF16))
        st[:, gsl] = ea_x[CHUNK - 1:CHUNK, gsl] * s_in + s_new
        for j in range(SSD_GROUP_WIDTH // LANES):
            xp = xdt[:, g * SSD_GROUP_WIDTH + j * LANES:
                     g * SSD_GROUP_WIDTH + (j + 1) * LANES].astype(BF16)
            weights, values = [], []
            for hh in range(LANES // SSD_HEAD_DIM):
                hd = g * (SSD_HEADS // SSD_GROUPS) + j * (LANES // SSD_HEAD_DIM) + hh
                seg = a_cs[:, hd:hd + 1] - a_cs_t[hd:hd + 1, :]
                dec = jnp.exp(jnp.where(causal, seg, -1e30))
                weights.append((cbm * dec).astype(BF16))
                own = (lane >= hh * SSD_HEAD_DIM) & (lane < (hh + 1) * SSD_HEAD_DIM)
                values.append(jnp.where(own, xp, jnp.zeros_like(xp)))
            y_diag = _dot(jnp.concatenate(weights, axis=1), jnp.concatenate(values, axis=0))
            ys.append(y_diag + y_off[:, j * LANES:(j + 1) * LANES])
            if j % 2:
                yield
    y = jnp.concatenate(ys, axis=1) + dsk_ref[...] * xs
    y = y * _silu(z)
    normed = []
    for g in range(SSD_GROUPS):
        yg = y[:, g * SSD_GROUP_WIDTH:(g + 1) * SSD_GROUP_WIDTH]
        normed.append(yg * lax.rsqrt(jnp.mean(yg * yg, axis=-1, keepdims=True) + RMS_EPS))
    return jnp.concatenate(normed, axis=1) * ng_ref[...]


def _lru_chunk(c, gt, lin, hist, lc, lcw_ref, lcb_ref, wax_ref, ba_ref, bx_ref, lam_ref):
    xb = _causal_conv(hist, c, lin, lcw_ref, lcb_ref)
    yield
    xb16 = xb.astype(BF16)
    both = [_dot(xb16[:, j * LANES:(j + 1) * LANES], wax_ref[j]) for j in range(LRU_WIDTH // LANES)]
    r = jax.nn.sigmoid(jnp.concatenate([p[:, 0:LANES] for p in both], axis=1) + ba_ref[...])
    ig = jax.nn.sigmoid(jnp.concatenate([p[:, LANES:2 * LANES] for p in both], axis=1) + bx_ref[...])
    log_a = (-LRU_C * _softplus(-lam_ref[...])) * r
    a = jnp.exp(log_a)
    mult = jnp.sqrt(jnp.tanh(-log_a) * (a * a + 1.0))
    grow = _time_ids((CHUNK, LRU_WIDTH)) + c * CHUNK
    mult = jnp.where(grow == PAD, 1.0, mult)
    u = jnp.where(grow >= PAD, mult * (ig * xb), 0.0)
    yield

    local, decay = [], []
    for k in range(SEG):
        a8 = a[k * 8:(k + 1) * 8, :]
        u8 = u[k * 8:(k + 1) * 8, :]
        local.append(u8 if k == 0 else a8 * local[-1] + u8)
        decay.append(a8 if k == 0 else a8 * decay[-1])
        if k == SEG // 2:
            yield
    r8 = lax.broadcasted_iota(jnp.int32, (8, LRU_WIDTH), 0)
    end, span = local[-1], decay[-1]
    for d in (1, 2, 4):
        end_sh = jnp.where(r8 >= d, pltpu.roll(end, d, 0), 0.0)
        span_sh = jnp.where(r8 >= d, pltpu.roll(span, d, 0), 1.0)
        end = span * end_sh + end
        span = span * span_sh
    carry = _carried(c, lc[0:1, :])
    end = end + span * carry
    lc[0:1, :] = end[7:8, :]
    entering = jnp.where(r8 >= 1, pltpu.roll(end, 1, 0), carry)
    hs = [local[k] + decay[k] * entering for k in range(SEG)]
    gelu = 0.5 * gt * (1.0 + jnp.tanh(math.sqrt(2.0 / math.pi) * (gt + 0.044715 * (gt * gt * gt))))
    return jnp.concatenate(hs, axis=0) * gelu


def _mixer_kernel(x_ref, meta_ref, gin_ref, wproj_ref,
                  cw_ref, cb_ref, dtb_ref, alog_ref, dsk_ref, ng_ref,
                  lcw_ref, lcb_ref, wax_ref, ba_ref, bx_ref, lam_ref,
                  wout_ref, exp_ref, o_ref, h_cur, h_next, p_cur, p_next, hist_x, hist_l, st, lc,
                  slab_in, slab_out):
    s = pl.program_id(0)
    p_chunk = lax.rem(jnp.minimum(s, N_MIX_ITEMS - 1), NCHUNK)
    c = lax.rem(jnp.maximum(s - 1, 0), NCHUNK)

    @pl.when(s == 0)
    def _():
        h_next[...] = jnp.zeros_like(h_next)
        p_next[...] = jnp.zeros_like(p_next)
        hist_x[...] = jnp.zeros_like(hist_x)
        hist_l[...] = jnp.zeros_like(hist_l)
        st[...] = jnp.zeros_like(st)
        lc[...] = jnp.zeros_like(lc)

    h_cur[...] = h_next[...]
    p_cur[...] = p_next[...]

    from_meta = jnp.full((CHUNK, D_MODEL), p_chunk, jnp.int32) == 0
    for b in range(MIX_B):
        h_next[b * CHUNK:(b + 1) * CHUNK, :] = jnp.where(
            from_meta, meta_ref[...], _load_interleaved(slab_in, x_ref.at[b]))
    hn = _rms(h_next[...], gin_ref[...]).astype(BF16)
    pieces = [(lo, min(lo + PROJ_PIECE, _PEND)) for lo in range(0, _PEND, PROJ_PIECE)]

    def project_piece():
        if pieces:
            lo, hi = pieces.pop(0)
            p_next[:, lo:hi] = _dot(hn, wproj_ref[:, lo:hi])

    def rows(b):
        return slice(b * CHUNK, (b + 1) * CHUNK)

    ssd = [_ssd_chunk(c, p_cur[rows(b), _Z0:_X0], p_cur[rows(b), _X0:_G0], p_cur[rows(b), _T0:_PEND],
                      hist_x.at[b], st.at[b], cw_ref, cb_ref, dtb_ref, alog_ref, dsk_ref, ng_ref, exp_ref)
           for b in range(MIX_B)]
    lru = [_lru_chunk(c, p_cur[rows(b), _G0:_L0], p_cur[rows(b), _L0:_T0], hist_l.at[b], lc.at[b],
                      lcw_ref, lcb_ref, wax_ref, ba_ref, bx_ref, lam_ref)
           for b in range(MIX_B)]

    def finish(stages):
        while True:
            try:
                next(stages)
            except StopIteration as done:
                return done.value
            project_piece()

    h = h_cur[...]
    project_piece()
    mixed = []
    for b in range(MIX_B):
        y_ssd = finish(ssd[b])
        y_lru = finish(lru[b])
        mixed.append(jnp.concatenate([y_ssd, y_lru], axis=1).astype(BF16))
    while pieces:
        project_piece()
    out = _dot(jnp.concatenate(mixed, axis=0), wout_ref[...]) + h
    keep = _time_ids((CHUNK, D_MODEL)) + c * CHUNK >= PAD
    for b in range(MIX_B):
        _store_time_order(slab_out, o_ref.at[b], jnp.where(keep, out[b * CHUNK:(b + 1) * CHUNK, :], 0.0))


def _mixer(x, meta_chunk, params):
    full = lambda a: pl.BlockSpec(a.shape, lambda s: (0,) * a.ndim, pipeline_mode=pl.Buffered(1))

    def x_block(s):
        item = jnp.minimum(s, N_MIX_ITEMS - 1)
        return item // NCHUNK, jnp.maximum(item % NCHUNK - 1, 0), 0

    def out_block(s):
        item = jnp.maximum(s - 1, 0)
        return item // NCHUNK, item % NCHUNK, 0

    return pl.pallas_call(
        _mixer_kernel,
        grid=(N_MIX_ITEMS + 1,),
        in_specs=[pl.BlockSpec((MIX_B, CHUNK, D_MODEL), x_block), full(meta_chunk)] + [full(p) for p in params],
        out_specs=pl.BlockSpec((MIX_B, CHUNK, D_MODEL), out_block),
        out_shape=jax.ShapeDtypeStruct((BATCH, LP, D_MODEL), F32),
        scratch_shapes=[
            pltpu.VMEM((MIX_B * CHUNK, D_MODEL), F32),
            pltpu.VMEM((MIX_B * CHUNK, D_MODEL), F32),
            pltpu.VMEM((MIX_B * CHUNK, _PEND), F32),
            pltpu.VMEM((MIX_B * CHUNK, _PEND), F32),
            pltpu.VMEM((MIX_B, 8 * (CONV_WIDTH - 1), SSD_CONV_DIM), F32),
            pltpu.VMEM((MIX_B, 8 * (CONV_WIDTH - 1), LRU_WIDTH), F32),
            pltpu.VMEM((MIX_B, SSD_STATE, SSD_WIDTH), F32),
            pltpu.VMEM((MIX_B, 8, LRU_WIDTH), F32),
            pltpu.VMEM((D_MODEL // LANES, CHUNK, LANES), F32),
            pltpu.VMEM((D_MODEL // LANES, CHUNK, LANES), F32),
        ],
        compiler_params=pltpu.CompilerParams(
            dimension_semantics=("arbitrary",), vmem_limit_bytes=VMEM_LIMIT),
        name="ssd_lru_mixer",
    )(x, meta_chunk, *params)


def _pool_rows(step, h_ref, g_ref, pw_ref, pb_ref, ps_ref, o_ref, buf):
    base = POOL_MAX_WINDOW
    for k in range(h_ref.shape[0] // CHUNK):
        chunk = step * (h_ref.shape[0] // CHUNK) + k
        c = lax.rem(chunk, NCHUNK)

        @pl.when(c == 0)
        def _():
            buf[0:base, :] = jnp.zeros((base, D_MODEL), F32)

        h = h_ref[k * CHUNK:(k + 1) * CHUNK, :]
        hn = _rms(h, g_ref[...])
        buf[base:base + CHUNK, :] = hn
        pos = _row_ids((CHUNK, POOL_GROUP_DIM), c) - PAD
        outs = []
        for g, w in enumerate(POOL_WINDOWS):
            sl = slice(g * POOL_GROUP_DIM, (g + 1) * POOL_GROUP_DIM)
            ws = buf[:, sl]
            s = 1
            while s < w:
                ws = ws + pltpu.roll(ws, s, 0)
                s *= 2
            ws = ws[base:, :]
            count = jnp.clip(pos + 1, 1, w).astype(F32)
            pooled = ws / count - hn[:, sl]
            outs.append(_dot(pooled.astype(BF16), pw_ref[g]))
        buf[0:base, :] = buf[CHUNK:CHUNK + base, :]
        y = (jnp.concatenate(outs, axis=1) + pb_ref[...]) * ps_ref[...]
        o_ref[k * CHUNK:(k + 1) * CHUNK, :] = jnp.where(_row_ids((CHUNK, D_MODEL), c) >= PAD, h + y, 0.0)


MOE_TM = 512
N_MOE_TILES = TP // MOE_TM
DISPATCH_TILES = 2
ROUTER_TILES = 2
COMBINE_TILES = 2
COMBINE_ROWS = COMBINE_TILES * MOE_TM
RUN_ALIGN = 16
RUN_SHIFT = 4
RUN_BIG = 64
RUN_BIG_SHIFT = 6
LOCAL_ROWS = 1280
EXPERT_TILE = 512
N_ROUTED = BATCH * (N_META + SEQ)
MAX_ROWS = (2 * N_ROUTED + N_MOE_TILES * MOE_EXPERTS * (RUN_ALIGN - 1)
            + MOE_EXPERTS * (EXPERT_TILE - 1))
NT_MAX = -(-MAX_ROWS // EXPERT_TILE)
NR = NT_MAX * EXPERT_TILE
XS_COLS = D_MODEL + LANES
INT_ROWS = 8
TOK_GATE1, TOK_GATE2, TOK_POS1, TOK_POS2, TOK_EXPERT1 = 0, 3, 6, 7, 8
assert LOCAL_ROWS >= 2 * MOE_TM + MOE_EXPERTS * (RUN_ALIGN - 1) and LOCAL_ROWS % LANES == 0


def _first_argmax(vals):
    best, idx = vals[0], jnp.zeros(vals[0].shape, jnp.int32)
    for k in range(1, len(vals)):
        better = vals[k] > best
        idx = jnp.where(better, k, idx)
        best = jnp.where(better, vals[k], best)
    return idx, best


def _softmax_rows(vals):
    m = functools.reduce(jnp.maximum, vals)
    ex = [jnp.exp(v - m) for v in vals]
    tot = functools.reduce(lambda p, q: p + q, ex)
    return [e / tot for e in ex]


def _bf16_pieces(x):
    hi = x.astype(BF16).astype(F32)
    rest = x - hi
    mid = rest.astype(BF16).astype(F32)
    lo = (rest - mid).astype(BF16).astype(F32)
    return [hi, mid, lo]


def _router_kernel(h_ref, valid_ref, g_ref, wr_ref, br_ref, lpos_ref, tok_ref, runlen_ref, runoff_ref, before):
    @pl.when(pl.program_id(0) == 0)
    def _():
        si = lax.broadcasted_iota(jnp.int32, (MOE_TM, MOE_TM), 0)
        ti = lax.broadcasted_iota(jnp.int32, (MOE_TM, MOE_TM), 1)
        before[...] = jnp.where(si < ti, 1.0, 0.0).astype(BF16)

    for t in range(ROUTER_TILES):
        tokens = pl.ds(t * MOE_TM, MOE_TM)
        experts = pl.ds(t * MOE_EXPERTS, MOE_EXPERTS)
        _route_tile(h_ref.at[tokens, :], valid_ref.at[:, tokens], g_ref, wr_ref, br_ref,
                    lpos_ref.at[:, tokens], tok_ref.at[tokens, :], runlen_ref.at[experts, :],
                    runoff_ref.at[experts, :], before)


def _route_tile(h_ref, valid_ref, g_ref, wr_ref, br_ref, lpos_ref, tok_ref, runlen_ref, runoff_ref, before):
    hn = _rms(h_ref[...], g_ref[...])
    hn_hi = hn.astype(BF16)
    hn_lo = (hn - hn_hi.astype(F32)).astype(BF16)
    by_head = _dot(hn_hi, wr_ref[...])
    logits = (by_head[:, 0:LANES] + (by_head[:, LANES:2 * LANES] + _dot(hn_lo, wr_ref[:, 0:LANES]))
              + br_ref[...])
    lt = logits.T
    p_group = _softmax_rows([lt[k:k + 1, :] for k in range(MOE_GROUPS)])
    g_sel, p_g = _first_argmax(p_group)
    fine = []
    for k in range(MOE_PER_GROUP):
        f = lt[ROUTER_EXPERT_ROW + k:ROUTER_EXPERT_ROW + k + 1, :]
        for g in range(1, MOE_GROUPS):
            r0 = ROUTER_EXPERT_ROW + g * MOE_PER_GROUP + k
            f = jnp.where(g_sel == g, lt[r0:r0 + 1, :], f)
        fine.append(f)
    q = _softmax_rows(fine)
    i1, t1 = _first_argmax(q)
    i2, t2 = _first_argmax([jnp.where(i1 == k, -1.0, q[k]) for k in range(MOE_PER_GROUP)])
    tot = t1 + t2
    gate1 = p_g * (t1 / tot)
    gate2 = p_g * (t2 / tot)

    valid = valid_ref[...] > 0
    e1 = jnp.where(valid, g_sel * MOE_PER_GROUP + i1, -1)
    e2 = jnp.where(valid, g_sel * MOE_PER_GROUP + i2, -1)
    erow = lax.broadcasted_iota(jnp.int32, (MOE_EXPERTS, MOE_TM), 0)
    hit1 = erow == e1
    hit2 = erow == e2
    onehot = jnp.where(hit1 | hit2, 1.0, 0.0)
    seen = _dot(onehot.astype(BF16), before[...])
    count = jnp.sum(onehot, axis=1, keepdims=True).astype(jnp.int32)
    runlen = lax.shift_left(lax.shift_right_logical(count + (RUN_ALIGN - 1), RUN_SHIFT), RUN_SHIFT)
    runlen_b = jnp.broadcast_to(runlen, (MOE_EXPERTS, LANES))
    ei = lax.broadcasted_iota(jnp.int32, (MOE_EXPERTS, MOE_EXPERTS), 0)
    ej = lax.broadcasted_iota(jnp.int32, (MOE_EXPERTS, MOE_EXPERTS), 1)
    runoff_b = _dot(jnp.where(ej < ei, 1.0, 0.0), runlen_b.astype(F32), HIGHEST)
    runlen_ref[...] = runlen_b
    runoff_ref[...] = runoff_b.astype(jnp.int32)
    place = seen + runoff_b[:, 0:1]
    pos1 = jnp.where(valid, jnp.sum(jnp.where(hit1, place, 0.0), axis=0, keepdims=True), -1.0)
    pos2 = jnp.where(valid, jnp.sum(jnp.where(hit2, place, 0.0), axis=0, keepdims=True), -1.0)
    r8 = lax.broadcasted_iota(jnp.int32, (INT_ROWS, MOE_TM), 0)
    lpos_ref[...] = jnp.where(r8 == 0, pos1.astype(jnp.int32),
                              jnp.where(r8 == 1, pos2.astype(jnp.int32), 0))

    rows = lax.broadcasted_iota(jnp.int32, lt.shape, 0)
    table = jnp.zeros(lt.shape, F32)
    for k, piece in enumerate(_bf16_pieces(gate1)):
        table = jnp.where(rows == TOK_GATE1 + k, piece, table)
    for k, piece in enumerate(_bf16_pieces(gate2)):
        table = jnp.where(rows == TOK_GATE2 + k, piece, table)
    table = jnp.where(rows == TOK_POS1, pos1, jnp.where(rows == TOK_POS2, pos2, table))
    table = jnp.where(rows == TOK_EXPERT1, e1.astype(F32), table)
    tok_ref[...] = table.T


def _router(h, valid, gain, wr, br):
    row = lambda i: (i, 0)
    col = lambda i: (0, i)
    const = lambda i: (0, 0)
    wr_hi = wr.astype(BF16)
    wr_lo = (wr - wr_hi.astype(F32)).astype(BF16)
    return pl.pallas_call(
        _router_kernel,
        grid=(N_MOE_TILES // ROUTER_TILES,),
        in_specs=[
            pl.BlockSpec((ROUTER_TILES * MOE_TM, D_MODEL), row),
            pl.BlockSpec((1, ROUTER_TILES * MOE_TM), col),
            pl.BlockSpec((1, D_MODEL), const),
            pl.BlockSpec((D_MODEL, 2 * LANES), const),
            pl.BlockSpec((1, LANES), const),
        ],
        out_specs=[pl.BlockSpec((INT_ROWS, ROUTER_TILES * MOE_TM), col),
                   pl.BlockSpec((ROUTER_TILES * MOE_TM, LANES), row),
                   pl.BlockSpec((ROUTER_TILES * MOE_EXPERTS, LANES), row),
                   pl.BlockSpec((ROUTER_TILES * MOE_EXPERTS, LANES), row)],
        out_shape=[jax.ShapeDtypeStruct((INT_ROWS, TP), jnp.int32),
                   jax.ShapeDtypeStruct((TP, LANES), F32),
                   jax.ShapeDtypeStruct((N_MOE_TILES * MOE_EXPERTS, LANES), jnp.int32),
                   jax.ShapeDtypeStruct((N_MOE_TILES * MOE_EXPERTS, LANES), jnp.int32)],
        scratch_shapes=[pltpu.VMEM((MOE_TM, MOE_TM), BF16)],
        compiler_params=pltpu.CompilerParams(
            dimension_semantics=("arbitrary",), vmem_limit_bytes=VMEM_LIMIT),
        name="moe_router",
    )(h, valid, gain, jnp.concatenate([wr_hi, wr_lo], axis=1), br)


def _moe_plan(runlen, runoff):
    i32 = jnp.int32
    lens = runlen[:, 0].reshape(N_MOE_TILES, MOE_EXPERTS)
    total = jnp.sum(lens, axis=0)
    padded = (total + (EXPERT_TILE - 1)) // EXPERT_TILE * EXPERT_TILE
    ends = jnp.cumsum(padded)
    base = ends - padded
    goff = base[None, :] + jnp.cumsum(lens, axis=0) - lens
    nval = ends[-1] // EXPERT_TILE
    tidx = jnp.minimum(jnp.arange(NT_MAX, dtype=i32), nval - 1)
    texp = jnp.minimum(jnp.sum(tidx[:, None] >= (ends // EXPERT_TILE)[None, :], axis=1), MOE_EXPERTS - 1)
    return dict(
        goff=goff.reshape(-1).astype(i32), lens=lens.reshape(-1).astype(i32),
        loff=runoff[:, 0].astype(i32), gap_start=(base + total).astype(i32),
        gap_len=(padded - total).astype(i32), nval=nval.reshape(1).astype(i32),
        tidx=tidx.astype(i32), texp=texp.astype(i32),
        nbig=jnp.sum(lens // RUN_BIG, axis=1).astype(i32),
        nsmall=jnp.sum(lens % RUN_BIG // RUN_ALIGN, axis=1).astype(i32))


def _aligned(x):
    return pl.multiple_of(x, RUN_ALIGN)


def _for_each_run(step, lens_ref, fn):
    def expert_body(e, carry):
        k = step * MOE_EXPERTS + e
        nbig = lax.shift_right_logical(lens_ref[k], RUN_BIG_SHIFT)
        rest = nbig * RUN_BIG

        def big_body(q, c):
            fn(k, q * RUN_BIG, RUN_BIG)
            return c

        def small_body(q, c):
            fn(k, rest + q * RUN_ALIGN, RUN_ALIGN)
            return c

        lax.fori_loop(0, nbig, big_body, 0)
        lax.fori_loop(0, lax.shift_right_logical(lens_ref[k] - rest, RUN_SHIFT), small_body, 0)
        return carry

    lax.fori_loop(0, MOE_EXPERTS, expert_body, 0)


def _wait_runs(copy, step, nbig_ref, nsmall_ref):
    def big_body(q, c):
        copy(RUN_BIG).wait()
        return c

    def small_body(q, c):
        copy(RUN_ALIGN).wait()
        return c

    lax.fori_loop(0, nbig_ref[step], big_body, 0)
    lax.fori_loop(0, nsmall_ref[step], small_body, 0)


def _dispatch_kernel(goff_ref, lens_ref, loff_ref, gaps_ref, gapl_ref, nval_ref, nbig_ref, nsmall_ref,
                     h_ref, g_ref, lpos_ref, tok_ref, xs_ref, local, zeros, sems):
    i = pl.program_id(0)
    last = pl.num_programs(0) - 1
    nslots = 2 * DISPATCH_TILES

    def run_copy(s, src, dst, rows):
        return pltpu.make_async_copy(local.at[s, pl.ds(_aligned(src), rows), :],
                                     xs_ref.at[pl.ds(_aligned(dst), rows), :], sems.at[s])

    def wait_tile(tile):
        s = lax.rem(tile, nslots)
        _wait_runs(lambda rows: run_copy(s, 0, 0, rows), tile, nbig_ref, nsmall_ref)

    @pl.when(i >= 2)
    def _():
        for t in range(DISPATCH_TILES):
            wait_tile((i - 2) * DISPATCH_TILES + t)

    riota = lax.broadcasted_iota(jnp.int32, (LOCAL_ROWS, MOE_TM), 0)
    lane = lax.broadcasted_iota(jnp.int32, (MOE_TM, LANES), 1)
    for t in range(DISPATCH_TILES):
        rows_t = slice(t * MOE_TM, (t + 1) * MOE_TM)
        slot = lax.rem(i, 2) * DISPATCH_TILES + t
        hn = _rms(h_ref[rows_t, :], g_ref[...]).astype(BF16)
        sel = jnp.where((riota == lpos_ref[0:1, rows_t]) | (riota == lpos_ref[1:2, rows_t]), 1.0, 0.0).astype(BF16)
        extra = jnp.where((lane < TOK_POS1) | (lane == TOK_EXPERT1), tok_ref[rows_t, :], 0.0).astype(BF16)
        local[slot] = _dot(sel, jnp.concatenate([hn, extra], axis=1)).astype(BF16)

    for t in range(DISPATCH_TILES):
        slot = lax.rem(i, 2) * DISPATCH_TILES + t
        _for_each_run(i * DISPATCH_TILES + t, lens_ref,
                      lambda k, q, rows, slot=slot: run_copy(slot, loff_ref[k] + q, goff_ref[k] + q, rows).start())

    @pl.when(i == last)
    def _():
        zeros[...] = jnp.zeros_like(zeros)
        zsem = sems.at[nslots]

        def gap_copy(dst):
            return pltpu.make_async_copy(zeros.at[pl.ds(0, RUN_ALIGN), :],
                                         xs_ref.at[pl.ds(_aligned(dst), RUN_ALIGN), :], zsem)

        def tile_copy(t):
            return pltpu.make_async_copy(
                zeros, xs_ref.at[pl.ds(pl.multiple_of(t * EXPERT_TILE, EXPERT_TILE), EXPERT_TILE), :], zsem)

        def expert_body(e, n):
            g = lax.shift_right_logical(gapl_ref[e], RUN_SHIFT)

            def body(q, c):
                gap_copy(gaps_ref[e] + q * RUN_ALIGN).start()
                return c

            lax.fori_loop(0, g, body, 0)
            return n + g

        ngap = lax.fori_loop(0, MOE_EXPERTS, expert_body, 0)

        def tail_start(t, c):
            tile_copy(t).start()
            return c

        lax.fori_loop(nval_ref[0], NT_MAX, tail_start, 0)

        def gap_wait(k, c):
            gap_copy(0).wait()
            return c

        lax.fori_loop(0, ngap, gap_wait, 0)

        def tail_wait(t, c):
            tile_copy(t).wait()
            return c

        lax.fori_loop(nval_ref[0], NT_MAX, tail_wait, 0)

        @pl.when(i >= 1)
        def _():
            for t in range(DISPATCH_TILES):
                wait_tile((i - 1) * DISPATCH_TILES + t)

        for t in range(DISPATCH_TILES):
            wait_tile(i * DISPATCH_TILES + t)


def _dispatch(plan, h, gain, lpos, tok):
    row = lambda i, *_: (i, 0)
    rows = DISPATCH_TILES * MOE_TM
    return pl.pallas_call(
        _dispatch_kernel,
        grid_spec=pltpu.PrefetchScalarGridSpec(
            num_scalar_prefetch=8,
            grid=(N_MOE_TILES // DISPATCH_TILES,),
            in_specs=[
                pl.BlockSpec((rows, D_MODEL), row),
                pl.BlockSpec((1, D_MODEL), lambda i, *_: (0, 0)),
                pl.BlockSpec((INT_ROWS, rows), lambda i, *_: (0, i)),
                pl.BlockSpec((rows, LANES), row),
            ],
            out_specs=pl.BlockSpec(memory_space=pl.ANY),
            scratch_shapes=[pltpu.VMEM((2 * DISPATCH_TILES, LOCAL_ROWS, XS_COLS), BF16),
                            pltpu.VMEM((EXPERT_TILE, XS_COLS), BF16),
                            pltpu.SemaphoreType.DMA((2 * DISPATCH_TILES + 1,))],
        ),
        out_shape=jax.ShapeDtypeStruct((NR, XS_COLS), BF16),
        compiler_params=pltpu.CompilerParams(
            dimension_semantics=("arbitrary",), vmem_limit_bytes=VMEM_LIMIT),
        name="moe_dispatch",
    )(plan["goff"], plan["lens"], plan["loff"], plan["gap_start"], plan["gap_len"], plan["nval"],
      plan["nbig"], plan["nsmall"], h, gain, lpos, tok)


def _experts_kernel(tidx_ref, texp_ref, nval_ref, xs_ref, wg_ref, wu_ref, wd_ref, ys_ref, wg16, wu16, wd16):
    j = pl.program_id(0)
    active = j < nval_ref[0]
    new_expert = (j == 0) | (texp_ref[j] != texp_ref[jnp.maximum(j - 1, 0)])

    @pl.when(active & new_expert)
    def _():
        wg16[...] = wg_ref[0].astype(BF16)
        wu16[...] = wu_ref[0].astype(BF16)
        wd16[...] = wd_ref[0].astype(BF16)

    @pl.when(active)
    def _():
        x = xs_ref[:, 0:D_MODEL]
        extra = xs_ref[:, D_MODEL:XS_COLS].astype(F32)
        lane = lax.broadcasted_iota(jnp.int32, extra.shape, 1)

        def lanes(lo, hi):
            return jnp.sum(jnp.where((lane >= lo) & (lane < hi), extra, 0.0), axis=1, keepdims=True)

        first = lanes(TOK_EXPERT1, TOK_EXPERT1 + 1) == texp_ref[j].astype(F32)
        gate = jnp.where(first, lanes(TOK_GATE1, TOK_GATE2), lanes(TOK_GATE2, TOK_POS1))
        a = _dot(x, wg16[...])
        b = _dot(x, wu16[...])
        y = _dot((_silu(a) * b).astype(BF16), wd16[...])
        ys_ref[...] = (gate * y).astype(BF16)

    @pl.when(jnp.logical_not(active))
    def _():
        ys_ref[...] = jnp.zeros_like(ys_ref)


def _experts(plan, xs, layer, wg, wu, wd):
    tile = lambda j, tidx, texp, nval: (tidx[j], 0)
    expert = lambda j, tidx, texp, nval: (layer, texp[j], 0, 0)
    return pl.pallas_call(
        _experts_kernel,
        grid_spec=pltpu.PrefetchScalarGridSpec(
            num_scalar_prefetch=3,
            grid=(NT_MAX,),
            in_specs=[
                pl.BlockSpec((EXPERT_TILE, XS_COLS), tile),
                pl.BlockSpec((None, 1, D_MODEL, D_EXPERT), expert),
                pl.BlockSpec((None, 1, D_MODEL, D_EXPERT), expert),
                pl.BlockSpec((None, 1, D_EXPERT, D_MODEL), expert),
            ],
            out_specs=pl.BlockSpec((EXPERT_TILE, D_MODEL), lambda j, *_: (j, 0)),
            scratch_shapes=[pltpu.VMEM((D_MODEL, D_EXPERT), BF16),
                            pltpu.VMEM((D_MODEL, D_EXPERT), BF16),
                            pltpu.VMEM((D_EXPERT, D_MODEL), BF16)],
        ),
        out_shape=jax.ShapeDtypeStruct((NR, D_MODEL), BF16),
        compiler_params=pltpu.CompilerParams(
            dimension_semantics=("arbitrary",), vmem_limit_bytes=VMEM_LIMIT),
        name="moe_experts",
    )(plan["tidx"], plan["texp"], plan["nval"], xs, wg, wu, wd)


def _combined_rows(goff_ref, lens_ref, loff_ref, nbig_ref, nsmall_ref, h_ref, tok_ref, ys_ref, local, sems):
    i = pl.program_id(0)
    nslots = 2 * COMBINE_TILES

    def run_copy(s, src, dst, rows):
        return pltpu.make_async_copy(ys_ref.at[pl.ds(_aligned(src), rows), :],
                                     local.at[s, pl.ds(_aligned(dst), rows), :], sems.at[s])

    def start_step(step):
        for t in range(COMBINE_TILES):
            tile = step * COMBINE_TILES + t
            s = lax.rem(tile, nslots)
            _for_each_run(tile, lens_ref,
                          lambda k, q, rows, s=s: run_copy(s, goff_ref[k] + q, loff_ref[k] + q, rows).start())

    @pl.when(i == 0)
    def _():
        local[...] = jnp.zeros_like(local)
        start_step(0)

    @pl.when(i < pl.num_programs(0) - 1)
    def _():
        start_step(i + 1)

    for t in range(COMBINE_TILES):
        slot = lax.rem(i, 2) * COMBINE_TILES + t
        _wait_runs(lambda rows, slot=slot: run_copy(slot, 0, 0, rows), i * COMBINE_TILES + t, nbig_ref, nsmall_ref)
    ciota = lax.broadcasted_iota(jnp.int32, (MOE_TM, LOCAL_ROWS), 1)
    out = []
    for t in range(COMBINE_TILES):
        rows_t = slice(t * MOE_TM, (t + 1) * MOE_TM)
        slot = lax.rem(i, 2) * COMBINE_TILES + t
        tok = tok_ref[rows_t, :]
        p1 = tok[:, TOK_POS1:TOK_POS1 + 1].astype(jnp.int32)
        p2 = tok[:, TOK_POS2:TOK_POS2 + 1].astype(jnp.int32)
        sel = jnp.where((ciota == p1) | (ciota == p2), 1.0, 0.0).astype(BF16)
        out.append(h_ref[rows_t, :] + _dot(sel, local[slot]))
    return jnp.concatenate(out, axis=0)


def _combine_pool_kernel(goff_ref, lens_ref, loff_ref, nbig_ref, nsmall_ref, h_ref, tok_ref,
                         g_ref, pw_ref, pb_ref, ps_ref, ys_ref, o_ref, local, mixed, buf, sems):
    mixed[...] = _combined_rows(goff_ref, lens_ref, loff_ref, nbig_ref, nsmall_ref, h_ref, tok_ref, ys_ref,
                                local, sems)
    _pool_rows(pl.program_id(0), mixed, g_ref, pw_ref, pb_ref, ps_ref, o_ref, buf)


def _combine_final_kernel(goff_ref, lens_ref, loff_ref, nbig_ref, nsmall_ref, h_ref, tok_ref, g_ref, ys_ref,
                          o_ref, local, stage, sems, out_sems):
    i = pl.program_id(0)
    last = pl.num_programs(0) - 1
    slot = lax.rem(i, 2)
    per_tile = COMBINE_ROWS // CHUNK

    def for_each_seq_chunk(step, fn):
        s = lax.rem(step, 2)
        for k in range(per_tile):
            chunk = step * per_tile + k
            c = lax.rem(chunk, NCHUNK)
            dst = (lax.div(chunk, NCHUNK) * (SEQ // CHUNK) + c - 1) * CHUNK

            @pl.when(c > 0)
            def _():
                fn(pltpu.make_async_copy(stage.at[s, pl.ds(k * CHUNK, CHUNK), :],
                                         o_ref.at[pl.ds(pl.multiple_of(dst, CHUNK), CHUNK), :],
                                         out_sems.at[s]))

    @pl.when(i >= 2)
    def _():
        for_each_seq_chunk(i - 2, lambda copy: copy.wait())

    out = _combined_rows(goff_ref, lens_ref, loff_ref, nbig_ref, nsmall_ref, h_ref, tok_ref, ys_ref,
                         local, sems)
    stage[slot] = _rms(out, g_ref[...])
    for_each_seq_chunk(i, lambda copy: copy.start())

    @pl.when(i == last)
    def _():
        @pl.when(i >= 1)
        def _():
            for_each_seq_chunk(i - 1, lambda copy: copy.wait())

        for_each_seq_chunk(i, lambda copy: copy.wait())


def _combine(plan, h, tok, ys, pool=None, final_gain=None):
    final = final_gain is not None
    row = lambda i, *_: (i, 0)
    whole = lambda a: pl.BlockSpec(a.shape, lambda i, *_: (0,) * a.ndim)
    extra = [final_gain] if final else list(pool)
    scratch = [pltpu.VMEM((2 * COMBINE_TILES, LOCAL_ROWS, D_MODEL), BF16)]
    if final:
        scratch.append(pltpu.VMEM((2, COMBINE_ROWS, D_MODEL), F32))
    else:
        scratch += [pltpu.VMEM((COMBINE_ROWS, D_MODEL), F32), pltpu.VMEM((CHUNK + POOL_MAX_WINDOW, D_MODEL), F32)]
    sems = [pltpu.SemaphoreType.DMA((2 * COMBINE_TILES,))] + ([pltpu.SemaphoreType.DMA((2,))] if final else [])
    return pl.pallas_call(
        _combine_final_kernel if final else _combine_pool_kernel,
        grid_spec=pltpu.PrefetchScalarGridSpec(
            num_scalar_prefetch=5,
            grid=(N_MOE_TILES // COMBINE_TILES,),
            in_specs=([pl.BlockSpec((COMBINE_ROWS, D_MODEL), row), pl.BlockSpec((COMBINE_ROWS, LANES), row)]
                      + [whole(a) for a in extra] + [pl.BlockSpec(memory_space=pl.ANY)]),
            out_specs=(pl.BlockSpec(memory_space=pl.ANY) if final
                       else pl.BlockSpec((COMBINE_ROWS, D_MODEL), row)),
            scratch_shapes=scratch + sems,
        ),
        out_shape=jax.ShapeDtypeStruct((BATCH * SEQ if final else TP, D_MODEL), F32),
        compiler_params=pltpu.CompilerParams(
            dimension_semantics=("arbitrary",), vmem_limit_bytes=VMEM_LIMIT),
        name="moe_combine_final" if final else "moe_combine_pool",
    )(plan["goff"], plan["lens"], plan["loff"], plan["nbig"], plan["nsmall"], h, tok, *extra, ys)


def _row(v):
    return v.reshape(1, -1).astype(F32)


def _pad_lanes(v):
    return jnp.pad(_row(v), ((0, 0), (0, LANES - v.shape[-1])))


def _pair_blockdiag(w):
    w = w.reshape(LRU_WIDTH // LANES, 2, 64, 64)
    z = jnp.zeros_like(w[:, 0])
    top = jnp.concatenate([w[:, 0], z], axis=2)
    bot = jnp.concatenate([z, w[:, 1]], axis=2)
    return jnp.concatenate([top, bot], axis=1).astype(BF16)


def _moe_layer(h, layer, pool, final_gain, valid, ffn_norm, rgw, rgb, rew, reb, wg, wu, wd):
    wr = jnp.zeros((D_MODEL, LANES), F32)
    wr = wr.at[:, 0:MOE_GROUPS].set(rgw[layer])
    wr = wr.at[:, ROUTER_EXPERT_ROW:ROUTER_EXPERT_ROW + MOE_EXPERTS].set(rew[layer])
    br = jnp.zeros((1, LANES), F32)
    br = br.at[0, 0:MOE_GROUPS].set(rgb[layer])
    br = br.at[0, ROUTER_EXPERT_ROW:ROUTER_EXPERT_ROW + MOE_EXPERTS].set(reb[layer])
    gain = _row(ffn_norm[layer])
    lpos, tok, runlen, runoff = _router(h, valid, gain, wr, br)
    plan = _moe_plan(runlen, runoff)
    xs = _dispatch(plan, h, gain, lpos, tok)
    ys = _experts(plan, xs, layer, wg, wu, wd)
    return _combine(plan, h, tok, ys, pool, final_gain)


def kernel(x, meta_tokens, norm_final, mix_norm_even, w_in, ssd_conv_w, ssd_conv_b, ssd_dt_bias, ssd_a_log, ssd_d, ssd_norm, lru_conv_w, lru_conv_b, lru_w_a, lru_b_a, lru_w_x, lru_b_x, lru_lambda, w_out, mix_norm_odd, pool_w, pool_b, pool_scale, ffn_norm, router_group_w, router_group_b, router_expert_w, router_expert_b, expert_w_gate, expert_w_up, expert_w_down):
    meta_chunk = jnp.concatenate([jnp.zeros((PAD, D_MODEL), F32), meta_tokens.astype(F32)], axis=0)
    rows = jnp.arange(CHUNK)
    meta_chunk = meta_chunk[(rows % 8) * SEG + rows // 8]
    valid = ((jnp.arange(TP, dtype=jnp.int32) % LP) >= PAD).astype(jnp.int32).reshape(1, TP)
    moe_args = (valid, ffn_norm, router_group_w, router_group_b, router_expert_w, router_expert_b,
                expert_w_gate, expert_w_up, expert_w_down)

    wi = w_in[0]
    w_proj = jnp.concatenate(
        [wi[:, 0:1024], wi[:, 1024:2560], wi[:, 2576:3600], wi[:, 3600:4624], wi[:, 2560:2576],
         jnp.zeros((D_MODEL, LANES - SSD_HEADS), F32)], axis=1).astype(BF16)
    piece_head = jnp.where(jnp.arange(LANES) < HEAD_PIECES * SSD_HEADS, jnp.arange(LANES) % SSD_HEADS, -1)
    expand = (piece_head[:, None] == (jnp.arange(SSD_WIDTH) // SSD_HEAD_DIM)[None, :]).astype(BF16)
    params = [
        _row(mix_norm_even[0]), w_proj,
        ssd_conv_w[0].astype(F32), _row(ssd_conv_b[0]), _pad_lanes(ssd_dt_bias[0]), _pad_lanes(ssd_a_log[0]),
        _row(jnp.repeat(ssd_d[0], SSD_HEAD_DIM)), _row(ssd_norm[0]),
        lru_conv_w[0].astype(F32), _row(lru_conv_b[0]),
        jnp.concatenate([_pair_blockdiag(lru_w_a[0]), _pair_blockdiag(lru_w_x[0])], axis=2),
        _row(lru_b_a[0]), _row(lru_b_x[0]), _row(lru_lambda[0]),
        w_out[0].astype(BF16), expand,
    ]
    h = _mixer(x.astype(F32), meta_chunk, params).reshape(TP, D_MODEL)
    pool = (_row(mix_norm_odd[0]), pool_w[0].astype(BF16), _row(pool_b[0]), _row(pool_scale[0]))
    h = _moe_layer(h, 0, pool, None, *moe_args)
    out = _moe_layer(h, 1, None, _row(norm_final), *moe_args)
    return out.reshape(BATCH, SEQ, D_MODEL)
```

```python
import functools
import math

import jax
import jax.numpy as jnp
from jax import lax
from jax.experimental import pallas as pl
from jax.experimental.pallas import tpu as pltpu

F32 = jnp.float32
BF16 = jnp.bfloat16
HIGHEST = lax.Precision.HIGHEST

D_MODEL = 1024
BATCH = 8
SEQ = 2048
N_META = 16
RMS_EPS = 1e-6
CONV_WIDTH = 4
CHUNK = 128
PAD = CHUNK - N_META
LP = PAD + N_META + SEQ
NCHUNK = LP // CHUNK
TP = BATCH * LP

SSD_HEADS = 16
SSD_HEAD_DIM = 64
SSD_WIDTH = 1024
SSD_GROUPS = 2
SSD_STATE = 128
SSD_GROUP_WIDTH = SSD_WIDTH // SSD_GROUPS
SSD_CONV_DIM = SSD_WIDTH + 2 * SSD_GROUPS * SSD_STATE
LRU_WIDTH = 1024
LRU_C = 8.0
LANES = 128
POOL_WINDOWS = (2, 4, 8, 16)
POOL_GROUP_DIM = 256
POOL_MAX_WINDOW = 16
MOE_GROUPS = 4
MOE_PER_GROUP = 4
MOE_EXPERTS = 16
D_EXPERT = 512
ROUTER_EXPERT_ROW = 8

VMEM_LIMIT = 56 * 1024 * 1024


def _dot(a, b, precision=None):
    return jnp.dot(a, b, preferred_element_type=F32, precision=precision)


def _rms(x, gain):
    ms = jnp.mean(x * x, axis=-1, keepdims=True)
    return x * lax.rsqrt(ms + RMS_EPS) * gain


def _silu(x):
    return x * jax.nn.sigmoid(x)


def _softplus(x):
    return jnp.maximum(x, 0.0) + jnp.log1p(jnp.exp(-jnp.abs(x)))


def _row_ids(shape, chunk):
    return lax.broadcasted_iota(jnp.int32, shape, 0) + chunk * CHUNK


MIX_B = 2
N_MIX_ITEMS = (BATCH // MIX_B) * NCHUNK
_Z0, _X0, _G0, _L0, _T0, _PEND = 0, 1024, 2560, 3584, 4608, 4736
HEAD_PIECES = 3
PROJ_PIECE = 256


def _carried(c, state):
    return jnp.where(jnp.full(state.shape, c, jnp.int32) == 0, 0.0, state)


SEG = CHUNK // 8


def _time_ids(shape, axis=0):
    p = lax.broadcasted_iota(jnp.int32, shape, axis)
    return lax.bitwise_and(p, 7) * SEG + lax.shift_right_logical(p, 3)


def _load_interleaved(slab, src):
    cols = []
    for j in range(D_MODEL // LANES):
        slab[j] = src[:, j * LANES:(j + 1) * LANES]
        cols.append(jnp.concatenate([slab[j, pl.ds(k, 8, stride=SEG), :] for k in range(SEG)], axis=0))
    return jnp.concatenate(cols, axis=1)


def _store_time_order(slab, dst, value):
    for j in range(D_MODEL // LANES):
        for k in range(SEG):
            slab[j, pl.ds(k, 8, stride=SEG), :] = value[k * 8:(k + 1) * 8, j * LANES:(j + 1) * LANES]
        dst[:, j * LANES:(j + 1) * LANES] = slab[j]


def _causal_conv(hist, c, x, w_ref, b_ref):
    taps = CONV_WIDTH - 1
    prev = _carried(c, hist[...])
    hist[...] = x[(SEG - taps) * 8:, :]
    first_segment = lax.broadcasted_iota(jnp.int32, (8, x.shape[1]), 0) == 0
    groups = [x[k * 8:(k + 1) * 8, :] for k in range(SEG)]
    wrapped = {i: jnp.where(first_segment, pltpu.roll(prev[(taps - i) * 8:(taps - i + 1) * 8, :], 1, 0),
                            pltpu.roll(groups[SEG - i], 1, 0)) for i in range(1, taps + 1)}
    out = []
    for k in range(SEG):
        acc = b_ref[...] + w_ref[taps:taps + 1, :] * groups[k]
        for j in range(1, taps + 1):
            earlier = groups[k - j] if k >= j else wrapped[j - k]
            acc = acc + w_ref[taps - j:taps - j + 1, :] * earlier
        out.append(acc)
    return jnp.concatenate(out, axis=0)


def _head_pieces(v):
    lane = lax.broadcasted_iota(jnp.int32, v.shape, 1)
    rest = jnp.where(lane < SSD_HEADS, v, 0.0)
    packed = None
    for k in range(HEAD_PIECES):
        piece = rest.astype(BF16).astype(F32)
        rest = rest - piece
        moved = piece if k == 0 else pltpu.roll(piece, k * SSD_HEADS, 1)
        packed = moved if packed is None else packed + moved
    return packed.astype(BF16)


def _expand_heads(v, exp_ref):
    return _dot(_head_pieces(v), exp_ref[...])


def _cumsum_over_time(order, v):
    sums = _dot(order, _head_pieces(v))
    total = sums
    for k in range(1, HEAD_PIECES):
        total = total + pltpu.roll(sums, LANES - k * SSD_HEADS, 1)
    lane = lax.broadcasted_iota(jnp.int32, v.shape, 1)
    return jnp.where(lane < SSD_HEADS, total, 0.0)


def _ssd_chunk(c, z, xbc, dt_raw, hist, st, cw_ref, cb_ref, dtb_ref, alog_ref, dsk_ref, ng_ref, exp_ref):
    xc = _silu(_causal_conv(hist, c, xbc, cw_ref, cb_ref))
    yield
    xs = xc[:, 0:SSD_WIDTH]
    valid_s = _time_ids((CHUNK, 2 * SSD_STATE)) + c * CHUNK >= PAD
    bm = jnp.where(valid_s, xc[:, SSD_WIDTH:SSD_WIDTH + 2 * SSD_STATE], 0.0)
    cm = jnp.where(valid_s, xc[:, SSD_WIDTH + 2 * SSD_STATE:], 0.0)

    li = _time_ids((CHUNK, CHUNK), 0)
    si = _time_ids((CHUNK, CHUNK), 1)
    causal = si <= li
    dt = _softplus(dt_raw + dtb_ref[...])
    dt = jnp.where(li + c * CHUNK >= PAD, dt, 0.0)
    adt = dt * (-jnp.exp(alog_ref[...]))
    a_cs = _cumsum_over_time(jnp.where(causal, 1.0, 0.0).astype(BF16), adt)
    a_last = a_cs[CHUNK - 1:CHUNK, :]
    ea = jnp.exp(a_cs)
    dt_x = _expand_heads(dt, exp_ref)
    w_x = _expand_heads(jnp.exp(a_last - a_cs) * dt, exp_ref)
    ea_x = _expand_heads(ea, exp_ref)
    xdt = xs * dt_x
    a_cs_t = a_cs.T
    yield
    lane = lax.broadcasted_iota(jnp.int32, (CHUNK, LANES), 1)

    ys = []
    for g in range(SSD_GROUPS):
        gsl = slice(g * SSD_GROUP_WIDTH, (g + 1) * SSD_GROUP_WIDTH)
        bg = bm[:, g * SSD_STATE:(g + 1) * SSD_STATE]
        cg16 = cm[:, g * SSD_STATE:(g + 1) * SSD_STATE].astype(BF16)
        cbm = lax.dot_general(cg16, bg.astype(BF16), (((1,), (1,)), ((), ())),
                              preferred_element_type=F32)
        s_in = _carried(c, st[:, gsl])
        y_off = _dot(cg16, s_in.astype(BF16)) * ea_x[:, gsl]
        s_new = _dot(bg.T.astype(BF16), (w_x[:, gsl] * xs[:, gsl]).astype(BF16))
        st[:, gsl] = ea_x[CHUNK - 1:CHUNK, gsl] * s_in + s_new
        for j in range(SSD_GROUP_WIDTH // LANES):
            xp = xdt[:, g * SSD_GROUP_WIDTH + j * LANES:
                     g * SSD_GROUP_WIDTH + (j + 1) * LANES].astype(BF16)
            weights, values = [], []
            for hh in range(LANES // SSD_HEAD_DIM):
                hd = g * (SSD_HEADS // SSD_GROUPS) + j * (LANES // SSD_HEAD_DIM) + hh
                seg = a_cs[:, hd:hd + 1] - a_cs_t[hd:hd + 1, :]
                dec = jnp.exp(jnp.where(causal, seg, -1e30))
                weights.append((cbm * dec).astype(BF16))
                own = (lane >= hh * SSD_HEAD_DIM) & (lane < (hh + 1) * SSD_HEAD_DIM)
                values.append(jnp.where(own, xp, jnp.zeros_like(xp)))
            y_diag = _dot(jnp.concatenate(weights, axis=1), jnp.concatenate(values, axis=0))
            ys.append(y_diag + y_off[:, j * LANES:(j + 1) * LANES])
            if j % 2:
                yield
    y = jnp.concatenate(ys, axis=1) + dsk_ref[...] * xs
    y = y * _silu(z)
    normed = []
    for g in range(SSD_GROUPS):
        yg = y[:, g * SSD_GROUP_WIDTH:(g + 1) * SSD_GROUP_WIDTH]
        normed.append(yg * lax.rsqrt(jnp.mean(yg * yg, axis=-1, keepdims=True) + RMS_EPS))
    return jnp.concatenate(normed, axis=1) * ng_ref[...]


def _lru_chunk(c, gt, lin, hist, lc, lcw_ref, lcb_ref, wax_ref, ba_ref, bx_ref, lam_ref):
    xb = _causal_conv(hist, c, lin, lcw_ref, lcb_ref)
    yield
    xb16 = xb.astype(BF16)
    both = [_dot(xb16[:, j * LANES:(j + 1) * LANES], wax_ref[j]) for j in range(LRU_WIDTH // LANES)]
    r = jax.nn.sigmoid(jnp.concatenate([p[:, 0:LANES] for p in both], axis=1) + ba_ref[...])
    ig = jax.nn.sigmoid(jnp.concatenate([p[:, LANES:2 * LANES] for p in both], axis=1) + bx_ref[...])
    log_a = (-LRU_C * _softplus(-lam_ref[...])) * r
    a = jnp.exp(log_a)
    mult = jnp.sqrt(jnp.tanh(-log_a) * (a * a + 1.0))
    grow = _time_ids((CHUNK, LRU_WIDTH)) + c * CHUNK
    mult = jnp.where(grow == PAD, 1.0, mult)
    u = jnp.where(grow >= PAD, mult * (ig * xb), 0.0)
    yield

    local, decay = [], []
    for k in range(SEG):
        a8 = a[k * 8:(k + 1) * 8, :]
        u8 = u[k * 8:(k + 1) * 8, :]
        local.append(u8 if k == 0 else a8 * local[-1] + u8)
        decay.append(a8 if k == 0 else a8 * decay[-1])
        if k == SEG // 2:
            yield
    r8 = lax.broadcasted_iota(jnp.int32, (8, LRU_WIDTH), 0)
    end, span = local[-1], decay[-1]
    for d in (1, 2, 4):
        end_sh = jnp.where(r8 >= d, pltpu.roll(end, d, 0), 0.0)
        span_sh = jnp.where(r8 >= d, pltpu.roll(span, d, 0), 1.0)
        end = span * end_sh + end
        span = span * span_sh
    carry = _carried(c, lc[0:1, :])
    end = end + span * carry
    lc[0:1, :] = end[7:8, :]
    entering = jnp.where(r8 >= 1, pltpu.roll(end, 1, 0), carry)
    hs = [local[k] + decay[k] * entering for k in range(SEG)]
    gelu = 0.5 * gt * (1.0 + jnp.tanh(math.sqrt(2.0 / math.pi) * (gt + 0.044715 * (gt * gt * gt))))
    return jnp.concatenate(hs, axis=0) * gelu


def _mixer_kernel(x_ref, meta_ref, gin_ref, wzx_ref, wgl_ref, wdt_ref,
                  cw_ref, cb_ref, dtb_ref, alog_ref, dsk_ref, ng_ref,
                  lcw_ref, lcb_ref, wax_ref, ba_ref, bx_ref, lam_ref,
                  wout_ref, exp_ref, o_ref, h_cur, h_next, p_cur, p_next, hist_x, hist_l, st, lc,
                  slab_in, slab_out):
    s = pl.program_id(0)
    p_chunk = lax.rem(jnp.minimum(s, N_MIX_ITEMS - 1), NCHUNK)
    c = lax.rem(jnp.maximum(s - 1, 0), NCHUNK)

    @pl.when(s == 0)
    def _():
        h_next[...] = jnp.zeros_like(h_next)
        p_next[...] = jnp.zeros_like(p_next)
        hist_x[...] = jnp.zeros_like(hist_x)
        hist_l[...] = jnp.zeros_like(hist_l)
        st[...] = jnp.zeros_like(st)
        lc[...] = jnp.zeros_like(lc)

    h_cur[...] = h_next[...]
    p_cur[...] = p_next[...]

    from_meta = jnp.full((CHUNK, D_MODEL), p_chunk, jnp.int32) == 0
    for b in range(MIX_B):
        h_next[b * CHUNK:(b + 1) * CHUNK, :] = jnp.where(
            from_meta, meta_ref[...], _load_interleaved(slab_in, x_ref.at[b]))
    hn = _rms(h_next[...], gin_ref[...]).astype(BF16)
    pieces = [(w_ref, dst, lo, min(lo + PROJ_PIECE, w_ref.shape[1]))
              for w_ref, dst in ((wzx_ref, _Z0), (wgl_ref, _G0), (wdt_ref, _T0))
              for lo in range(0, w_ref.shape[1], PROJ_PIECE)]

    def project_piece():
        if pieces:
            w_ref, dst, lo, hi = pieces.pop(0)
            p_next[:, dst + lo:dst + hi] = _dot(hn, w_ref[:, lo:hi])

    def rows(b):
        return slice(b * CHUNK, (b + 1) * CHUNK)

    ssd = [_ssd_chunk(c, p_cur[rows(b), _Z0:_X0], p_cur[rows(b), _X0:_G0], p_cur[rows(b), _T0:_PEND],
                      hist_x.at[b], st.at[b], cw_ref, cb_ref, dtb_ref, alog_ref, dsk_ref, ng_ref, exp_ref)
           for b in range(MIX_B)]
    lru = [_lru_chunk(c, p_cur[rows(b), _G0:_L0], p_cur[rows(b), _L0:_T0], hist_l.at[b], lc.at[b],
                      lcw_ref, lcb_ref, wax_ref, ba_ref, bx_ref, lam_ref)
           for b in range(MIX_B)]

    def finish(stages):
        while True:
            try:
                next(stages)
            except StopIteration as done:
                return done.value
            project_piece()

    h = h_cur[...]
    project_piece()
    mixed = []
    for b in range(MIX_B):
        y_ssd = finish(ssd[b])
        y_lru = finish(lru[b])
        mixed.append(jnp.concatenate([y_ssd, y_lru], axis=1).astype(BF16))
    while pieces:
        project_piece()
    out = _dot(jnp.concatenate(mixed, axis=0), wout_ref[...]) + h
    keep = _time_ids((CHUNK, D_MODEL)) + c * CHUNK >= PAD
    for b in range(MIX_B):
        _store_time_order(slab_out, o_ref.at[b], jnp.where(keep, out[b * CHUNK:(b + 1) * CHUNK, :], 0.0))


def _mixer(x, meta_chunk, params):
    full = lambda a: pl.BlockSpec(a.shape, lambda s: (0,) * a.ndim, pipeline_mode=pl.Buffered(1))

    def x_block(s):
        item = jnp.minimum(s, N_MIX_ITEMS - 1)
        return item // NCHUNK, jnp.maximum(item % NCHUNK - 1, 0), 0

    def out_block(s):
        item = jnp.maximum(s - 1, 0)
        return item // NCHUNK, item % NCHUNK, 0

    return pl.pallas_call(
        _mixer_kernel,
        grid=(N_MIX_ITEMS + 1,),
        in_specs=[pl.BlockSpec((MIX_B, CHUNK, D_MODEL), x_block), full(meta_chunk)] + [full(p) for p in params],
        out_specs=pl.BlockSpec((MIX_B, CHUNK, D_MODEL), out_block),
        out_shape=jax.ShapeDtypeStruct((BATCH, LP, D_MODEL), F32),
        scratch_shapes=[
            pltpu.VMEM((MIX_B * CHUNK, D_MODEL), F32),
            pltpu.VMEM((MIX_B * CHUNK, D_MODEL), F32),
            pltpu.VMEM((MIX_B * CHUNK, _PEND), F32),
            pltpu.VMEM((MIX_B * CHUNK, _PEND), F32),
            pltpu.VMEM((MIX_B, 8 * (CONV_WIDTH - 1), SSD_CONV_DIM), F32),
            pltpu.VMEM((MIX_B, 8 * (CONV_WIDTH - 1), LRU_WIDTH), F32),
            pltpu.VMEM((MIX_B, SSD_STATE, SSD_WIDTH), F32),
            pltpu.VMEM((MIX_B, 8, LRU_WIDTH), F32),
            pltpu.VMEM((D_MODEL // LANES, CHUNK, LANES), F32),
            pltpu.VMEM((D_MODEL // LANES, CHUNK, LANES), F32),
        ],
        compiler_params=pltpu.CompilerParams(
            dimension_semantics=("arbitrary",), vmem_limit_bytes=VMEM_LIMIT),
        name="ssd_lru_mixer",
    )(x, meta_chunk, *params)


def _pool_rows(step, h_ref, g_ref, pw_ref, pb_ref, ps_ref, o_ref, buf):
    base = POOL_MAX_WINDOW
    for k in range(h_ref.shape[0] // CHUNK):
        chunk = step * (h_ref.shape[0] // CHUNK) + k
        c = lax.rem(chunk, NCHUNK)

        @pl.when(c == 0)
        def _():
            buf[0:base, :] = jnp.zeros((base, D_MODEL), F32)

        h = h_ref[k * CHUNK:(k + 1) * CHUNK, :]
        hn = _rms(h, g_ref[...])
        buf[base:base + CHUNK, :] = hn
        pos = _row_ids((CHUNK, POOL_GROUP_DIM), c) - PAD
        outs = []
        for g, w in enumerate(POOL_WINDOWS):
            sl = slice(g * POOL_GROUP_DIM, (g + 1) * POOL_GROUP_DIM)
            ws = buf[:, sl]
            s = 1
            while s < w:
                ws = ws + pltpu.roll(ws, s, 0)
                s *= 2
            ws = ws[base:, :]
            count = jnp.clip(pos + 1, 1, w).astype(F32)
            pooled = ws / count - hn[:, sl]
            outs.append(_dot(pooled.astype(BF16), pw_ref[g]))
        buf[0:base, :] = buf[CHUNK:CHUNK + base, :]
        y = (jnp.concatenate(outs, axis=1) + pb_ref[...]) * ps_ref[...]
        o_ref[k * CHUNK:(k + 1) * CHUNK, :] = jnp.where(_row_ids((CHUNK, D_MODEL), c) >= PAD, h + y, 0.0)


MOE_TM = 512
N_MOE_TILES = TP // MOE_TM
DISPATCH_TILES = 2
ROUTER_TILES = 2
COMBINE_TILES = 2
COMBINE_ROWS = COMBINE_TILES * MOE_TM
RUN_ALIGN = 16
RUN_SHIFT = 4
RUN_BIG = 64
RUN_BIG_SHIFT = 6
LOCAL_ROWS = 1280
EXPERT_TILE = 512
N_ROUTED = BATCH * (N_META + SEQ)
MAX_ROWS = (2 * N_ROUTED + N_MOE_TILES * MOE_EXPERTS * (RUN_ALIGN - 1)
            + MOE_EXPERTS * (EXPERT_TILE - 1))
NT_MAX = -(-MAX_ROWS // EXPERT_TILE)
NR = NT_MAX * EXPERT_TILE
XS_COLS = D_MODEL + LANES
INT_ROWS = 8
TOK_GATE1, TOK_GATE2, TOK_POS1, TOK_POS2, TOK_EXPERT1 = 0, 3, 6, 7, 8
assert LOCAL_ROWS >= 2 * MOE_TM + MOE_EXPERTS * (RUN_ALIGN - 1) and LOCAL_ROWS % LANES == 0


def _first_argmax(vals):
    best, idx = vals[0], jnp.zeros(vals[0].shape, jnp.int32)
    for k in range(1, len(vals)):
        better = vals[k] > best
        idx = jnp.where(better, k, idx)
        best = jnp.where(better, vals[k], best)
    return idx, best


def _softmax_rows(vals):
    m = functools.reduce(jnp.maximum, vals)
    ex = [jnp.exp(v - m) for v in vals]
    tot = functools.reduce(lambda p, q: p + q, ex)
    return [e / tot for e in ex]


def _bf16_pieces(x):
    hi = x.astype(BF16).astype(F32)
    rest = x - hi
    mid = rest.astype(BF16).astype(F32)
    lo = (rest - mid).astype(BF16).astype(F32)
    return [hi, mid, lo]


def _router_kernel(h_ref, valid_ref, g_ref, wr_ref, br_ref, lpos_ref, tok_ref, runlen_ref, runoff_ref, before):
    @pl.when(pl.program_id(0) == 0)
    def _():
        si = lax.broadcasted_iota(jnp.int32, (MOE_TM, MOE_TM), 0)
        ti = lax.broadcasted_iota(jnp.int32, (MOE_TM, MOE_TM), 1)
        before[...] = jnp.where(si < ti, 1.0, 0.0).astype(BF16)

    for t in range(ROUTER_TILES):
        tokens = pl.ds(t * MOE_TM, MOE_TM)
        experts = pl.ds(t * MOE_EXPERTS, MOE_EXPERTS)
        _route_tile(h_ref.at[tokens, :], valid_ref.at[:, tokens], g_ref, wr_ref, br_ref,
                    lpos_ref.at[:, tokens], tok_ref.at[tokens, :], runlen_ref.at[experts, :],
                    runoff_ref.at[experts, :], before)


def _route_tile(h_ref, valid_ref, g_ref, wr_ref, br_ref, lpos_ref, tok_ref, runlen_ref, runoff_ref, before):
    hn = _rms(h_ref[...], g_ref[...])
    hn_hi = hn.astype(BF16)
    hn_lo = (hn - hn_hi.astype(F32)).astype(BF16)
    by_head = _dot(hn_hi, wr_ref[...])
    logits = (by_head[:, 0:LANES] + (by_head[:, LANES:2 * LANES] + _dot(hn_lo, wr_ref[:, 0:LANES]))
              + br_ref[...])
    lt = logits.T
    p_group = _softmax_rows([lt[k:k + 1, :] for k in range(MOE_GROUPS)])
    g_sel, p_g = _first_argmax(p_group)
    fine = []
    for k in range(MOE_PER_GROUP):
        f = lt[ROUTER_EXPERT_ROW + k:ROUTER_EXPERT_ROW + k + 1, :]
        for g in range(1, MOE_GROUPS):
            r0 = ROUTER_EXPERT_ROW + g * MOE_PER_GROUP + k
            f = jnp.where(g_sel == g, lt[r0:r0 + 1, :], f)
        fine.append(f)
    q = _softmax_rows(fine)
    i1, t1 = _first_argmax(q)
    i2, t2 = _first_argmax([jnp.where(i1 == k, -1.0, q[k]) for k in range(MOE_PER_GROUP)])
    tot = t1 + t2
    gate1 = p_g * (t1 / tot)
    gate2 = p_g * (t2 / tot)

    valid = valid_ref[...] > 0
    e1 = jnp.where(valid, g_sel * MOE_PER_GROUP + i1, -1)
    e2 = jnp.where(valid, g_sel * MOE_PER_GROUP + i2, -1)
    erow = lax.broadcasted_iota(jnp.int32, (MOE_EXPERTS, MOE_TM), 0)
    hit1 = erow == e1
    hit2 = erow == e2
    onehot = jnp.where(hit1 | hit2, 1.0, 0.0)
    seen = _dot(onehot.astype(BF16), before[...])
    count = jnp.sum(onehot, axis=1, keepdims=True).astype(jnp.int32)
    runlen = lax.shift_left(lax.shift_right_logical(count + (RUN_ALIGN - 1), RUN_SHIFT), RUN_SHIFT)
    runlen_b = jnp.broadcast_to(runlen, (MOE_EXPERTS, LANES))
    ei = lax.broadcasted_iota(jnp.int32, (MOE_EXPERTS, MOE_EXPERTS), 0)
    ej = lax.broadcasted_iota(jnp.int32, (MOE_EXPERTS, MOE_EXPERTS), 1)
    runoff_b = _dot(jnp.where(ej < ei, 1.0, 0.0), runlen_b.astype(F32), HIGHEST)
    runlen_ref[...] = runlen_b
    runoff_ref[...] = runoff_b.astype(jnp.int32)
    place = seen + runoff_b[:, 0:1]
    pos1 = jnp.where(valid, jnp.sum(jnp.where(hit1, place, 0.0), axis=0, keepdims=True), -1.0)
    pos2 = jnp.where(valid, jnp.sum(jnp.where(hit2, place, 0.0), axis=0, keepdims=True), -1.0)
    r8 = lax.broadcasted_iota(jnp.int32, (INT_ROWS, MOE_TM), 0)
    lpos_ref[...] = jnp.where(r8 == 0, pos1.astype(jnp.int32),
                              jnp.where(r8 == 1, pos2.astype(jnp.int32), 0))

    rows = lax.broadcasted_iota(jnp.int32, lt.shape, 0)
    table = jnp.zeros(lt.shape, F32)
    for k, piece in enumerate(_bf16_pieces(gate1)):
        table = jnp.where(rows == TOK_GATE1 + k, piece, table)
    for k, piece in enumerate(_bf16_pieces(gate2)):
        table = jnp.where(rows == TOK_GATE2 + k, piece, table)
    table = jnp.where(rows == TOK_POS1, pos1, jnp.where(rows == TOK_POS2, pos2, table))
    table = jnp.where(rows == TOK_EXPERT1, e1.astype(F32), table)
    tok_ref[...] = table.T


def _router(h, valid, gain, wr, br):
    row = lambda i: (i, 0)
    col = lambda i: (0, i)
    const = lambda i: (0, 0)
    wr_hi = wr.astype(BF16)
    wr_lo = (wr - wr_hi.astype(F32)).astype(BF16)
    return pl.pallas_call(
        _router_kernel,
        grid=(N_MOE_TILES // ROUTER_TILES,),
        in_specs=[
            pl.BlockSpec((ROUTER_TILES * MOE_TM, D_MODEL), row),
            pl.BlockSpec((1, ROUTER_TILES * MOE_TM), col),
            pl.BlockSpec((1, D_MODEL), const),
            pl.BlockSpec((D_MODEL, 2 * LANES), const),
            pl.BlockSpec((1, LANES), const),
        ],
        out_specs=[pl.BlockSpec((INT_ROWS, ROUTER_TILES * MOE_TM), col),
                   pl.BlockSpec((ROUTER_TILES * MOE_TM, LANES), row),
                   pl.BlockSpec((ROUTER_TILES * MOE_EXPERTS, LANES), row),
                   pl.BlockSpec((ROUTER_TILES * MOE_EXPERTS, LANES), row)],
        out_shape=[jax.ShapeDtypeStruct((INT_ROWS, TP), jnp.int32),
                   jax.ShapeDtypeStruct((TP, LANES), F32),
                   jax.ShapeDtypeStruct((N_MOE_TILES * MOE_EXPERTS, LANES), jnp.int32),
                   jax.ShapeDtypeStruct((N_MOE_TILES * MOE_EXPERTS, LANES), jnp.int32)],
        scratch_shapes=[pltpu.VMEM((MOE_TM, MOE_TM), BF16)],
        compiler_params=pltpu.CompilerParams(
            dimension_semantics=("arbitrary",), vmem_limit_bytes=VMEM_LIMIT),
        name="moe_router",
    )(h, valid, gain, jnp.concatenate([wr_hi, wr_lo], axis=1), br)


def _moe_plan(runlen, runoff):
    i32 = jnp.int32
    lens = runlen[:, 0].reshape(N_MOE_TILES, MOE_EXPERTS)
    total = jnp.sum(lens, axis=0)
    padded = (total + (EXPERT_TILE - 1)) // EXPERT_TILE * EXPERT_TILE
    ends = jnp.cumsum(padded)
    base = ends - padded
    goff = base[None, :] + jnp.cumsum(lens, axis=0) - lens
    nval = ends[-1] // EXPERT_TILE
    tidx = jnp.minimum(jnp.arange(NT_MAX, dtype=i32), nval - 1)
    texp = jnp.minimum(jnp.sum(tidx[:, None] >= (ends // EXPERT_TILE)[None, :], axis=1), MOE_EXPERTS - 1)
    return dict(
        goff=goff.reshape(-1).astype(i32), lens=lens.reshape(-1).astype(i32),
        loff=runoff[:, 0].astype(i32), gap_start=(base + total).astype(i32),
        gap_len=(padded - total).astype(i32), nval=nval.reshape(1).astype(i32),
        tidx=tidx.astype(i32), texp=texp.astype(i32),
        nbig=jnp.sum(lens // RUN_BIG, axis=1).astype(i32),
        nsmall=jnp.sum(lens % RUN_BIG // RUN_ALIGN, axis=1).astype(i32))


def _aligned(x):
    return pl.multiple_of(x, RUN_ALIGN)


def _for_each_run(step, lens_ref, fn):
    def expert_body(e, carry):
        k = step * MOE_EXPERTS + e
        nbig = lax.shift_right_logical(lens_ref[k], RUN_BIG_SHIFT)
        rest = nbig * RUN_BIG

        def big_body(q, c):
            fn(k, q * RUN_BIG, RUN_BIG)
            return c

        def small_body(q, c):
            fn(k, rest + q * RUN_ALIGN, RUN_ALIGN)
            return c

        lax.fori_loop(0, nbig, big_body, 0)
        lax.fori_loop(0, lax.shift_right_logical(lens_ref[k] - rest, RUN_SHIFT), small_body, 0)
        return carry

    lax.fori_loop(0, MOE_EXPERTS, expert_body, 0)


def _wait_runs(copy, step, nbig_ref, nsmall_ref):
    def big_body(q, c):
        copy(RUN_BIG).wait()
        return c

    def small_body(q, c):
        copy(RUN_ALIGN).wait()
        return c

    lax.fori_loop(0, nbig_ref[step], big_body, 0)
    lax.fori_loop(0, nsmall_ref[step], small_body, 0)


def _dispatch_kernel(goff_ref, lens_ref, loff_ref, gaps_ref, gapl_ref, nval_ref, nbig_ref, nsmall_ref,
                     h_ref, g_ref, lpos_ref, tok_ref, xs_ref, local, zeros, sems):
    i = pl.program_id(0)
    last = pl.num_programs(0) - 1
    nslots = 2 * DISPATCH_TILES

    def run_copy(s, src, dst, rows):
        return pltpu.make_async_copy(local.at[s, pl.ds(_aligned(src), rows), :],
                                     xs_ref.at[pl.ds(_aligned(dst), rows), :], sems.at[s])

    def wait_tile(tile):
        s = lax.rem(tile, nslots)
        _wait_runs(lambda rows: run_copy(s, 0, 0, rows), tile, nbig_ref, nsmall_ref)

    @pl.when(i >= 2)
    def _():
        for t in range(DISPATCH_TILES):
            wait_tile((i - 2) * DISPATCH_TILES + t)

    riota = lax.broadcasted_iota(jnp.int32, (LOCAL_ROWS, MOE_TM), 0)
    lane = lax.broadcasted_iota(jnp.int32, (MOE_TM, LANES), 1)
    for t in range(DISPATCH_TILES):
        rows_t = slice(t * MOE_TM, (t + 1) * MOE_TM)
        slot = lax.rem(i, 2) * DISPATCH_TILES + t
        hn = _rms(h_ref[rows_t, :], g_ref[...]).astype(BF16)
        sel = jnp.where((riota == lpos_ref[0:1, rows_t]) | (riota == lpos_ref[1:2, rows_t]), 1.0, 0.0).astype(BF16)
        extra = jnp.where((lane < TOK_POS1) | (lane == TOK_EXPERT1), tok_ref[rows_t, :], 0.0).astype(BF16)
        local[slot] = _dot(sel, jnp.concatenate([hn, extra], axis=1)).astype(BF16)

    for t in range(DISPATCH_TILES):
        slot = lax.rem(i, 2) * DISPATCH_TILES + t
        _for_each_run(i * DISPATCH_TILES + t, lens_ref,
                      lambda k, q, rows, slot=slot: run_copy(slot, loff_ref[k] + q, goff_ref[k] + q, rows).start())

    @pl.when(i == last)
    def _():
        zeros[...] = jnp.zeros_like(zeros)
        zsem = sems.at[nslots]

        def gap_copy(dst):
            return pltpu.make_async_copy(zeros.at[pl.ds(0, RUN_ALIGN), :],
                                         xs_ref.at[pl.ds(_aligned(dst), RUN_ALIGN), :], zsem)

        def tile_copy(t):
            return pltpu.make_async_copy(
                zeros, xs_ref.at[pl.ds(pl.multiple_of(t * EXPERT_TILE, EXPERT_TILE), EXPERT_TILE), :], zsem)

        def expert_body(e, n):
            g = lax.shift_right_logical(gapl_ref[e], RUN_SHIFT)

            def body(q, c):
                gap_copy(gaps_ref[e] + q * RUN_ALIGN).start()
                return c

            lax.fori_loop(0, g, body, 0)
            return n + g

        ngap = lax.fori_loop(0, MOE_EXPERTS, expert_body, 0)

        def tail_start(t, c):
            tile_copy(t).start()
            return c

        lax.fori_loop(nval_ref[0], NT_MAX, tail_start, 0)

        def gap_wait(k, c):
            gap_copy(0).wait()
            return c

        lax.fori_loop(0, ngap, gap_wait, 0)

        def tail_wait(t, c):
            tile_copy(t).wait()
            return c

        lax.fori_loop(nval_ref[0], NT_MAX, tail_wait, 0)

        @pl.when(i >= 1)
        def _():
            for t in range(DISPATCH_TILES):
                wait_tile((i - 1) * DISPATCH_TILES + t)

        for t in range(DISPATCH_TILES):
            wait_tile(i * DISPATCH_TILES + t)


def _dispatch(plan, h, gain, lpos, tok):
    row = lambda i, *_: (i, 0)
    rows = DISPATCH_TILES * MOE_TM
    return pl.pallas_call(
        _dispatch_kernel,
        grid_spec=pltpu.PrefetchScalarGridSpec(
            num_scalar_prefetch=8,
            grid=(N_MOE_TILES // DISPATCH_TILES,),
            in_specs=[
                pl.BlockSpec((rows, D_MODEL), row),
                pl.BlockSpec((1, D_MODEL), lambda i, *_: (0, 0)),
                pl.BlockSpec((INT_ROWS, rows), lambda i, *_: (0, i)),
                pl.BlockSpec((rows, LANES), row),
            ],
            out_specs=pl.BlockSpec(memory_space=pl.ANY),
            scratch_shapes=[pltpu.VMEM((2 * DISPATCH_TILES, LOCAL_ROWS, XS_COLS), BF16),
                            pltpu.VMEM((EXPERT_TILE, XS_COLS), BF16),
                            pltpu.SemaphoreType.DMA((2 * DISPATCH_TILES + 1,))],
        ),
        out_shape=jax.ShapeDtypeStruct((NR, XS_COLS), BF16),
        compiler_params=pltpu.CompilerParams(
            dimension_semantics=("arbitrary",), vmem_limit_bytes=VMEM_LIMIT),
        name="moe_dispatch",
    )(plan["goff"], plan["lens"], plan["loff"], plan["gap_start"], plan["gap_len"], plan["nval"],
      plan["nbig"], plan["nsmall"], h, gain, lpos, tok)


def _experts_kernel(tidx_ref, texp_ref, nval_ref, xs_ref, wg_ref, wu_ref, wd_ref, ys_ref, wgu16, wd16):
    j = pl.program_id(0)
    active = j < nval_ref[0]
    new_expert = (j == 0) | (texp_ref[j] != texp_ref[jnp.maximum(j - 1, 0)])

    @pl.when(active & new_expert)
    def _():
        wgu16[:, 0:D_EXPERT] = wg_ref[0].astype(BF16)
        wgu16[:, D_EXPERT:2 * D_EXPERT] = wu_ref[0].astype(BF16)
        wd16[...] = wd_ref[0].astype(BF16)

    @pl.when(active)
    def _():
        x = xs_ref[:, 0:D_MODEL]
        extra = xs_ref[:, D_MODEL:XS_COLS].astype(F32)
        lane = lax.broadcasted_iota(jnp.int32, extra.shape, 1)

        def lanes(lo, hi):
            return jnp.sum(jnp.where((lane >= lo) & (lane < hi), extra, 0.0), axis=1, keepdims=True)

        first = lanes(TOK_EXPERT1, TOK_EXPERT1 + 1) == texp_ref[j].astype(F32)
        gate = jnp.where(first, lanes(TOK_GATE1, TOK_GATE2), lanes(TOK_GATE2, TOK_POS1))
        ab = _dot(x, wgu16[...])
        y = _dot((_silu(ab[:, 0:D_EXPERT]) * ab[:, D_EXPERT:2 * D_EXPERT]).astype(BF16), wd16[...])
        ys_ref[...] = (gate * y).astype(BF16)

    @pl.when(jnp.logical_not(active))
    def _():
        ys_ref[...] = jnp.zeros_like(ys_ref)


def _experts(plan, xs, layer, wg, wu, wd):
    tile = lambda j, tidx, texp, nval: (tidx[j], 0)
    expert = lambda j, tidx, texp, nval: (layer, texp[j], 0, 0)
    return pl.pallas_call(
        _experts_kernel,
        grid_spec=pltpu.PrefetchScalarGridSpec(
            num_scalar_prefetch=3,
            grid=(NT_MAX,),
            in_specs=[
                pl.BlockSpec((EXPERT_TILE, XS_COLS), tile),
                pl.BlockSpec((None, 1, D_MODEL, D_EXPERT), expert),
                pl.BlockSpec((None, 1, D_MODEL, D_EXPERT), expert),
                pl.BlockSpec((None, 1, D_EXPERT, D_MODEL), expert),
            ],
            out_specs=pl.BlockSpec((EXPERT_TILE, D_MODEL), lambda j, *_: (j, 0)),
            scratch_shapes=[pltpu.VMEM((D_MODEL, 2 * D_EXPERT), BF16),
                            pltpu.VMEM((D_EXPERT, D_MODEL), BF16)],
        ),
        out_shape=jax.ShapeDtypeStruct((NR, D_MODEL), BF16),
        compiler_params=pltpu.CompilerParams(
            dimension_semantics=("arbitrary",), vmem_limit_bytes=VMEM_LIMIT),
        name="moe_experts",
    )(plan["tidx"], plan["texp"], plan["nval"], xs, wg, wu, wd)


def _combined_rows(goff_ref, lens_ref, loff_ref, nbig_ref, nsmall_ref, h_ref, tok_ref, ys_ref, local, sems):
    i = pl.program_id(0)
    nslots = 2 * COMBINE_TILES

    def run_copy(s, src, dst, rows):
        return pltpu.make_async_copy(ys_ref.at[pl.ds(_aligned(src), rows), :],
                                     local.at[s, pl.ds(_aligned(dst), rows), :], sems.at[s])

    def start_step(step):
        for t in range(COMBINE_TILES):
            tile = step * COMBINE_TILES + t
            s = lax.rem(tile, nslots)
            _for_each_run(tile, lens_ref,
                          lambda k, q, rows, s=s: run_copy(s, goff_ref[k] + q, loff_ref[k] + q, rows).start())

    @pl.when(i == 0)
    def _():
        local[...] = jnp.zeros_like(local)
        start_step(0)

    @pl.when(i < pl.num_programs(0) - 1)
    def _():
        start_step(i + 1)

    for t in range(COMBINE_TILES):
        slot = lax.rem(i, 2) * COMBINE_TILES + t
        _wait_runs(lambda rows, slot=slot: run_copy(slot, 0, 0, rows), i * COMBINE_TILES + t, nbig_ref, nsmall_ref)
    ciota = lax.broadcasted_iota(jnp.int32, (MOE_TM, LOCAL_ROWS), 1)
    out = []
    for t in range(COMBINE_TILES):
        rows_t = slice(t * MOE_TM, (t + 1) * MOE_TM)
        slot = lax.rem(i, 2) * COMBINE_TILES + t
        tok = tok_ref[rows_t, :]
        p1 = tok[:, TOK_POS1:TOK_POS1 + 1].astype(jnp.int32)
        p2 = tok[:, TOK_POS2:TOK_POS2 + 1].astype(jnp.int32)
        sel = jnp.where((ciota == p1) | (ciota == p2), 1.0, 0.0).astype(BF16)
        out.append(h_ref[rows_t, :] + _dot(sel, local[slot]))
    return jnp.concatenate(out, axis=0)


def _combine_pool_kernel(goff_ref, lens_ref, loff_ref, nbig_ref, nsmall_ref, h_ref, tok_ref,
                         g_ref, pw_ref, pb_ref, ps_ref, ys_ref, o_ref, local, mixed, buf, sems):
    mixed[...] = _combined_rows(goff_ref, lens_ref, loff_ref, nbig_ref, nsmall_ref, h_ref, tok_ref, ys_ref,
                                local, sems)
    _pool_rows(pl.program_id(0), mixed, g_ref, pw_ref, pb_ref, ps_ref, o_ref, buf)


def _combine_final_kernel(goff_ref, lens_ref, loff_ref, nbig_ref, nsmall_ref, h_ref, tok_ref, g_ref, ys_ref,
                          o_ref, local, stage, sems, out_sems):
    i = pl.program_id(0)
    last = pl.num_programs(0) - 1
    slot = lax.rem(i, 2)
    per_tile = COMBINE_ROWS // CHUNK

    def for_each_seq_chunk(step, fn):
        s = lax.rem(step, 2)
        for k in range(per_tile):
            chunk = step * per_tile + k
            c = lax.rem(chunk, NCHUNK)
            dst = (lax.div(chunk, NCHUNK) * (SEQ // CHUNK) + c - 1) * CHUNK

            @pl.when(c > 0)
            def _():
                fn(pltpu.make_async_copy(stage.at[s, pl.ds(k * CHUNK, CHUNK), :],
                                         o_ref.at[pl.ds(pl.multiple_of(dst, CHUNK), CHUNK), :],
                                         out_sems.at[s]))

    @pl.when(i >= 2)
    def _():
        for_each_seq_chunk(i - 2, lambda copy: copy.wait())

    out = _combined_rows(goff_ref, lens_ref, loff_ref, nbig_ref, nsmall_ref, h_ref, tok_ref, ys_ref,
                         local, sems)
    stage[slot] = _rms(out, g_ref[...])
    for_each_seq_chunk(i, lambda copy: copy.start())

    @pl.when(i == last)
    def _():
        @pl.when(i >= 1)
        def _():
            for_each_seq_chunk(i - 1, lambda copy: copy.wait())

        for_each_seq_chunk(i, lambda copy: copy.wait())


def _combine(plan, h, tok, ys, pool=None, final_gain=None):
    final = final_gain is not None
    row = lambda i, *_: (i, 0)
    whole = lambda a: pl.BlockSpec(a.shape, lambda i, *_: (0,) * a.ndim)
    extra = [final_gain] if final else list(pool)
    scratch = [pltpu.VMEM((2 * COMBINE_TILES, LOCAL_ROWS, D_MODEL), BF16)]
    if final:
        scratch.append(pltpu.VMEM((2, COMBINE_ROWS, D_MODEL), F32))
    else:
        scratch += [pltpu.VMEM((COMBINE_ROWS, D_MODEL), F32), pltpu.VMEM((CHUNK + POOL_MAX_WINDOW, D_MODEL), F32)]
    sems = [pltpu.SemaphoreType.DMA((2 * COMBINE_TILES,))] + ([pltpu.SemaphoreType.DMA((2,))] if final else [])
    return pl.pallas_call(
        _combine_final_kernel if final else _combine_pool_kernel,
        grid_spec=pltpu.PrefetchScalarGridSpec(
            num_scalar_prefetch=5,
            grid=(N_MOE_TILES // COMBINE_TILES,),
            in_specs=([pl.BlockSpec((COMBINE_ROWS, D_MODEL), row), pl.BlockSpec((COMBINE_ROWS, LANES), row)]
                      + [whole(a) for a in extra] + [pl.BlockSpec(memory_space=pl.ANY)]),
            out_specs=(pl.BlockSpec(memory_space=pl.ANY) if final
                       else pl.BlockSpec((COMBINE_ROWS, D_MODEL), row)),
            scratch_shapes=scratch + sems,
        ),
        out_shape=jax.ShapeDtypeStruct((BATCH * SEQ if final else TP, D_MODEL), F32),
        compiler_params=pltpu.CompilerParams(
            dimension_semantics=("arbitrary",), vmem_limit_bytes=VMEM_LIMIT),
        name="moe_combine_final" if final else "moe_combine_pool",
    )(plan["goff"], plan["lens"], plan["loff"], plan["nbig"], plan["nsmall"], h, tok, *extra, ys)


def _row(v):
    return v.reshape(1, -1).astype(F32)


def _pad_lanes(v):
    return jnp.pad(_row(v), ((0, 0), (0, LANES - v.shape[-1])))


def _pair_blockdiag(w):
    w = w.reshape(LRU_WIDTH // LANES, 2, 64, 64)
    z = jnp.zeros_like(w[:, 0])
    top = jnp.concatenate([w[:, 0], z], axis=2)
    bot = jnp.concatenate([z, w[:, 1]], axis=2)
    return jnp.concatenate([top, bot], axis=1).astype(BF16)


def _moe_layer(h, layer, pool, final_gain, valid, ffn_norm, rgw, rgb, rew, reb, wg, wu, wd):
    wr = jnp.zeros((D_MODEL, LANES), F32)
    wr = wr.at[:, 0:MOE_GROUPS].set(rgw[layer])
    wr = wr.at[:, ROUTER_EXPERT_ROW:ROUTER_EXPERT_ROW + MOE_EXPERTS].set(rew[layer])
    br = jnp.zeros((1, LANES), F32)
    br = br.at[0, 0:MOE_GROUPS].set(rgb[layer])
    br = br.at[0, ROUTER_EXPERT_ROW:ROUTER_EXPERT_ROW + MOE_EXPERTS].set(reb[layer])
    gain = _row(ffn_norm[layer])
    lpos, tok, runlen, runoff = _router(h, valid, gain, wr, br)
    plan = _moe_plan(runlen, runoff)
    xs = _dispatch(plan, h, gain, lpos, tok)
    ys = _experts(plan, xs, layer, wg, wu, wd)
    return _combine(plan, h, tok, ys, pool, final_gain)


def kernel(x, meta_tokens, norm_final, mix_norm_even, w_in, ssd_conv_w, ssd_conv_b, ssd_dt_bias, ssd_a_log, ssd_d, ssd_norm, lru_conv_w, lru_conv_b, lru_w_a, lru_b_a, lru_w_x, lru_b_x, lru_lambda, w_out, mix_norm_odd, pool_w, pool_b, pool_scale, ffn_norm, router_group_w, router_group_b, router_expert_w, router_expert_b, expert_w_gate, expert_w_up, expert_w_down):
    meta_chunk = jnp.concatenate([jnp.zeros((PAD, D_MODEL), F32), meta_tokens.astype(F32)], axis=0)
    rows = jnp.arange(CHUNK)
    meta_chunk = meta_chunk[(rows % 8) * SEG + rows // 8]
    valid = ((jnp.arange(TP, dtype=jnp.int32) % LP) >= PAD).astype(jnp.int32).reshape(1, TP)
    moe_args = (valid, ffn_norm, router_group_w, router_group_b, router_expert_w, router_expert_b,
                expert_w_gate, expert_w_up, expert_w_down)

    wi = w_in[0]
    dt0 = SSD_WIDTH + SSD_CONV_DIM
    w_zx = wi[:, 0:dt0].astype(BF16)
    w_gl = wi[:, dt0 + SSD_HEADS:].astype(BF16)
    w_dt = jnp.pad(wi[:, dt0:dt0 + SSD_HEADS], ((0, 0), (0, LANES - SSD_HEADS))).astype(BF16)
    piece_head = jnp.where(jnp.arange(LANES) < HEAD_PIECES * SSD_HEADS, jnp.arange(LANES) % SSD_HEADS, -1)
    expand = (piece_head[:, None] == (jnp.arange(SSD_WIDTH) // SSD_HEAD_DIM)[None, :]).astype(BF16)
    params = [
        _row(mix_norm_even[0]), w_zx, w_gl, w_dt,
        ssd_conv_w[0].astype(F32), _row(ssd_conv_b[0]), _pad_lanes(ssd_dt_bias[0]), _pad_lanes(ssd_a_log[0]),
        _row(jnp.repeat(ssd_d[0], SSD_HEAD_DIM)), _row(ssd_norm[0]),
        lru_conv_w[0].astype(F32), _row(lru_conv_b[0]),
        jnp.concatenate([_pair_blockdiag(lru_w_a[0]), _pair_blockdiag(lru_w_x[0])], axis=2),
        _row(lru_b_a[0]), _row(lru_b_x[0]), _row(lru_lambda[0]),
        w_out[0].astype(BF16), expand,
    ]
    h = _mixer(x.astype(F32), meta_chunk, params).reshape(TP, D_MODEL)
    pool = (_row(mix_norm_odd[0]), pool_w[0].astype(BF16), _row(pool_b[0]), _row(pool_scale[0]))
    h = _moe_layer(h, 0, pool, None, *moe_args)
    out = _moe_layer(h, 1, None, _row(norm_final), *moe_args)
    return out.reshape(BATCH, SEQ, D_MODEL)
```

```python
import functools
import math

import jax
import jax.numpy as jnp
from jax import lax
from jax.experimental import pallas as pl
from jax.experimental.pallas import tpu as pltpu

F32 = jnp.float32
BF16 = jnp.bfloat16
HIGHEST = lax.Precision.HIGHEST

D_MODEL = 1024
BATCH = 8
SEQ = 2048
N_META = 16
RMS_EPS = 1e-6
CONV_WIDTH = 4
CHUNK = 128
PAD = CHUNK - N_META
LP = PAD + N_META + SEQ
NCHUNK = LP // CHUNK
TP = BATCH * LP

SSD_HEADS = 16
SSD_HEAD_DIM = 64
SSD_WIDTH = 1024
SSD_GROUPS = 2
SSD_STATE = 128
SSD_GROUP_WIDTH = SSD_WIDTH // SSD_GROUPS
SSD_CONV_DIM = SSD_WIDTH + 2 * SSD_GROUPS * SSD_STATE
LRU_WIDTH = 1024
LRU_C = 8.0
LANES = 128
POOL_WINDOWS = (2, 4, 8, 16)
POOL_GROUP_DIM = 256
POOL_MAX_WINDOW = 16
MOE_GROUPS = 4
MOE_PER_GROUP = 4
MOE_EXPERTS = 16
D_EXPERT = 512
ROUTER_EXPERT_ROW = 8

VMEM_LIMIT = 56 * 1024 * 1024


def _dot(a, b, precision=None):
    return jnp.dot(a, b, preferred_element_type=F32, precision=precision)


def _rms(x, gain):
    ms = jnp.mean(x * x, axis=-1, keepdims=True)
    return x * lax.rsqrt(ms + RMS_EPS) * gain


def _silu(x):
    return x * jax.nn.sigmoid(x)


def _softplus(x):
    return jnp.maximum(x, 0.0) + jnp.log1p(jnp.exp(-jnp.abs(x)))


def _row_ids(shape, chunk):
    return lax.broadcasted_iota(jnp.int32, shape, 0) + chunk * CHUNK


MIX_B = 2
N_MIX_ITEMS = (BATCH // MIX_B) * NCHUNK
_Z0, _X0, _G0, _L0, _T0, _PEND = 0, 1024, 2560, 3584, 4608, 4736
HEAD_PIECES = 3
PROJ_PIECE = 256


def _carried(c, state):
    return jnp.where(jnp.full(state.shape, c, jnp.int32) == 0, 0.0, state)


SEG = CHUNK // 8


def _time_ids(shape, axis=0):
    p = lax.broadcasted_iota(jnp.int32, shape, axis)
    return lax.bitwise_and(p, 7) * SEG + lax.shift_right_logical(p, 3)


def _load_interleaved(slab, src):
    cols = []
    for j in range(D_MODEL // LANES):
        slab[j] = src[:, j * LANES:(j + 1) * LANES]
        cols.append(jnp.concatenate([slab[j, pl.ds(k, 8, stride=SEG), :] for k in range(SEG)], axis=0))
    return jnp.concatenate(cols, axis=1)


def _store_time_order(slab, dst, value):
    for j in range(D_MODEL // LANES):
        for k in range(SEG):
            slab[j, pl.ds(k, 8, stride=SEG), :] = value[k * 8:(k + 1) * 8, j * LANES:(j + 1) * LANES]
        dst[:, j * LANES:(j + 1) * LANES] = slab[j]


def _causal_conv(hist, c, x, w_ref, b_ref):
    taps = CONV_WIDTH - 1
    prev = _carried(c, hist[...])
    hist[...] = x[(SEG - taps) * 8:, :]
    first_segment = lax.broadcasted_iota(jnp.int32, (8, x.shape[1]), 0) == 0
    groups = [x[k * 8:(k + 1) * 8, :] for k in range(SEG)]
    wrapped = {i: jnp.where(first_segment, pltpu.roll(prev[(taps - i) * 8:(taps - i + 1) * 8, :], 1, 0),
                            pltpu.roll(groups[SEG - i], 1, 0)) for i in range(1, taps + 1)}
    out = []
    for k in range(SEG):
        acc = b_ref[...] + w_ref[taps:taps + 1, :] * groups[k]
        for j in range(1, taps + 1):
            earlier = groups[k - j] if k >= j else wrapped[j - k]
            acc = acc + w_ref[taps - j:taps - j + 1, :] * earlier
        out.append(acc)
    return jnp.concatenate(out, axis=0)


def _head_pieces(v):
    lane = lax.broadcasted_iota(jnp.int32, v.shape, 1)
    rest = jnp.where(lane < SSD_HEADS, v, 0.0)
    packed = None
    for k in range(HEAD_PIECES):
        piece = rest.astype(BF16).astype(F32)
        rest = rest - piece
        moved = piece if k == 0 else pltpu.roll(piece, k * SSD_HEADS, 1)
        packed = moved if packed is None else packed + moved
    return packed.astype(BF16)


def _expand_heads(v, exp_ref):
    return _dot(_head_pieces(v), exp_ref[...])


def _ssd_chunk(c, z, xbc, dt_raw, hist, st, cw_ref, cb_ref, dtb_ref, alog_ref, dsk_ref, ng_ref, exp_ref):
    xc = _silu(_causal_conv(hist, c, xbc, cw_ref, cb_ref))
    yield
    xs = xc[:, 0:SSD_WIDTH]
    valid_s = _time_ids((CHUNK, 2 * SSD_STATE)) + c * CHUNK >= PAD
    bm = jnp.where(valid_s, xc[:, SSD_WIDTH:SSD_WIDTH + 2 * SSD_STATE], 0.0)
    cm = jnp.where(valid_s, xc[:, SSD_WIDTH + 2 * SSD_STATE:], 0.0)

    li = _time_ids((CHUNK, CHUNK), 0)
    si = _time_ids((CHUNK, CHUNK), 1)
    causal = si <= li
    dt = _softplus(dt_raw + dtb_ref[...])
    dt = jnp.where(li + c * CHUNK >= PAD, dt, 0.0)
    adt = dt * (-jnp.exp(alog_ref[...]))
    a_cs = _dot(jnp.where(causal, 1.0, 0.0), adt, HIGHEST)
    a_last = a_cs[CHUNK - 1:CHUNK, :]
    ea = jnp.exp(a_cs)
    dt_x = _expand_heads(dt, exp_ref)
    w_x = _expand_heads(jnp.exp(a_last - a_cs) * dt, exp_ref)
    ea_x = _expand_heads(ea, exp_ref)
    xdt = xs * dt_x
    a_cs_t = a_cs.T
    yield
    lane = lax.broadcasted_iota(jnp.int32, (CHUNK, LANES), 1)

    ys = []
    for g in range(SSD_GROUPS):
        gsl = slice(g * SSD_GROUP_WIDTH, (g + 1) * SSD_GROUP_WIDTH)
        bg = bm[:, g * SSD_STATE:(g + 1) * SSD_STATE]
        cg16 = cm[:, g * SSD_STATE:(g + 1) * SSD_STATE].astype(BF16)
        cbm = lax.dot_general(cg16, bg.astype(BF16), (((1,), (1,)), ((), ())),
                              preferred_element_type=F32)
        s_in = _carried(c, st[:, gsl])
        y_off = _dot(cg16, s_in.astype(BF16)) * ea_x[:, gsl]
        s_new = _dot(bg.T.astype(BF16), (w_x[:, gsl] * xs[:, gsl]).astype(BF16))
        st[:, gsl] = ea_x[CHUNK - 1:CHUNK, gsl] * s_in + s_new
        for j in range(SSD_GROUP_WIDTH // LANES):
            xp = xdt[:, g * SSD_GROUP_WIDTH + j * LANES:
                     g * SSD_GROUP_WIDTH + (j + 1) * LANES].astype(BF16)
            weights, values = [], []
            for hh in range(LANES // SSD_HEAD_DIM):
                hd = g * (SSD_HEADS // SSD_GROUPS) + j * (LANES // SSD_HEAD_DIM) + hh
                seg = a_cs[:, hd:hd + 1] - a_cs_t[hd:hd + 1, :]
                dec = jnp.exp(jnp.where(causal, seg, -1e30))
                weights.append((cbm * dec).astype(BF16))
                own = (lane >= hh * SSD_HEAD_DIM) & (lane < (hh + 1) * SSD_HEAD_DIM)
                values.append(jnp.where(own, xp, jnp.zeros_like(xp)))
            y_diag = _dot(jnp.concatenate(weights, axis=1), jnp.concatenate(values, axis=0))
            ys.append(y_diag + y_off[:, j * LANES:(j + 1) * LANES])
            if j % 2:
                yield
    y = jnp.concatenate(ys, axis=1) + dsk_ref[...] * xs
    y = y * _silu(z)
    normed = []
    for g in range(SSD_GROUPS):
        yg = y[:, g * SSD_GROUP_WIDTH:(g + 1) * SSD_GROUP_WIDTH]
        normed.append(yg * lax.rsqrt(jnp.mean(yg * yg, axis=-1, keepdims=True) + RMS_EPS))
    return jnp.concatenate(normed, axis=1) * ng_ref[...]


def _lru_chunk(c, gt, lin, hist, lc, lcw_ref, lcb_ref, wax_ref, ba_ref, bx_ref, lam_ref):
    xb = _causal_conv(hist, c, lin, lcw_ref, lcb_ref)
    yield
    xb16 = xb.astype(BF16)
    both = [_dot(xb16[:, j * LANES:(j + 1) * LANES], wax_ref[j]) for j in range(LRU_WIDTH // LANES)]
    r = jax.nn.sigmoid(jnp.concatenate([p[:, 0:LANES] for p in both], axis=1) + ba_ref[...])
    ig = jax.nn.sigmoid(jnp.concatenate([p[:, LANES:2 * LANES] for p in both], axis=1) + bx_ref[...])
    log_a = (-LRU_C * _softplus(-lam_ref[...])) * r
    a = jnp.exp(log_a)
    mult = jnp.sqrt(jnp.tanh(-log_a) * (a * a + 1.0))
    grow = _time_ids((CHUNK, LRU_WIDTH)) + c * CHUNK
    mult = jnp.where(grow == PAD, 1.0, mult)
    u = jnp.where(grow >= PAD, mult * (ig * xb), 0.0)
    yield

    local, decay = [], []
    for k in range(SEG):
        a8 = a[k * 8:(k + 1) * 8, :]
        u8 = u[k * 8:(k + 1) * 8, :]
        local.append(u8 if k == 0 else a8 * local[-1] + u8)
        decay.append(a8 if k == 0 else a8 * decay[-1])
        if k == SEG // 2:
            yield
    r8 = lax.broadcasted_iota(jnp.int32, (8, LRU_WIDTH), 0)
    end, span = local[-1], decay[-1]
    for d in (1, 2, 4):
        end_sh = jnp.where(r8 >= d, pltpu.roll(end, d, 0), 0.0)
        span_sh = jnp.where(r8 >= d, pltpu.roll(span, d, 0), 1.0)
        end = span * end_sh + end
        span = span * span_sh
    carry = _carried(c, lc[0:1, :])
    end = end + span * carry
    lc[0:1, :] = end[7:8, :]
    entering = jnp.where(r8 >= 1, pltpu.roll(end, 1, 0), carry)
    hs = [local[k] + decay[k] * entering for k in range(SEG)]
    gelu = 0.5 * gt * (1.0 + jnp.tanh(math.sqrt(2.0 / math.pi) * (gt + 0.044715 * (gt * gt * gt))))
    return jnp.concatenate(hs, axis=0) * gelu


def _mixer_kernel(x_ref, meta_ref, gin_ref, wzx_ref, wgl_ref, wdt_ref,
                  cw_ref, cb_ref, dtb_ref, alog_ref, dsk_ref, ng_ref,
                  lcw_ref, lcb_ref, wax_ref, ba_ref, bx_ref, lam_ref,
                  wout_ref, exp_ref, o_ref, h_cur, h_next, p_cur, p_next, hist_x, hist_l, st, lc,
                  slab_in, slab_out):
    s = pl.program_id(0)
    p_chunk = lax.rem(jnp.minimum(s, N_MIX_ITEMS - 1), NCHUNK)
    c = lax.rem(jnp.maximum(s - 1, 0), NCHUNK)

    @pl.when(s == 0)
    def _():
        h_next[...] = jnp.zeros_like(h_next)
        p_next[...] = jnp.zeros_like(p_next)
        hist_x[...] = jnp.zeros_like(hist_x)
        hist_l[...] = jnp.zeros_like(hist_l)
        st[...] = jnp.zeros_like(st)
        lc[...] = jnp.zeros_like(lc)

    h_cur[...] = h_next[...]
    p_cur[...] = p_next[...]

    from_meta = jnp.full((CHUNK, D_MODEL), p_chunk, jnp.int32) == 0
    for b in range(MIX_B):
        h_next[b * CHUNK:(b + 1) * CHUNK, :] = jnp.where(
            from_meta, meta_ref[...], _load_interleaved(slab_in, x_ref.at[b]))
    hn = _rms(h_next[...], gin_ref[...]).astype(BF16)
    pieces = [(w_ref, dst, lo, min(lo + PROJ_PIECE, w_ref.shape[0]))
              for w_ref, dst in ((wzx_ref, _Z0), (wgl_ref, _G0), (wdt_ref, _T0))
              for lo in range(0, w_ref.shape[0], PROJ_PIECE)]

    def project_piece():
        if pieces:
            w_ref, dst, lo, hi = pieces.pop(0)
            p_next[:, dst + lo:dst + hi] = lax.dot_general(
                hn, w_ref[lo:hi, :], (((1,), (1,)), ((), ())), preferred_element_type=F32)

    def rows(b):
        return slice(b * CHUNK, (b + 1) * CHUNK)

    ssd = [_ssd_chunk(c, p_cur[rows(b), _Z0:_X0], p_cur[rows(b), _X0:_G0], p_cur[rows(b), _T0:_PEND],
                      hist_x.at[b], st.at[b], cw_ref, cb_ref, dtb_ref, alog_ref, dsk_ref, ng_ref, exp_ref)
           for b in range(MIX_B)]
    lru = [_lru_chunk(c, p_cur[rows(b), _G0:_L0], p_cur[rows(b), _L0:_T0], hist_l.at[b], lc.at[b],
                      lcw_ref, lcb_ref, wax_ref, ba_ref, bx_ref, lam_ref)
           for b in range(MIX_B)]

    def finish(stages):
        while True:
            try:
                next(stages)
            except StopIteration as done:
                return done.value
            project_piece()

    h = h_cur[...]
    project_piece()
    mixed = []
    for b in range(MIX_B):
        y_ssd = finish(ssd[b])
        y_lru = finish(lru[b])
        mixed.append(jnp.concatenate([y_ssd, y_lru], axis=1).astype(BF16))
    while pieces:
        project_piece()
    out = _dot(jnp.concatenate(mixed, axis=0), wout_ref[...]) + h
    keep = _time_ids((CHUNK, D_MODEL)) + c * CHUNK >= PAD
    for b in range(MIX_B):
        _store_time_order(slab_out, o_ref.at[b], jnp.where(keep, out[b * CHUNK:(b + 1) * CHUNK, :], 0.0))


def _mixer(x, meta_chunk, params):
    full = lambda a: pl.BlockSpec(a.shape, lambda s: (0,) * a.ndim, pipeline_mode=pl.Buffered(1))

    def x_block(s):
        item = jnp.minimum(s, N_MIX_ITEMS - 1)
        return item // NCHUNK, jnp.maximum(item % NCHUNK - 1, 0), 0

    def out_block(s):
        item = jnp.maximum(s - 1, 0)
        return item // NCHUNK, item % NCHUNK, 0

    return pl.pallas_call(
        _mixer_kernel,
        grid=(N_MIX_ITEMS + 1,),
        in_specs=[pl.BlockSpec((MIX_B, CHUNK, D_MODEL), x_block), full(meta_chunk)] + [full(p) for p in params],
        out_specs=pl.BlockSpec((MIX_B, CHUNK, D_MODEL), out_block),
        out_shape=jax.ShapeDtypeStruct((BATCH, LP, D_MODEL), F32),
        scratch_shapes=[
            pltpu.VMEM((MIX_B * CHUNK, D_MODEL), F32),
            pltpu.VMEM((MIX_B * CHUNK, D_MODEL), F32),
            pltpu.VMEM((MIX_B * CHUNK, _PEND), F32),
            pltpu.VMEM((MIX_B * CHUNK, _PEND), F32),
            pltpu.VMEM((MIX_B, 8 * (CONV_WIDTH - 1), SSD_CONV_DIM), F32),
            pltpu.VMEM((MIX_B, 8 * (CONV_WIDTH - 1), LRU_WIDTH), F32),
            pltpu.VMEM((MIX_B, SSD_STATE, SSD_WIDTH), F32),
            pltpu.VMEM((MIX_B, 8, LRU_WIDTH), F32),
            pltpu.VMEM((D_MODEL // LANES, CHUNK, LANES), F32),
            pltpu.VMEM((D_MODEL // LANES, CHUNK, LANES), F32),
        ],
        compiler_params=pltpu.CompilerParams(
            dimension_semantics=("arbitrary",), vmem_limit_bytes=VMEM_LIMIT),
        name="ssd_lru_mixer",
    )(x, meta_chunk, *params)


def _pool_rows(step, h_ref, g_ref, pw_ref, pb_ref, ps_ref, o_ref, buf):
    base = POOL_MAX_WINDOW
    for k in range(h_ref.shape[0] // CHUNK):
        chunk = step * (h_ref.shape[0] // CHUNK) + k
        c = lax.rem(chunk, NCHUNK)

        @pl.when(c == 0)
        def _():
            buf[0:base, :] = jnp.zeros((base, D_MODEL), F32)

        h = h_ref[k * CHUNK:(k + 1) * CHUNK, :]
        hn = _rms(h, g_ref[...])
        buf[base:base + CHUNK, :] = hn
        pos = _row_ids((CHUNK, 1), c) - PAD
        outs = []
        for g, w in enumerate(POOL_WINDOWS):
            sl = slice(g * POOL_GROUP_DIM, (g + 1) * POOL_GROUP_DIM)
            ws = buf[:, sl]
            s = 1
            while s < w:
                ws = ws + pltpu.roll(ws, s, 0)
                s *= 2
            ws = ws[base:, :]
            count = jnp.clip(pos + 1, 1, w).astype(F32)
            pooled = ws * (1.0 / count) - hn[:, sl]
            outs.append(_dot(pooled.astype(BF16), pw_ref[g]))
        buf[0:base, :] = buf[CHUNK:CHUNK + base, :]
        y = (jnp.concatenate(outs, axis=1) + pb_ref[...]) * ps_ref[...]
        o_ref[k * CHUNK:(k + 1) * CHUNK, :] = jnp.where(_row_ids((CHUNK, D_MODEL), c) >= PAD, h + y, 0.0)


MOE_TM = 512
N_MOE_TILES = TP // MOE_TM
DISPATCH_TILES = 2
ROUTER_TILES = 2
COMBINE_TILES = 2
COMBINE_ROWS = COMBINE_TILES * MOE_TM
RUN_ALIGN = 16
RUN_SHIFT = 4
RUN_BIG = 64
RUN_BIG_SHIFT = 6
LOCAL_ROWS = 1280
EXPERT_TILE = 512
N_ROUTED = BATCH * (N_META + SEQ)
MAX_ROWS = (2 * N_ROUTED + N_MOE_TILES * MOE_EXPERTS * (RUN_ALIGN - 1)
            + MOE_EXPERTS * (EXPERT_TILE - 1))
NT_MAX = -(-MAX_ROWS // EXPERT_TILE)
NR = NT_MAX * EXPERT_TILE
XS_COLS = D_MODEL + LANES
INT_ROWS = 8
TOK_GATE1, TOK_GATE2, TOK_POS1, TOK_POS2, TOK_EXPERT1 = 0, 3, 6, 7, 8
assert LOCAL_ROWS >= 2 * MOE_TM + MOE_EXPERTS * (RUN_ALIGN - 1) and LOCAL_ROWS % LANES == 0


def _first_argmax(vals):
    best, idx = vals[0], jnp.zeros(vals[0].shape, jnp.int32)
    for k in range(1, len(vals)):
        better = vals[k] > best
        idx = jnp.where(better, k, idx)
        best = jnp.where(better, vals[k], best)
    return idx, best


def _softmax_rows(vals):
    m = functools.reduce(jnp.maximum, vals)
    ex = [jnp.exp(v - m) for v in vals]
    tot = functools.reduce(lambda p, q: p + q, ex)
    return [e / tot for e in ex]


def _bf16_pieces(x):
    hi = x.astype(BF16).astype(F32)
    rest = x - hi
    mid = rest.astype(BF16).astype(F32)
    lo = (rest - mid).astype(BF16).astype(F32)
    return [hi, mid, lo]


def _router_kernel(h_ref, valid_ref, g_ref, wr_ref, br_ref, lpos_ref, tok_ref, runlen_ref, runoff_ref, before):
    @pl.when(pl.program_id(0) == 0)
    def _():
        si = lax.broadcasted_iota(jnp.int32, (MOE_TM, MOE_TM), 0)
        ti = lax.broadcasted_iota(jnp.int32, (MOE_TM, MOE_TM), 1)
        before[...] = jnp.where(si < ti, 1.0, 0.0).astype(BF16)

    for t in range(ROUTER_TILES):
        tokens = pl.ds(t * MOE_TM, MOE_TM)
        experts = pl.ds(t * MOE_EXPERTS, MOE_EXPERTS)
        _route_tile(h_ref.at[tokens, :], valid_ref.at[:, tokens], g_ref, wr_ref, br_ref,
                    lpos_ref.at[:, tokens], tok_ref.at[tokens, :], runlen_ref.at[experts, :],
                    runoff_ref.at[experts, :], before)


def _route_tile(h_ref, valid_ref, g_ref, wr_ref, br_ref, lpos_ref, tok_ref, runlen_ref, runoff_ref, before):
    hn = _rms(h_ref[...], g_ref[...])
    hn_hi = hn.astype(BF16)
    hn_lo = (hn - hn_hi.astype(F32)).astype(BF16)
    by_head = _dot(hn_hi, wr_ref[...])
    logits = (by_head[:, 0:LANES] + (by_head[:, LANES:2 * LANES] + _dot(hn_lo, wr_ref[:, 0:LANES]))
              + br_ref[...])
    lt = logits.T
    p_group = _softmax_rows([lt[k:k + 1, :] for k in range(MOE_GROUPS)])
    g_sel, p_g = _first_argmax(p_group)
    fine = []
    for k in range(MOE_PER_GROUP):
        f = lt[ROUTER_EXPERT_ROW + k:ROUTER_EXPERT_ROW + k + 1, :]
        for g in range(1, MOE_GROUPS):
            r0 = ROUTER_EXPERT_ROW + g * MOE_PER_GROUP + k
            f = jnp.where(g_sel == g, lt[r0:r0 + 1, :], f)
        fine.append(f)
    q = _softmax_rows(fine)
    i1, t1 = _first_argmax(q)
    i2, t2 = _first_argmax([jnp.where(i1 == k, -1.0, q[k]) for k in range(MOE_PER_GROUP)])
    tot = t1 + t2
    gate1 = p_g * (t1 / tot)
    gate2 = p_g * (t2 / tot)

    valid = valid_ref[...] > 0
    e1 = jnp.where(valid, g_sel * MOE_PER_GROUP + i1, -1)
    e2 = jnp.where(valid, g_sel * MOE_PER_GROUP + i2, -1)
    erow = lax.broadcasted_iota(jnp.int32, (MOE_EXPERTS, MOE_TM), 0)
    hit1 = erow == e1
    hit2 = erow == e2
    onehot = jnp.where(hit1 | hit2, 1.0, 0.0)
    seen = _dot(onehot.astype(BF16), before[...])
    count = jnp.sum(onehot, axis=1, keepdims=True).astype(jnp.int32)
    runlen = lax.shift_left(lax.shift_right_logical(count + (RUN_ALIGN - 1), RUN_SHIFT), RUN_SHIFT)
    runlen_b = jnp.broadcast_to(runlen, (MOE_EXPERTS, LANES))
    ei = lax.broadcasted_iota(jnp.int32, (MOE_EXPERTS, MOE_EXPERTS), 0)
    ej = lax.broadcasted_iota(jnp.int32, (MOE_EXPERTS, MOE_EXPERTS), 1)
    runoff_b = _dot(jnp.where(ej < ei, 1.0, 0.0), runlen_b.astype(F32), HIGHEST)
    runlen_ref[...] = runlen_b
    runoff_ref[...] = runoff_b.astype(jnp.int32)
    place = seen + runoff_b[:, 0:1]
    pos1 = jnp.where(valid, jnp.sum(jnp.where(hit1, place, 0.0), axis=0, keepdims=True), -1.0)
    pos2 = jnp.where(valid, jnp.sum(jnp.where(hit2, place, 0.0), axis=0, keepdims=True), -1.0)
    r8 = lax.broadcasted_iota(jnp.int32, (INT_ROWS, MOE_TM), 0)
    lpos_ref[...] = jnp.where(r8 == 0, pos1.astype(jnp.int32),
                              jnp.where(r8 == 1, pos2.astype(jnp.int32), 0))

    rows = lax.broadcasted_iota(jnp.int32, lt.shape, 0)
    table = jnp.zeros(lt.shape, F32)
    for k, piece in enumerate(_bf16_pieces(gate1)):
        table = jnp.where(rows == TOK_GATE1 + k, piece, table)
    for k, piece in enumerate(_bf16_pieces(gate2)):
        table = jnp.where(rows == TOK_GATE2 + k, piece, table)
    table = jnp.where(rows == TOK_POS1, pos1, jnp.where(rows == TOK_POS2, pos2, table))
    table = jnp.where(rows == TOK_EXPERT1, e1.astype(F32), table)
    tok_ref[...] = table.T


def _router(h, valid, gain, wr, br):
    row = lambda i: (i, 0)
    col = lambda i: (0, i)
    const = lambda i: (0, 0)
    wr_hi = wr.astype(BF16)
    wr_lo = (wr - wr_hi.astype(F32)).astype(BF16)
    return pl.pallas_call(
        _router_kernel,
        grid=(N_MOE_TILES // ROUTER_TILES,),
        in_specs=[
            pl.BlockSpec((ROUTER_TILES * MOE_TM, D_MODEL), row),
            pl.BlockSpec((1, ROUTER_TILES * MOE_TM), col),
            pl.BlockSpec((1, D_MODEL), const),
            pl.BlockSpec((D_MODEL, 2 * LANES), const),
            pl.BlockSpec((1, LANES), const),
        ],
        out_specs=[pl.BlockSpec((INT_ROWS, ROUTER_TILES * MOE_TM), col),
                   pl.BlockSpec((ROUTER_TILES * MOE_TM, LANES), row),
                   pl.BlockSpec((ROUTER_TILES * MOE_EXPERTS, LANES), row),
                   pl.BlockSpec((ROUTER_TILES * MOE_EXPERTS, LANES), row)],
        out_shape=[jax.ShapeDtypeStruct((INT_ROWS, TP), jnp.int32),
                   jax.ShapeDtypeStruct((TP, LANES), F32),
                   jax.ShapeDtypeStruct((N_MOE_TILES * MOE_EXPERTS, LANES), jnp.int32),
                   jax.ShapeDtypeStruct((N_MOE_TILES * MOE_EXPERTS, LANES), jnp.int32)],
        scratch_shapes=[pltpu.VMEM((MOE_TM, MOE_TM), BF16)],
        compiler_params=pltpu.CompilerParams(
            dimension_semantics=("arbitrary",), vmem_limit_bytes=VMEM_LIMIT),
        name="moe_router",
    )(h, valid, gain, jnp.concatenate([wr_hi, wr_lo], axis=1), br)


def _moe_plan(runlen, runoff):
    i32 = jnp.int32
    lens = runlen[:, 0].reshape(N_MOE_TILES, MOE_EXPERTS)
    total = jnp.sum(lens, axis=0)
    padded = (total + (EXPERT_TILE - 1)) // EXPERT_TILE * EXPERT_TILE
    ends = jnp.cumsum(padded)
    base = ends - padded
    goff = base[None, :] + jnp.cumsum(lens, axis=0) - lens
    nval = ends[-1] // EXPERT_TILE
    tidx = jnp.minimum(jnp.arange(NT_MAX, dtype=i32), nval - 1)
    texp = jnp.minimum(jnp.sum(tidx[:, None] >= (ends // EXPERT_TILE)[None, :], axis=1), MOE_EXPERTS - 1)
    return dict(
        goff=goff.reshape(-1).astype(i32), lens=lens.reshape(-1).astype(i32),
        loff=runoff[:, 0].astype(i32), gap_start=(base + total).astype(i32),
        gap_len=(padded - total).astype(i32), nval=nval.reshape(1).astype(i32),
        tidx=tidx.astype(i32), texp=texp.astype(i32),
        nbig=jnp.sum(lens // RUN_BIG, axis=1).astype(i32),
        nsmall=jnp.sum(lens % RUN_BIG // RUN_ALIGN, axis=1).astype(i32))


def _aligned(x):
    return pl.multiple_of(x, RUN_ALIGN)


def _for_each_run(step, lens_ref, fn):
    def expert_body(e, carry):
        k = step * MOE_EXPERTS + e
        nbig = lax.shift_right_logical(lens_ref[k], RUN_BIG_SHIFT)
        rest = nbig * RUN_BIG

        def big_body(q, c):
            fn(k, q * RUN_BIG, RUN_BIG)
            return c

        def small_body(q, c):
            fn(k, rest + q * RUN_ALIGN, RUN_ALIGN)
            return c

        lax.fori_loop(0, nbig, big_body, 0)
        lax.fori_loop(0, lax.shift_right_logical(lens_ref[k] - rest, RUN_SHIFT), small_body, 0)
        return carry

    lax.fori_loop(0, MOE_EXPERTS, expert_body, 0)


def _wait_runs(copy, step, nbig_ref, nsmall_ref):
    def big_body(q, c):
        copy(RUN_BIG).wait()
        return c

    def small_body(q, c):
        copy(RUN_ALIGN).wait()
        return c

    lax.fori_loop(0, nbig_ref[step], big_body, 0)
    lax.fori_loop(0, nsmall_ref[step], small_body, 0)


def _dispatch_kernel(goff_ref, lens_ref, loff_ref, gaps_ref, gapl_ref, nval_ref, nbig_ref, nsmall_ref,
                     h_ref, g_ref, lpos_ref, tok_ref, xs_ref, local, zeros, sems):
    i = pl.program_id(0)
    last = pl.num_programs(0) - 1
    nslots = 2 * DISPATCH_TILES

    def run_copy(s, src, dst, rows):
        return pltpu.make_async_copy(local.at[s, pl.ds(_aligned(src), rows), :],
                                     xs_ref.at[pl.ds(_aligned(dst), rows), :], sems.at[s])

    def wait_tile(tile):
        s = lax.rem(tile, nslots)
        _wait_runs(lambda rows: run_copy(s, 0, 0, rows), tile, nbig_ref, nsmall_ref)

    @pl.when(i >= 2)
    def _():
        for t in range(DISPATCH_TILES):
            wait_tile((i - 2) * DISPATCH_TILES + t)

    riota = lax.broadcasted_iota(jnp.int32, (LOCAL_ROWS, MOE_TM), 0)
    lane = lax.broadcasted_iota(jnp.int32, (MOE_TM, LANES), 1)
    for t in range(DISPATCH_TILES):
        rows_t = slice(t * MOE_TM, (t + 1) * MOE_TM)
        slot = lax.rem(i, 2) * DISPATCH_TILES + t
        hn = _rms(h_ref[rows_t, :], g_ref[...]).astype(BF16)
        sel = jnp.where((riota == lpos_ref[0:1, rows_t]) | (riota == lpos_ref[1:2, rows_t]), 1.0, 0.0).astype(BF16)
        extra = jnp.where((lane < TOK_POS1) | (lane == TOK_EXPERT1), tok_ref[rows_t, :], 0.0).astype(BF16)
        local[slot] = _dot(sel, jnp.concatenate([hn, extra], axis=1)).astype(BF16)

    for t in range(DISPATCH_TILES):
        slot = lax.rem(i, 2) * DISPATCH_TILES + t
        _for_each_run(i * DISPATCH_TILES + t, lens_ref,
                      lambda k, q, rows, slot=slot: run_copy(slot, loff_ref[k] + q, goff_ref[k] + q, rows).start())

    @pl.when(i == last)
    def _():
        zeros[...] = jnp.zeros_like(zeros)
        zsem = sems.at[nslots]

        def gap_copy(dst):
            return pltpu.make_async_copy(zeros.at[pl.ds(0, RUN_ALIGN), :],
                                         xs_ref.at[pl.ds(_aligned(dst), RUN_ALIGN), :], zsem)

        def tile_copy(t):
            return pltpu.make_async_copy(
                zeros, xs_ref.at[pl.ds(pl.multiple_of(t * EXPERT_TILE, EXPERT_TILE), EXPERT_TILE), :], zsem)

        def expert_body(e, n):
            g = lax.shift_right_logical(gapl_ref[e], RUN_SHIFT)

            def body(q, c):
                gap_copy(gaps_ref[e] + q * RUN_ALIGN).start()
                return c

            lax.fori_loop(0, g, body, 0)
            return n + g

        ngap = lax.fori_loop(0, MOE_EXPERTS, expert_body, 0)

        def tail_start(t, c):
            tile_copy(t).start()
            return c

        lax.fori_loop(nval_ref[0], NT_MAX, tail_start, 0)

        def gap_wait(k, c):
            gap_copy(0).wait()
            return c

        lax.fori_loop(0, ngap, gap_wait, 0)

        def tail_wait(t, c):
            tile_copy(t).wait()
            return c

        lax.fori_loop(nval_ref[0], NT_MAX, tail_wait, 0)

        @pl.when(i >= 1)
        def _():
            for t in range(DISPATCH_TILES):
                wait_tile((i - 1) * DISPATCH_TILES + t)

        for t in range(DISPATCH_TILES):
            wait_tile(i * DISPATCH_TILES + t)


def _dispatch(plan, h, gain, lpos, tok):
    row = lambda i, *_: (i, 0)
    rows = DISPATCH_TILES * MOE_TM
    return pl.pallas_call(
        _dispatch_kernel,
        grid_spec=pltpu.PrefetchScalarGridSpec(
            num_scalar_prefetch=8,
            grid=(N_MOE_TILES // DISPATCH_TILES,),
            in_specs=[
                pl.BlockSpec((rows, D_MODEL), row),
                pl.BlockSpec((1, D_MODEL), lambda i, *_: (0, 0)),
                pl.BlockSpec((INT_ROWS, rows), lambda i, *_: (0, i)),
                pl.BlockSpec((rows, LANES), row),
            ],
            out_specs=pl.BlockSpec(memory_space=pl.ANY),
            scratch_shapes=[pltpu.VMEM((2 * DISPATCH_TILES, LOCAL_ROWS, XS_COLS), BF16),
                            pltpu.VMEM((EXPERT_TILE, XS_COLS), BF16),
                            pltpu.SemaphoreType.DMA((2 * DISPATCH_TILES + 1,))],
        ),
        out_shape=jax.ShapeDtypeStruct((NR, XS_COLS), BF16),
        compiler_params=pltpu.CompilerParams(
            dimension_semantics=("arbitrary",), vmem_limit_bytes=VMEM_LIMIT),
        name="moe_dispatch",
    )(plan["goff"], plan["lens"], plan["loff"], plan["gap_start"], plan["gap_len"], plan["nval"],
      plan["nbig"], plan["nsmall"], h, gain, lpos, tok)


def _experts_kernel(tidx_ref, texp_ref, nval_ref, xs_ref, wg_ref, wu_ref, wd_ref, ys_ref, wgu16, wd16):
    j = pl.program_id(0)
    active = j < nval_ref[0]
    new_expert = (j == 0) | (texp_ref[j] != texp_ref[jnp.maximum(j - 1, 0)])

    @pl.when(active & new_expert)
    def _():
        wgu16[:, 0:D_EXPERT] = wg_ref[0].astype(BF16)
        wgu16[:, D_EXPERT:2 * D_EXPERT] = wu_ref[0].astype(BF16)
        wd16[...] = wd_ref[0].astype(BF16)

    @pl.when(active)
    def _():
        x = xs_ref[:, 0:D_MODEL]
        extra = xs_ref[:, D_MODEL:XS_COLS].astype(F32)
        lane = lax.broadcasted_iota(jnp.int32, extra.shape, 1)

        def lanes(lo, hi):
            return jnp.sum(jnp.where((lane >= lo) & (lane < hi), extra, 0.0), axis=1, keepdims=True)

        first = lanes(TOK_EXPERT1, TOK_EXPERT1 + 1) == texp_ref[j].astype(F32)
        gate = jnp.where(first, lanes(TOK_GATE1, TOK_GATE2), lanes(TOK_GATE2, TOK_POS1))
        ab = _dot(x, wgu16[...])
        y = _dot((_silu(ab[:, 0:D_EXPERT]) * ab[:, D_EXPERT:2 * D_EXPERT]).astype(BF16), wd16[...])
        ys_ref[...] = (gate * y).astype(BF16)

    @pl.when(jnp.logical_not(active))
    def _():
        ys_ref[...] = jnp.zeros_like(ys_ref)


def _experts(plan, xs, layer, wg, wu, wd):
    tile = lambda j, tidx, texp, nval: (tidx[j], 0)
    expert = lambda j, tidx, texp, nval: (layer, texp[j], 0, 0)
    return pl.pallas_call(
        _experts_kernel,
        grid_spec=pltpu.PrefetchScalarGridSpec(
            num_scalar_prefetch=3,
            grid=(NT_MAX,),
            in_specs=[
                pl.BlockSpec((EXPERT_TILE, XS_COLS), tile),
                pl.BlockSpec((None, 1, D_MODEL, D_EXPERT), expert),
                pl.BlockSpec((None, 1, D_MODEL, D_EXPERT), expert),
                pl.BlockSpec((None, 1, D_EXPERT, D_MODEL), expert),
            ],
            out_specs=pl.BlockSpec((EXPERT_TILE, D_MODEL), lambda j, *_: (j, 0)),
            scratch_shapes=[pltpu.VMEM((D_MODEL, 2 * D_EXPERT), BF16),
                            pltpu.VMEM((D_EXPERT, D_MODEL), BF16)],
        ),
        out_shape=jax.ShapeDtypeStruct((NR, D_MODEL), BF16),
        compiler_params=pltpu.CompilerParams(
            dimension_semantics=("arbitrary",), vmem_limit_bytes=VMEM_LIMIT),
        name="moe_experts",
    )(plan["tidx"], plan["texp"], plan["nval"], xs, wg, wu, wd)


def _combined_rows(goff_ref, lens_ref, loff_ref, nbig_ref, nsmall_ref, h_ref, tok_ref, ys_ref, local, sems):
    i = pl.program_id(0)
    nslots = 2 * COMBINE_TILES

    def run_copy(s, src, dst, rows):
        return pltpu.make_async_copy(ys_ref.at[pl.ds(_aligned(src), rows), :],
                                     local.at[s, pl.ds(_aligned(dst), rows), :], sems.at[s])

    def start_step(step):
        for t in range(COMBINE_TILES):
            tile = step * COMBINE_TILES + t
            s = lax.rem(tile, nslots)
            _for_each_run(tile, lens_ref,
                          lambda k, q, rows, s=s: run_copy(s, goff_ref[k] + q, loff_ref[k] + q, rows).start())

    @pl.when(i == 0)
    def _():
        local[...] = jnp.zeros_like(local)
        start_step(0)

    @pl.when(i < pl.num_programs(0) - 1)
    def _():
        start_step(i + 1)

    for t in range(COMBINE_TILES):
        slot = lax.rem(i, 2) * COMBINE_TILES + t
        _wait_runs(lambda rows, slot=slot: run_copy(slot, 0, 0, rows), i * COMBINE_TILES + t, nbig_ref, nsmall_ref)
    ciota = lax.broadcasted_iota(jnp.int32, (MOE_TM, LOCAL_ROWS), 1)
    out = []
    for t in range(COMBINE_TILES):
        rows_t = slice(t * MOE_TM, (t + 1) * MOE_TM)
        slot = lax.rem(i, 2) * COMBINE_TILES + t
        tok = tok_ref[rows_t, :]
        p1 = tok[:, TOK_POS1:TOK_POS1 + 1].astype(jnp.int32)
        p2 = tok[:, TOK_POS2:TOK_POS2 + 1].astype(jnp.int32)
        sel = jnp.where((ciota == p1) | (ciota == p2), 1.0, 0.0).astype(BF16)
        out.append(h_ref[rows_t, :] + _dot(sel, local[slot]))
    return jnp.concatenate(out, axis=0)


def _combine_pool_kernel(goff_ref, lens_ref, loff_ref, nbig_ref, nsmall_ref, h_ref, tok_ref,
                         g_ref, pw_ref, pb_ref, ps_ref, ys_ref, o_ref, local, mixed, buf, sems):
    mixed[...] = _combined_rows(goff_ref, lens_ref, loff_ref, nbig_ref, nsmall_ref, h_ref, tok_ref, ys_ref,
                                local, sems)
    _pool_rows(pl.program_id(0), mixed, g_ref, pw_ref, pb_ref, ps_ref, o_ref, buf)


def _combine_final_kernel(goff_ref, lens_ref, loff_ref, nbig_ref, nsmall_ref, h_ref, tok_ref, g_ref, ys_ref,
                          o_ref, local, stage, sems, out_sems):
    i = pl.program_id(0)
    last = pl.num_programs(0) - 1
    slot = lax.rem(i, 2)
    per_tile = COMBINE_ROWS // CHUNK

    def for_each_seq_chunk(step, fn):
        s = lax.rem(step, 2)
        for k in range(per_tile):
            chunk = step * per_tile + k
            c = lax.rem(chunk, NCHUNK)
            dst = (lax.div(chunk, NCHUNK) * (SEQ // CHUNK) + c - 1) * CHUNK

            @pl.when(c > 0)
            def _():
                fn(pltpu.make_async_copy(stage.at[s, pl.ds(k * CHUNK, CHUNK), :],
                                         o_ref.at[pl.ds(pl.multiple_of(dst, CHUNK), CHUNK), :],
                                         out_sems.at[s]))

    @pl.when(i >= 2)
    def _():
        for_each_seq_chunk(i - 2, lambda copy: copy.wait())

    out = _combined_rows(goff_ref, lens_ref, loff_ref, nbig_ref, nsmall_ref, h_ref, tok_ref, ys_ref,
                         local, sems)
    stage[slot] = _rms(out, g_ref[...])
    for_each_seq_chunk(i, lambda copy: copy.start())

    @pl.when(i == last)
    def _():
        @pl.when(i >= 1)
        def _():
            for_each_seq_chunk(i - 1, lambda copy: copy.wait())

        for_each_seq_chunk(i, lambda copy: copy.wait())


def _combine(plan, h, tok, ys, pool=None, final_gain=None):
    final = final_gain is not None
    row = lambda i, *_: (i, 0)
    whole = lambda a: pl.BlockSpec(a.shape, lambda i, *_: (0,) * a.ndim)
    extra = [final_gain] if final else list(pool)
    scratch = [pltpu.VMEM((2 * COMBINE_TILES, LOCAL_ROWS, D_MODEL), BF16)]
    if final:
        scratch.append(pltpu.VMEM((2, COMBINE_ROWS, D_MODEL), F32))
    else:
        scratch += [pltpu.VMEM((COMBINE_ROWS, D_MODEL), F32), pltpu.VMEM((CHUNK + POOL_MAX_WINDOW, D_MODEL), F32)]
    sems = [pltpu.SemaphoreType.DMA((2 * COMBINE_TILES,))] + ([pltpu.SemaphoreType.DMA((2,))] if final else [])
    return pl.pallas_call(
        _combine_final_kernel if final else _combine_pool_kernel,
        grid_spec=pltpu.PrefetchScalarGridSpec(
            num_scalar_prefetch=5,
            grid=(N_MOE_TILES // COMBINE_TILES,),
            in_specs=([pl.BlockSpec((COMBINE_ROWS, D_MODEL), row), pl.BlockSpec((COMBINE_ROWS, LANES), row)]
                      + [whole(a) for a in extra] + [pl.BlockSpec(memory_space=pl.ANY)]),
            out_specs=(pl.BlockSpec(memory_space=pl.ANY) if final
                       else pl.BlockSpec((COMBINE_ROWS, D_MODEL), row)),
            scratch_shapes=scratch + sems,
        ),
        out_shape=jax.ShapeDtypeStruct((BATCH * SEQ if final else TP, D_MODEL), F32),
        compiler_params=pltpu.CompilerParams(
            dimension_semantics=("arbitrary",), vmem_limit_bytes=VMEM_LIMIT),
        name="moe_combine_final" if final else "moe_combine_pool",
    )(plan["goff"], plan["lens"], plan["loff"], plan["nbig"], plan["nsmall"], h, tok, *extra, ys)


def _row(v):
    return v.reshape(1, -1).astype(F32)


def _pad_lanes(v):
    return jnp.pad(_row(v), ((0, 0), (0, LANES - v.shape[-1])))


def _pair_blockdiag(w):
    w = w.reshape(LRU_WIDTH // LANES, 2, 64, 64)
    z = jnp.zeros_like(w[:, 0])
    top = jnp.concatenate([w[:, 0], z], axis=2)
    bot = jnp.concatenate([z, w[:, 1]], axis=2)
    return jnp.concatenate([top, bot], axis=1).astype(BF16)


def _moe_layer(h, layer, pool, final_gain, valid, ffn_norm, rgw, rgb, rew, reb, wg, wu, wd):
    wr = jnp.zeros((D_MODEL, LANES), F32)
    wr = wr.at[:, 0:MOE_GROUPS].set(rgw[layer])
    wr = wr.at[:, ROUTER_EXPERT_ROW:ROUTER_EXPERT_ROW + MOE_EXPERTS].set(rew[layer])
    br = jnp.zeros((1, LANES), F32)
    br = br.at[0, 0:MOE_GROUPS].set(rgb[layer])
    br = br.at[0, ROUTER_EXPERT_ROW:ROUTER_EXPERT_ROW + MOE_EXPERTS].set(reb[layer])
    gain = _row(ffn_norm[layer])
    lpos, tok, runlen, runoff = _router(h, valid, gain, wr, br)
    plan = _moe_plan(runlen, runoff)
    xs = _dispatch(plan, h, gain, lpos, tok)
    ys = _experts(plan, xs, layer, wg, wu, wd)
    return _combine(plan, h, tok, ys, pool, final_gain)


def kernel(x, meta_tokens, norm_final, mix_norm_even, w_in, ssd_conv_w, ssd_conv_b, ssd_dt_bias, ssd_a_log, ssd_d, ssd_norm, lru_conv_w, lru_conv_b, lru_w_a, lru_b_a, lru_w_x, lru_b_x, lru_lambda, w_out, mix_norm_odd, pool_w, pool_b, pool_scale, ffn_norm, router_group_w, router_group_b, router_expert_w, router_expert_b, expert_w_gate, expert_w_up, expert_w_down):
    meta_chunk = jnp.concatenate([jnp.zeros((PAD, D_MODEL), F32), meta_tokens.astype(F32)], axis=0)
    rows = jnp.arange(CHUNK)
    meta_chunk = meta_chunk[(rows % 8) * SEG + rows // 8]
    valid = ((jnp.arange(TP, dtype=jnp.int32) % LP) >= PAD).astype(jnp.int32).reshape(1, TP)
    moe_args = (valid, ffn_norm, router_group_w, router_group_b, router_expert_w, router_expert_b,
                expert_w_gate, expert_w_up, expert_w_down)

    wi = jnp.swapaxes(w_in[0], 0, 1)
    dt0 = SSD_WIDTH + SSD_CONV_DIM
    w_zx = wi[0:dt0].astype(BF16)
    w_gl = wi[dt0 + SSD_HEADS:].astype(BF16)
    w_dt = jnp.pad(wi[dt0:dt0 + SSD_HEADS], ((0, LANES - SSD_HEADS), (0, 0))).astype(BF16)
    piece_head = jnp.where(jnp.arange(LANES) < HEAD_PIECES * SSD_HEADS, jnp.arange(LANES) % SSD_HEADS, -1)
    expand = (piece_head[:, None] == (jnp.arange(SSD_WIDTH) // SSD_HEAD_DIM)[None, :]).astype(BF16)
    params = [
        _row(mix_norm_even[0]), w_zx, w_gl, w_dt,
        ssd_conv_w[0].astype(F32), _row(ssd_conv_b[0]), _pad_lanes(ssd_dt_bias[0]), _pad_lanes(ssd_a_log[0]),
        _row(jnp.repeat(ssd_d[0], SSD_HEAD_DIM)), _row(ssd_norm[0]),
        lru_conv_w[0].astype(F32), _row(lru_conv_b[0]),
        jnp.concatenate([_pair_blockdiag(lru_w_a[0]), _pair_blockdiag(lru_w_x[0])], axis=2),
        _row(lru_b_a[0]), _row(lru_b_x[0]), _row(lru_lambda[0]),
        w_out[0].astype(BF16), expand,
    ]
    h = _mixer(x.astype(F32), meta_chunk, params).reshape(TP, D_MODEL)
    pool = (_row(mix_norm_odd[0]), pool_w[0].astype(BF16), _row(pool_b[0]), _row(pool_scale[0]))
    h = _moe_layer(h, 0, pool, None, *moe_args)
    out = _moe_layer(h, 1, None, _row(norm_final), *moe_args)
    return out.reshape(BATCH, SEQ, D_MODEL)
```
